```python
import math
import jax
import jax.numpy as jnp
from jax import lax
import numpy as np

D_MODEL = 1024
BATCH = 2
SEQ = 8192
DEPTH = 2
DEC_BATCH = 32
DEC_SEQ = 4
PAST_LEN = 16384
PAGE_SIZE = 128

N_HEADS = 8
HEAD_DIM = 64
N_KV = 2
HPG = N_HEADS // N_KV
CMP_STRIDE = 16
CMP_LEN = 2 * CMP_STRIDE
SLC_BLOCK = 64
SLC_TOPK = 16
WINDOW = 512
Q_BLOCK = 128
SSM_WIDTH = D_MODEL // 2
SSM_GROUP = 16
N_SSM_GROUPS = SSM_WIDTH // SSM_GROUP
SSM_STATE = 64
N_MEM = 256
MEM_HEADS = 4
MEM_HEAD_DIM = 128
MEM_WIDTH = MEM_HEADS * MEM_HEAD_DIM
D_FF = 4 * D_MODEL
NSA_WIDTH = N_HEADS * HEAD_DIM
KV_WIDTH = N_KV * HEAD_DIM
MIX_WIDTH = NSA_WIDTH + SSM_WIDTH + MEM_WIDTH
OFF_KV = NSA_WIDTH
OFF_GN = OFF_KV + 6 * KV_WIDTH
OFF_U = OFF_GN + 3 * N_HEADS
OFF_MQ = OFF_U + SSM_WIDTH
OFF_GB = OFF_MQ + MEM_WIDTH
IN_WIDTH = OFF_GB + MIX_WIDTH
EPS = 1e-6
NEG_INF = -1e30
FORCED_SCORE = 1e9

kernel_name = 'hybrid_nsa_s5_memory_decode_step'


def rmsnorm(x, g):
    xf = x.astype(jnp.float32)
    y = xf * lax.rsqrt(jnp.mean(xf * xf, axis=-1, keepdims=True) + EPS)
    return (y * g.astype(jnp.float32)).astype(x.dtype)


def alibi_slopes(n):
    return jnp.exp2(-8.0 * (jnp.arange(n, dtype=jnp.float32) + 1.0) / n)


def masked_probs(s, mask):
    s = jnp.where(mask, s, NEG_INF)
    m = jnp.max(s, axis=-1, keepdims=True)
    e = jnp.where(mask, jnp.exp(s - m), 0.0)
    return e / jnp.maximum(jnp.sum(e, axis=-1, keepdims=True), 1e-30)


def in_proj(h, w_in):
    B, T, _ = h.shape
    z = h @ w_in
    q = z[..., :OFF_KV].reshape(B, T, N_HEADS, HEAD_DIM)
    kv = z[..., OFF_KV:OFF_GN].reshape(B, T, 3, 2, N_KV, HEAD_DIM)
    gn = jax.nn.sigmoid(z[..., OFF_GN:OFF_U]).reshape(B, T, 3, N_HEADS)
    u = z[..., OFF_U:OFF_MQ]
    mq = z[..., OFF_MQ:OFF_GB].reshape(B, T, MEM_HEADS, MEM_HEAD_DIM)
    gb = jax.nn.sigmoid(z[..., OFF_GB:])
    return q, kv, gn, u, mq, gb


def compress(rows, pe, w1, w2):
    B, L, G, dh = rows.shape
    ch = rows.reshape(B, L // CMP_STRIDE, CMP_STRIDE, G, dh)
    lo = jnp.einsum('bnlgd,lde->bgne', ch + pe[:CMP_STRIDE, None, :], w1[:CMP_STRIDE])
    hi = jnp.einsum('bnlgd,lde->bgne', ch + pe[CMP_STRIDE:, None, :], w1[CMP_STRIDE:])
    return jax.nn.gelu(lo[:, :, :-1] + hi[:, :, 1:]) @ w2


def nsa_block(q, g, pos_q, ck, cv, kb, vb, kw, vw, pos_w):
    B, G, Hg, Tq, dh = q.shape
    f32 = jnp.float32
    scale = dh ** -0.5
    slope = alibi_slopes(N_HEADS).reshape(1, G, Hg, 1, 1)
    n_c = ck.shape[2]
    c_end = jnp.arange(n_c, dtype=jnp.int32) * CMP_STRIDE + (CMP_LEN - 1)
    d_c = pos_q[:, None] - c_end[None, :]
    s = jnp.einsum('bghtd,bgnd->bghtn', q, ck, preferred_element_type=f32) * scale - slope * d_c.astype(f32)
    p_c = masked_probs(s, d_c >= 0)
    o_c = jnp.einsum('bghtn,bgnd->bghtd', p_c.astype(cv.dtype), cv)
    n_b = kb.shape[2]
    per = SLC_BLOCK // CMP_STRIDE
    imp = jnp.pad(jnp.sum(p_c, axis=2), ((0, 0), (0, 0), (0, 0), (0, n_b * per - n_c)))
    imp = imp.reshape(B, G, Tq, n_b, per).sum(-1)
    blk = jnp.arange(n_b, dtype=jnp.int32)[None, :]
    cur = (pos_q // SLC_BLOCK)[:, None]
    valid = blk <= cur
    forced = (blk == 0) | (blk == cur) | (blk == cur - 1)
    score = jnp.where(valid, jnp.where(forced, FORCED_SCORE, imp), -1.0)
    _, idx = lax.top_k(score, min(SLC_TOPK, n_b))
    bi = jnp.arange(B)[:, None, None, None]
    gi = jnp.arange(G)[None, :, None, None]
    ks = kb[bi, gi, idx].reshape(B, G, Tq, -1, dh)
    vs = vb[bi, gi, idx].reshape(B, G, Tq, -1, dh)
    kpos = (idx[..., None] * SLC_BLOCK + jnp.arange(SLC_BLOCK, dtype=jnp.int32)).reshape(B, G, Tq, -1)
    d_s = (pos_q[None, None, :, None] - kpos)[:, :, None]
    s = jnp.einsum('bghtd,bgtnd->bghtn', q, ks, preferred_element_type=f32) * scale - slope * d_s.astype(f32)
    p_s = masked_probs(s, d_s >= 0)
    o_s = jnp.einsum('bghtn,bgtnd->bghtd', p_s.astype(vs.dtype), vs)
    d_w = pos_q[:, None] - pos_w[None, :]
    s = jnp.einsum('bghtd,bgld->bghtl', q, kw, preferred_element_type=f32) * scale - slope * d_w.astype(f32)
    p_w = masked_probs(s, (d_w >= 0) & (d_w <= WINDOW) & (pos_w >= 0)[None, :])
    o_w = jnp.einsum('bghtl,bgld->bghtd', p_w.astype(vw.dtype), vw)
    gate = g.transpose(2, 0, 3, 1).reshape(3, B, G, Hg, Tq)[..., None]
    o = gate[0] * o_c + gate[1] * o_s + gate[2] * o_w
    return o.transpose(0, 3, 1, 2, 4).reshape(B, Tq, G * Hg * dh)


def nsa_prompt(q, kv, gn, p):
    B, T = q.shape[:2]
    ck = compress(kv[:, :, 0, 0], p['cmp_pe'][0], p['cmp_w1'][0], p['cmp_w2'][0])
    cv = compress(kv[:, :, 0, 1], p['cmp_pe'][1], p['cmp_w1'][1], p['cmp_w2'][1])
    n_b = T // SLC_BLOCK
    kb = kv[:, :, 1, 0].transpose(0, 2, 1, 3).reshape(B, N_KV, n_b, SLC_BLOCK, HEAD_DIM)
    vb = kv[:, :, 1, 1].transpose(0, 2, 1, 3).reshape(B, N_KV, n_b, SLC_BLOCK, HEAD_DIM)
    pad = ((0, 0), (0, 0), (WINDOW, 0), (0, 0))
    kw = jnp.pad(kv[:, :, 2, 0].transpose(0, 2, 1, 3), pad)
    vw = jnp.pad(kv[:, :, 2, 1].transpose(0, 2, 1, 3), pad)
    nq = T // Q_BLOCK
    qb = q.reshape(B, nq, Q_BLOCK, N_KV, HPG, HEAD_DIM).transpose(1, 0, 3, 4, 2, 5)
    gb = gn.reshape(B, nq, Q_BLOCK, 3, N_HEADS).transpose(1, 0, 2, 3, 4)

    def body(args):
        j, qj, gj = args
        start = j * Q_BLOCK
        pos_q = start + jnp.arange(Q_BLOCK, dtype=jnp.int32)
        kwj = lax.dynamic_slice_in_dim(kw, start, WINDOW + Q_BLOCK, axis=2)
        vwj = lax.dynamic_slice_in_dim(vw, start, WINDOW + Q_BLOCK, axis=2)
        pos_w = start - WINDOW + jnp.arange(WINDOW + Q_BLOCK, dtype=jnp.int32)
        return nsa_block(qj, gj, pos_q, ck, cv, kb, vb, kwj, vwj, pos_w)

    o = lax.map(body, (jnp.arange(nq, dtype=jnp.int32), qb, gb))
    return o.transpose(1, 0, 2, 3).reshape(B, T, NSA_WIDTH)


def nsa_sample(q, kv, gn, cmp_pool, slc_pool, win_buf, page_table, p):
    B, S = q.shape[:2]
    past = page_table.shape[1] * PAGE_SIZE
    L = past + S
    l_pad = -(-L // SLC_BLOCK) * SLC_BLOCK

    def paged_rows(pool, new):
        old = pool[page_table].reshape((B, past) + pool.shape[2:])
        rows = jnp.concatenate([old, new.astype(old.dtype)], axis=1)
        return jnp.pad(rows, ((0, 0), (0, l_pad - L), (0, 0), (0, 0), (0, 0)))

    rc = paged_rows(cmp_pool, kv[:, :, 0])
    ck = compress(rc[:, :, 0], p['cmp_pe'][0], p['cmp_w1'][0], p['cmp_w2'][0])
    cv = compress(rc[:, :, 1], p['cmp_pe'][1], p['cmp_w1'][1], p['cmp_w2'][1])
    rs = paged_rows(slc_pool, kv[:, :, 1])
    n_b = l_pad // SLC_BLOCK
    kb = rs[:, :, 0].transpose(0, 2, 1, 3).reshape(B, N_KV, n_b, SLC_BLOCK, HEAD_DIM)
    vb = rs[:, :, 1].transpose(0, 2, 1, 3).reshape(B, N_KV, n_b, SLC_BLOCK, HEAD_DIM)
    wrows = jnp.concatenate([win_buf, kv[:, :, 2].astype(win_buf.dtype)], axis=1)
    wb = win_buf.shape[1]
    pos_w = past - wb + jnp.arange(wb + S, dtype=jnp.int32)
    kw = wrows[:, :, 0].transpose(0, 2, 1, 3)
    vw = wrows[:, :, 1].transpose(0, 2, 1, 3)
    pos_q = past + jnp.arange(S, dtype=jnp.int32)
    qh = q.reshape(B, S, N_KV, HPG, HEAD_DIM).transpose(0, 2, 3, 1, 4)
    o = nsa_block(qh, gn, pos_q, ck, cv, kb, vb, kw, vw, pos_w)
    return o, wrows[:, S:]


def complex_affine_combine(e1, e2):
    a1r, a1i, b1r, b1i = e1
    a2r, a2i, b2r, b2i = e2
    ar = a1r * a2r - a1i * a2i
    ai = a1r * a2i + a1i * a2r
    br = a2r * b1r - a2i * b1i + b2r
    bi = a2r * b1i + a2i * b1r + b2i
    return ar, ai, br, bi


def ssm_discretise(p):
    f32 = jnp.float32
    ar = p['a_re'].astype(f32)
    ai = p['a_im'].astype(f32)
    dt = jnp.exp(p['log_dt'].astype(f32))[:, None]
    mag = jnp.exp(dt * ar)
    abr = mag * jnp.cos(dt * ai)
    abi = mag * jnp.sin(dt * ai)
    den = ar * ar + ai * ai
    fr = ((abr - 1.0) * ar + abi * ai) / den
    fi = (abi * ar - (abr - 1.0) * ai) / den
    br = p['b_re'].astype(f32)
    bi = p['b_im'].astype(f32)
    bbr = fr[..., None] * br - fi[..., None] * bi
    bbi = fr[..., None] * bi + fi[..., None] * br
    return abr, abi, bbr, bbi


def ssm_branch(u, h0_re, h0_im, p):
    B, T, _ = u.shape
    f32 = jnp.float32
    abr, abi, bbr, bbi = ssm_discretise(p)
    uf = u.astype(f32).reshape(B, T, N_SSM_GROUPS, SSM_GROUP)
    bu_r = jnp.einsum('btgc,gpc->tbgp', uf, bbr)
    bu_i = jnp.einsum('btgc,gpc->tbgp', uf, bbi)
    h0r = h0_re.astype(f32)
    h0i = h0_im.astype(f32)
    bu_r = bu_r.at[0].add(abr * h0r - abi * h0i)
    bu_i = bu_i.at[0].add(abr * h0i + abi * h0r)
    a_r = jnp.broadcast_to(abr, bu_r.shape)
    a_i = jnp.broadcast_to(abi, bu_i.shape)
    _, _, h_r, h_i = lax.associative_scan(complex_affine_combine, (a_r, a_i, bu_r, bu_i), axis=0)
    y = (jnp.einsum('tbgp,gcp->btgc', h_r, p['c_re'].astype(f32))
         - jnp.einsum('tbgp,gcp->btgc', h_i, p['c_im'].astype(f32))
         + p['d'].astype(f32) * uf)
    y = jax.nn.gelu(y.reshape(B, T, SSM_WIDTH)).astype(u.dtype)
    o = y * jax.nn.sigmoid(y @ p['w_glu'] + p['b_glu'])
    return o, h_r[-1].astype(h0_re.dtype), h_i[-1].astype(h0_re.dtype)


def memory_kv(mem, g, w):
    B, M, _ = mem.shape
    return (rmsnorm(mem, g) @ w).reshape(B, M, 2, MEM_HEADS, MEM_HEAD_DIM)


def memory_attend(mq, mkv):
    B, T = mq.shape[:2]
    s = jnp.einsum('bthd,bmhd->bhtm', mq, mkv[:, :, 0], preferred_element_type=jnp.float32) * (MEM_HEAD_DIM ** -0.5)
    pr = jax.nn.softmax(s, axis=-1)
    o = jnp.einsum('bhtm,bmhd->bthd', pr.astype(mkv.dtype), mkv[:, :, 1])
    return o.reshape(B, T, MEM_WIDTH)


def merge(x, o_a, o_b, o_c, gb, w_o):
    mixed = jnp.concatenate([o_a.astype(x.dtype), o_b.astype(x.dtype), o_c.astype(x.dtype)], axis=-1) * gb
    return x + mixed @ w_o


def ffn(x, g, w1, w2):
    h = rmsnorm(x, g)
    return x + jnp.square(jax.nn.relu(h @ w1)) @ w2


def setup_inputs(seed: int = 0) -> dict:
    key = jax.random.key(seed)
    ks = jax.random.split(key, 40)
    f32 = jnp.float32

    def nrm(i, shape, scale=1.0):
        return jax.random.normal(ks[i], shape, f32) * scale

    n_pages = PAST_LEN // PAGE_SIZE
    n_phys = (DEC_BATCH * n_pages * 5) // 4
    win_buf = min(WINDOW, PAST_LEN)
    page_table = jax.random.permutation(ks[0], n_phys)[:DEC_BATCH * n_pages].reshape(DEC_BATCH, n_pages).astype(jnp.int32)
    a_im = math.pi * jnp.arange(SSM_STATE, dtype=f32)[None, None, :] + nrm(17, (DEPTH, N_SSM_GROUPS, SSM_STATE), 0.01)
    return {
        'x_prompt': nrm(1, (BATCH, SEQ, D_MODEL)),
        'x_sample': nrm(2, (DEC_BATCH, DEC_SEQ, D_MODEL)),
        'cache_cmp_kv': nrm(3, (DEPTH, n_phys, PAGE_SIZE, 2, N_KV, HEAD_DIM)),
        'cache_slc_kv': nrm(4, (DEPTH, n_phys, PAGE_SIZE, 2, N_KV, HEAD_DIM)),
        'cache_win_kv': nrm(5, (DEPTH, DEC_BATCH, win_buf, 2, N_KV, HEAD_DIM)),
        'state_ssm_re': nrm(6, (DEPTH, DEC_BATCH, N_SSM_GROUPS, SSM_STATE), 0.5),
        'state_ssm_im': nrm(7, (DEPTH, DEC_BATCH, N_SSM_GROUPS, SSM_STATE), 0.5),
        'cache_mem_kv': nrm(8, (DEPTH, DEC_BATCH, N_MEM, 2, MEM_HEADS, MEM_HEAD_DIM)),
        'page_table': page_table,
        'mem_prompt': nrm(9, (BATCH, N_MEM, D_MODEL)),
        'norm1_g': 1.0 + nrm(10, (DEPTH, D_MODEL), 0.02),
        'w_in': nrm(11, (DEPTH, D_MODEL, IN_WIDTH), D_MODEL ** -0.5),
        'cmp_pe': nrm(12, (DEPTH, 2, CMP_LEN, HEAD_DIM), 0.1),
        'cmp_w1': nrm(13, (DEPTH, 2, CMP_LEN, HEAD_DIM, HEAD_DIM), (CMP_LEN * HEAD_DIM) ** -0.5),
        'cmp_w2': nrm(14, (DEPTH, 2, HEAD_DIM, HEAD_DIM), HEAD_DIM ** -0.5),
        'ssm_a_re': -0.5 + nrm(15, (DEPTH, N_SSM_GROUPS, SSM_STATE), 0.01),
        'ssm_a_im': a_im,
        'ssm_log_dt': jax.random.uniform(ks[16], (DEPTH, N_SSM_GROUPS), f32, math.log(1e-3), math.log(1e-1)),
        'ssm_b_re': nrm(18, (DEPTH, N_SSM_GROUPS, SSM_STATE, SSM_GROUP), (2.0 * SSM_GROUP) ** -0.5),
        'ssm_b_im': nrm(19, (DEPTH, N_SSM_GROUPS, SSM_STATE, SSM_GROUP), (2.0 * SSM_GROUP) ** -0.5),
        'ssm_c_re': nrm(20, (DEPTH, N_SSM_GROUPS, SSM_GROUP, SSM_STATE), (2.0 * SSM_STATE) ** -0.5),
        'ssm_c_im': nrm(21, (DEPTH, N_SSM_GROUPS, SSM_GROUP, SSM_STATE), (2.0 * SSM_STATE) ** -0.5),
        'ssm_d': nrm(22, (DEPTH, N_SSM_GROUPS, SSM_GROUP)),
        'w_glu': nrm(23, (DEPTH, SSM_WIDTH, SSM_WIDTH), SSM_WIDTH ** -0.5),
        'b_glu': nrm(24, (DEPTH, SSM_WIDTH), 0.01),
        'mem_norm_g': 1.0 + nrm(25, (DEPTH, D_MODEL), 0.02),
        'w_mem_kv': nrm(26, (DEPTH, D_MODEL, 2 * MEM_WIDTH), D_MODEL ** -0.5),
        'w_o': nrm(27, (DEPTH, MIX_WIDTH, D_MODEL), MIX_WIDTH ** -0.5),
        'norm2_g': 1.0 + nrm(28, (DEPTH, D_MODEL), 0.02),
        'w_ff1': nrm(29, (DEPTH, D_MODEL, D_FF), D_MODEL ** -0.5),
        'w_ff2': nrm(30, (DEPTH, D_FF, D_MODEL), D_FF ** -0.5),
        'final_norm_g': 1.0 + nrm(31, (D_MODEL,), 0.02),
    }


def reference(x_prompt, x_sample, cache_cmp_kv, cache_slc_kv, cache_win_kv, state_ssm_re, state_ssm_im,
              cache_mem_kv, page_table, mem_prompt, norm1_g, w_in, cmp_pe, cmp_w1, cmp_w2,
              ssm_a_re, ssm_a_im, ssm_log_dt, ssm_b_re, ssm_b_im, ssm_c_re, ssm_c_im, ssm_d,
              w_glu, b_glu, mem_norm_g, w_mem_kv, w_o, norm2_g, w_ff1, w_ff2, final_norm_g):
    xp = x_prompt
    xs = x_sample
    cmp_p, cmp_s, slc_p, slc_s, win_p, win_s = [], [], [], [], [], []
    hr_p, hi_p, hr_s, hi_s, mkv_p = [], [], [], [], []
    for l in range(DEPTH):
        p = {'cmp_pe': cmp_pe[l], 'cmp_w1': cmp_w1[l], 'cmp_w2': cmp_w2[l],
             'a_re': ssm_a_re[l], 'a_im': ssm_a_im[l], 'log_dt': ssm_log_dt[l],
             'b_re': ssm_b_re[l], 'b_im': ssm_b_im[l], 'c_re': ssm_c_re[l], 'c_im': ssm_c_im[l],
             'd': ssm_d[l], 'w_glu': w_glu[l], 'b_glu': b_glu[l]}
        h = rmsnorm(xp, norm1_g[l])
        q, kv, gn, u, mq, gb = in_proj(h, w_in[l])
        o_a = nsa_prompt(q, kv, gn, p)
        z0 = jnp.zeros((xp.shape[0], N_SSM_GROUPS, SSM_STATE), jnp.float32)
        o_b, hr, hi = ssm_branch(u, z0, z0, p)
        mkv = memory_kv(mem_prompt, mem_norm_g[l], w_mem_kv[l])
        o_c = memory_attend(mq, mkv)
        xp = ffn(merge(xp, o_a, o_b, o_c, gb, w_o[l]), norm2_g[l], w_ff1[l], w_ff2[l])
        cmp_p.append(kv[:, :, 0])
        slc_p.append(kv[:, :, 1])
        win_p.append(kv[:, -min(WINDOW, xp.shape[1]):, 2])
        hr_p.append(hr)
        hi_p.append(hi)
        mkv_p.append(mkv)
        h = rmsnorm(xs, norm1_g[l])
        q, kv, gn, u, mq, gb = in_proj(h, w_in[l])
        o_a, new_win = nsa_sample(q, kv, gn, cache_cmp_kv[l], cache_slc_kv[l], cache_win_kv[l], page_table, p)
        o_b, hr, hi = ssm_branch(u, state_ssm_re[l], state_ssm_im[l], p)
        o_c = memory_attend(mq, cache_mem_kv[l])
        xs = ffn(merge(xs, o_a, o_b, o_c, gb, w_o[l]), norm2_g[l], w_ff1[l], w_ff2[l])
        cmp_s.append(kv[:, :, 0])
        slc_s.append(kv[:, :, 1])
        win_s.append(new_win)
        hr_s.append(hr)
        hi_s.append(hi)
    y_prompt = rmsnorm(xp, final_norm_g)
    y_sample = rmsnorm(xs, final_norm_g)
    return (y_prompt, y_sample,
            jnp.stack(cmp_p), jnp.stack(cmp_s),
            jnp.stack(slc_p), jnp.stack(slc_s),
            jnp.stack(win_p), jnp.stack(win_s),
            jnp.stack(hr_p), jnp.stack(hi_p),
            jnp.stack(hr_s), jnp.stack(hi_s),
            jnp.stack(mkv_p))
```

```python
import functools
import math

import jax
import jax.numpy as jnp
from jax import lax
from jax.experimental import pallas as pl
from jax.experimental.pallas import tpu as pltpu

F32 = jnp.float32
BF16 = jnp.bfloat16

D_MODEL = 1024
PAGE_SIZE = 128
N_HEADS = 8
HEAD_DIM = 64
N_KV = 2
HPG = N_HEADS // N_KV
CMP_STRIDE = 16
CMP_LEN = 2 * CMP_STRIDE
SLC_BLOCK = 64
SLC_TOPK = 16
WINDOW = 512
Q_BLOCK = 128
SSM_WIDTH = D_MODEL // 2
SSM_GROUP = 16
N_SSM_GROUPS = SSM_WIDTH // SSM_GROUP
SSM_STATE = 64
SSM_FLAT = N_SSM_GROUPS * SSM_STATE
N_MEM = 256
MEM_HEADS = 4
MEM_HEAD_DIM = 128
MEM_WIDTH = MEM_HEADS * MEM_HEAD_DIM
D_FF = 4 * D_MODEL
NSA_WIDTH = N_HEADS * HEAD_DIM
KV_WIDTH = N_KV * HEAD_DIM
MIX_WIDTH = NSA_WIDTH + SSM_WIDTH + MEM_WIDTH
OFF_KV = NSA_WIDTH
OFF_GN = OFF_KV + 6 * KV_WIDTH
OFF_U = OFF_GN + 3 * N_HEADS
OFF_MQ = OFF_U + SSM_WIDTH
OFF_GB = OFF_MQ + MEM_WIDTH
IN_WIDTH = OFF_GB + MIX_WIDTH
EPS = 1e-6
NEG_INF = -1e30
FORCED_SCORE = 1e9

LANE = 128
CMP_ROW = CMP_STRIDE * 2 * KV_WIDTH
SSM_GPB = LANE // SSM_GROUP
SSM_SLABS = SSM_WIDTH // LANE
VMEM_LIMIT = 56 * 2**20


def _cparams(n_grid, vmem=VMEM_LIMIT):
    return pltpu.CompilerParams(dimension_semantics=("arbitrary",) * n_grid, vmem_limit_bytes=vmem)


def _rms(x, g):
    return x * lax.rsqrt(jnp.mean(x * x, axis=-1, keepdims=True) + EPS) * g


def _dot(a, b):
    return jnp.dot(a, b, preferred_element_type=F32)


def _dot_nt(a, b):
    return lax.dot_general(a, b, (((1,), (1,)), ((), ())), preferred_element_type=F32)


def _masked_probs(s, mask):
    s = jnp.where(mask, s, NEG_INF)
    m = jnp.max(s, axis=-1, keepdims=True)
    e = jnp.where(mask, jnp.exp(s - m), 0.0)
    return e / jnp.maximum(jnp.sum(e, axis=-1, keepdims=True), 1e-30)


def _row_tile(n, pref):
    t = min(n, pref)
    assert n % t == 0
    return t


_SEGS = ((0, 512), (512, 768), (768, 1024), (1024, 1280), (1280, 1792), (1792, 2304), (2304, 3840), (3840, 3968))
_SIGMOID_FROM = 2304
_PACKED_WIDTH = 3968


def _pack_w_in(w):
    gn = jnp.pad(w[:, OFF_GN:OFF_U], ((0, 0), (0, LANE - 3 * N_HEADS)))
    return jnp.concatenate([w[:, :OFF_GN], w[:, OFF_U:], gn], axis=1).astype(BF16)


def _inproj_kernel(x_ref, g_ref, w_ref, *o_refs):
    h = _rms(x_ref[...], g_ref[...]).astype(BF16)
    for (a, b), o_ref in zip(_SEGS, o_refs):
        z = _dot(h, w_ref[:, a:b])
        if a >= _SIGMOID_FROM:
            z = jax.nn.sigmoid(z)
        o_ref[...] = z


def _in_proj(x2, g, w_packed):
    n = x2.shape[0]
    tm = _row_tile(n, 512)
    return pl.pallas_call(
        _inproj_kernel,
        grid=(n // tm,),
        in_specs=[pl.BlockSpec((tm, D_MODEL), lambda i: (i, 0)),
                  pl.BlockSpec((1, D_MODEL), lambda i: (0, 0)),
                  pl.BlockSpec((D_MODEL, _PACKED_WIDTH), lambda i: (0, 0))],
        out_specs=[pl.BlockSpec((tm, b - a), lambda i: (i, 0)) for a, b in _SEGS],
        out_shape=[jax.ShapeDtypeStruct((n, b - a), F32) for a, b in _SEGS],
        compiler_params=_cparams(1),
        name="in_proj",
    )(x2, g.reshape(1, D_MODEL), w_packed)


def _norm_mm_kernel(x_ref, g_ref, w_ref, o_ref):
    o_ref[...] = _dot(_rms(x_ref[...], g_ref[...]).astype(BF16), w_ref[...])


def _norm_mm(x2, g, w_bf16):
    n, k = x2.shape
    m = w_bf16.shape[1]
    tm = _row_tile(n, 256)
    return pl.pallas_call(
        _norm_mm_kernel,
        grid=(n // tm,),
        in_specs=[pl.BlockSpec((tm, k), lambda i: (i, 0)),
                  pl.BlockSpec((1, k), lambda i: (0, 0)),
                  pl.BlockSpec((k, m), lambda i: (0, 0))],
        out_specs=pl.BlockSpec((tm, m), lambda i: (i, 0)),
        out_shape=jax.ShapeDtypeStruct((n, m), F32),
        compiler_params=_cparams(1),
        name="norm_mm",
    )(x2, g.reshape(1, k), w_bf16)


def _expand_cmp_w1(w1):
    w = w1.reshape(2, 2, CMP_STRIDE, HEAD_DIM, HEAD_DIM)
    eye2 = jnp.eye(2, dtype=w1.dtype)
    big = jnp.einsum('khlde,kK,gG->hlkgdKGe', w, eye2, eye2)
    return big.reshape(2, CMP_ROW, 2 * KV_WIDTH).astype(BF16)


def _flat_cmp_pe(pe):
    p = pe.reshape(2, 2, CMP_STRIDE, HEAD_DIM)
    p = jnp.broadcast_to(p.transpose(1, 2, 0, 3)[:, :, :, None, :], (2, CMP_STRIDE, 2, N_KV, HEAD_DIM))
    return p.reshape(2, 1, CMP_ROW)


def _expand_cmp_w2(w2):
    eye2 = jnp.eye(2, dtype=w2.dtype)
    big = jnp.einsum('kde,kK,gG->kgdGKe', w2, eye2, eye2)
    return big.reshape(2 * KV_WIDTH, 2 * KV_WIDTH).astype(BF16)


def _cmp_lohi(x, pe_ref, w_ref):
    lo = _dot((x + pe_ref[0]).astype(BF16), w_ref[0])
    hi = _dot((x + pe_ref[1]).astype(BF16), w_ref[1])
    return jnp.concatenate([lo, hi], axis=1)


def _cmp_lh_kernel(x_ref, pe_ref, w_ref, o_ref):
    o_ref[...] = _cmp_lohi(x_ref[...], pe_ref, w_ref)


def _cmp_lh(x2, pe_flat, w1_big):
    n = x2.shape[0]
    tm = _row_tile(n, 256)
    return pl.pallas_call(
        _cmp_lh_kernel,
        grid=(n // tm,),
        in_specs=[pl.BlockSpec((tm, CMP_ROW), lambda i: (i, 0)),
                  pl.BlockSpec((2, 1, CMP_ROW), lambda i: (0, 0, 0)),
                  pl.BlockSpec((2, CMP_ROW, 2 * KV_WIDTH), lambda i: (0, 0, 0))],
        out_specs=pl.BlockSpec((tm, 4 * KV_WIDTH), lambda i: (i, 0)),
        out_shape=jax.ShapeDtypeStruct((n, 4 * KV_WIDTH), F32),
        compiler_params=_cparams(1),
        name="cmp_lh",
    )(x2, pe_flat, w1_big)


def _cmp_combine(lh, w2_big):
    r = lh.shape[0]
    hi_next = pltpu.roll(lh[:, 2 * KV_WIDTH:], r - 1, 0)
    return _dot(jax.nn.gelu(lh[:, :2 * KV_WIDTH] + hi_next).astype(BF16), w2_big)


def _cmp_combine_kernel(lh_ref, w2_ref, o_ref):
    o_ref[0] = _cmp_combine(lh_ref[0], w2_ref[...])


def _cmp_tokens_prompt(lh, w2_big):
    b, r, _ = lh.shape
    return pl.pallas_call(
        _cmp_combine_kernel,
        grid=(b,),
        in_specs=[pl.BlockSpec((1, r, 4 * KV_WIDTH), lambda i: (i, 0, 0)),
                  pl.BlockSpec((2 * KV_WIDTH, 2 * KV_WIDTH), lambda i: (0, 0))],
        out_specs=pl.BlockSpec((1, r, 2 * KV_WIDTH), lambda i: (i, 0, 0)),
        out_shape=jax.ShapeDtypeStruct((b, r, 2 * KV_WIDTH), F32),
        compiler_params=_cparams(1),
        name="cmp_combine",
    )(lh, w2_big)


def _alibi_slopes():
    return jnp.exp2(-8.0 * (jnp.arange(N_HEADS, dtype=F32) + 1.0) / N_HEADS)


def _query_pack(q, pos):
    b, t, _ = q.shape
    qh = q.reshape(b, t, N_HEADS, HEAD_DIM).transpose(0, 2, 1, 3) * (HEAD_DIM ** -0.5)
    slope = _alibi_slopes()[:, None]
    pos_hi = ((pos // SLC_BLOCK) * SLC_BLOCK).astype(F32)[None, :]
    pos_lo = (pos % SLC_BLOCK).astype(F32)[None, :]
    ones = jnp.ones((1, t), F32)
    c = jnp.stack([slope * ones, slope * ones, -slope * pos_hi, -slope * pos_lo], axis=-1)
    c = jnp.pad(c, ((0, 0), (0, 0), (0, HEAD_DIM - 4)))
    c = jnp.broadcast_to(c[None], (b, N_HEADS, t, HEAD_DIM))
    return jnp.concatenate([qh, c], axis=-1).astype(BF16)


def _key_consts(kpos):
    hi = ((kpos // SLC_BLOCK) * SLC_BLOCK).astype(F32)
    lo = (kpos % SLC_BLOCK).astype(F32)
    one = jnp.ones_like(hi)
    return jnp.pad(jnp.stack([hi, lo, one, one], axis=-1), ((0, 0), (0, HEAD_DIM - 4)))


def _split3(x):
    hi = x.astype(BF16)
    r1 = x - hi.astype(F32)
    mid = r1.astype(BF16)
    lo = (r1 - mid.astype(F32)).astype(BF16)
    return hi, mid, lo


def _pool_blocks(ps, pool):
    hi, mid, lo = _split3(ps)
    return _dot(hi, pool) + _dot(mid, pool) + _dot(lo, pool)


def _topk_select(score, blk_f, k):
    sel = jnp.zeros(score.shape, F32)
    picks = []
    for _ in range(k):
        m = jnp.max(score, axis=-1, keepdims=True)
        idx = jnp.min(jnp.where(score == m, blk_f, 1e9), axis=-1, keepdims=True)
        pick = blk_f == idx
        sel = jnp.where(pick, 1.0, sel)
        score = jnp.where(pick, -3e38, score)
        picks.append(idx)
    return sel, picks


def _nsa_prompt_kernel(qp_ref, gn_ref, ckv_ref, kp_ref, vw_ref, pool_ref, o_ref, *, tk):
    j = pl.program_id(2)
    start = j * Q_BLOCK
    rows = HPG * Q_BLOCK
    qp = qp_ref[0].reshape(rows, LANE)
    q = qp[:, :HEAD_DIM]
    slope = qp[:, HEAD_DIM:HEAD_DIM + 1].astype(F32)
    row = lax.broadcasted_iota(jnp.int32, (rows, 1), 0)
    pos_q = start + (row & (Q_BLOCK - 1))

    ckv = ckv_ref[0]
    n_c = ckv.shape[0]
    ck = ckv[:, :HEAD_DIM].astype(BF16)
    cv = ckv[:, HEAD_DIM:].astype(BF16)
    ci = lax.broadcasted_iota(jnp.int32, (1, n_c), 1)
    d_c = pos_q - (ci * CMP_STRIDE + (CMP_LEN - 1))
    s = _dot_nt(q, ck) - slope * d_c.astype(F32)
    p_c = _masked_probs(s, d_c >= 0)
    o_c = _dot(p_c.astype(BF16), cv)
    ps = p_c[0:Q_BLOCK]
    for h in range(1, HPG):
        ps = ps + p_c[h * Q_BLOCK:(h + 1) * Q_BLOCK]
    imp = _pool_blocks(ps, pool_ref[...])

    t_i = lax.broadcasted_iota(jnp.int32, (Q_BLOCK, 1), 0)
    blk = lax.broadcasted_iota(jnp.int32, (1, LANE), 1)
    cur = (start + t_i) // SLC_BLOCK
    valid = blk <= cur
    forced = (blk == 0) | (blk == cur) | (blk == cur - 1)
    score = jnp.where(valid, jnp.where(forced, FORCED_SCORE, imp), -1.0)
    sel, _ = _topk_select(score, blk.astype(F32), SLC_TOPK)
    mb = jnp.where((sel > 0.0) & valid, 0.0, NEG_INF).astype(BF16)
    qs = jnp.concatenate([jnp.concatenate([mb] * HPG, axis=0), qp], axis=1)

    def tile(i, carry, diag):
        m, l, acc = carry
        off = pl.multiple_of(i * tk, tk)
        s = _dot_nt(qs, kp_ref[0, 0, pl.ds(off, tk), :])
        if diag:
            kpos = off + lax.broadcasted_iota(jnp.int32, (1, tk), 1)
            s = jnp.where(kpos <= pos_q, s, NEG_INF)
        m2 = jnp.maximum(m, jnp.max(s, axis=-1, keepdims=True))
        a = jnp.exp(m - m2)
        e = jnp.exp(s - m2)
        l = a * l + jnp.sum(e, axis=-1, keepdims=True)
        acc = a * acc + _dot(e.astype(BF16), vw_ref[0, 0, pl.ds(off, tk), 0:HEAD_DIM])
        return m2, l, acc

    n_full = start // tk
    carry = (jnp.full((rows, 1), NEG_INF, F32), jnp.zeros((rows, 1), F32), jnp.zeros((rows, HEAD_DIM), F32))
    carry = lax.fori_loop(0, n_full, lambda i, c: tile(i, c, False), carry)
    _, l_s, acc_s = tile(n_full, carry, True)
    o_s = acc_s / jnp.maximum(l_s, 1e-30)

    wlen = WINDOW + Q_BLOCK
    woff = pl.multiple_of(jnp.maximum(start - WINDOW, 0), Q_BLOCK)
    kw = vw_ref[0, 0, pl.ds(woff, wlen), HEAD_DIM:2 * HEAD_DIM]
    vw = vw_ref[0, 0, pl.ds(woff, wlen), 2 * HEAD_DIM:3 * HEAD_DIM]
    d_w = pos_q - (woff + lax.broadcasted_iota(jnp.int32, (1, wlen), 1))
    s = _dot_nt(q, kw) - slope * d_w.astype(F32)
    p_w = _masked_probs(s, (d_w >= 0) & (d_w <= WINDOW))
    o_w = _dot(p_w.astype(BF16), vw)

    gn = gn_ref[0, 0]
    outs = []
    for h in range(HPG):
        sl = slice(h * Q_BLOCK, (h + 1) * Q_BLOCK)
        outs.append(gn[:, h:h + 1] * o_c[sl] + gn[:, HPG + h:HPG + h + 1] * o_s[sl]
                    + gn[:, 2 * HPG + h:2 * HPG + h + 1] * o_w[sl])
    o_ref[0] = jnp.concatenate(outs, axis=1)


def _gate_pack(gn, b, t):
    g = gn[:, :3 * N_HEADS].reshape(b, t, 3, N_KV, HPG).transpose(0, 3, 1, 2, 4).reshape(b, N_KV, t, 3 * HPG)
    return jnp.pad(g, ((0, 0), (0, 0), (0, 0), (0, 16 - 3 * HPG)))


def _nsa_prompt(q, kvs, kvw, gn, ckv, b, t):
    n_b = t // SLC_BLOCK
    assert n_b <= LANE and t % Q_BLOCK == 0 and t >= WINDOW + Q_BLOCK
    tk = min(512, t)
    pos = jnp.arange(t, dtype=jnp.int32)
    qp = _query_pack(q.reshape(b, t, NSA_WIDTH), pos)
    ks = kvs.reshape(b, t, 2, N_KV, HEAD_DIM).transpose(2, 0, 3, 1, 4)
    kw = kvw.reshape(b, t, 2, N_KV, HEAD_DIM).transpose(2, 0, 3, 1, 4)
    onehot = (pos[:, None] // SLC_BLOCK == jnp.arange(LANE, dtype=jnp.int32)[None, :]).astype(F32)
    kconst = jnp.concatenate([onehot, jnp.zeros((t, HEAD_DIM), F32), _key_consts(pos)], axis=1)
    kp = (jnp.pad(ks[0], ((0, 0), (0, 0), (0, 0), (LANE, HEAD_DIM))) + kconst[None, None]).astype(BF16)
    vw = jnp.concatenate([ks[1], kw[0], kw[1], jnp.zeros_like(kw[1])], axis=-1).astype(BF16)
    n_c = t // CMP_STRIDE
    pool = (jnp.arange(n_c)[:, None] // (SLC_BLOCK // CMP_STRIDE) == jnp.arange(LANE)[None, :]).astype(BF16)
    gn2 = _gate_pack(gn, b, t)
    return pl.pallas_call(
        functools.partial(_nsa_prompt_kernel, tk=tk),
        grid=(b, N_KV, t // Q_BLOCK),
        in_specs=[pl.BlockSpec((1, HPG, Q_BLOCK, LANE), lambda i, g, j: (i, g, j, 0)),
                  pl.BlockSpec((1, 1, Q_BLOCK, 16), lambda i, g, j: (i, g, j, 0)),
                  pl.BlockSpec((1, n_c, 2 * HEAD_DIM), lambda i, g, j: (i, 0, g)),
                  pl.BlockSpec((1, 1, t, 2 * LANE), lambda i, g, j: (i, g, 0, 0)),
                  pl.BlockSpec((1, 1, t, 2 * LANE), lambda i, g, j: (i, g, 0, 0)),
                  pl.BlockSpec((n_c, LANE), lambda i, g, j: (0, 0))],
        out_specs=pl.BlockSpec((1, Q_BLOCK, HPG * HEAD_DIM), lambda i, g, j: (i, j, g)),
        out_shape=jax.ShapeDtypeStruct((b, t, NSA_WIDTH), F32),
        compiler_params=_cparams(3),
        name="nsa_prompt",
    )(qp, gn2, ckv, kp, vw, pool).reshape(b * t, NSA_WIDTH)


def _mem_attend_kernel(q_ref, kv_ref, o_ref):
    q = q_ref[0]
    kv = kv_ref[0]
    scale = MEM_HEAD_DIM ** -0.5
    for h in range(MEM_HEADS):
        sl = slice(h * MEM_HEAD_DIM, (h + 1) * MEM_HEAD_DIM)
        k = kv[:, sl].astype(BF16)
        v = kv[:, MEM_WIDTH + h * MEM_HEAD_DIM:MEM_WIDTH + (h + 1) * MEM_HEAD_DIM].astype(BF16)
        s = _dot_nt(q[:, sl].astype(BF16), k) * scale
        e = jnp.exp(s - jnp.max(s, axis=-1, keepdims=True))
        p = e / jnp.sum(e, axis=-1, keepdims=True)
        o_ref[0, :, sl] = _dot(p.astype(BF16), v)


def _mem_attend(mq, mkv):
    b, t, _ = mq.shape
    tm = _row_tile(t, 512)
    return pl.pallas_call(
        _mem_attend_kernel,
        grid=(b, t // tm),
        in_specs=[pl.BlockSpec((1, tm, MEM_WIDTH), lambda i, j: (i, j, 0)),
                  pl.BlockSpec((1, N_MEM, 2 * MEM_WIDTH), lambda i, j: (i, 0, 0))],
        out_specs=pl.BlockSpec((1, tm, MEM_WIDTH), lambda i, j: (i, j, 0)),
        out_shape=jax.ShapeDtypeStruct((b, t, MEM_WIDTH), F32),
        compiler_params=_cparams(2),
        name="mem_attend",
    )(mq, mkv)


def _ssm_disc_kernel(ar_ref, ai_ref, ldt_ref, br_ref, bi_ref, abr_ref, abi_ref, bbr_ref, bbi_ref):
    ar = ar_ref[...]
    ai = ai_ref[...]
    dt = jnp.exp(ldt_ref[...])
    mag = jnp.exp(dt * ar)
    abr = mag * jnp.cos(dt * ai)
    abi = mag * jnp.sin(dt * ai)
    den = ar * ar + ai * ai
    fr = ((abr - 1.0) * ar + abi * ai) / den
    fi = (abi * ar - (abr - 1.0) * ai) / den
    br = br_ref[...]
    bi = bi_ref[...]
    abr_ref[...] = abr
    abi_ref[...] = abi
    bbr_ref[...] = fr * br - fi * bi
    bbi_ref[...] = fr * bi + fi * br


def _ssm_discretise(a_re, a_im, log_dt, b_re, b_im):
    rep = lambda x: jnp.repeat(x, SSM_GROUP, axis=0)
    tb = lambda x: x.transpose(0, 2, 1).reshape(SSM_WIDTH, SSM_STATE)
    shp = jax.ShapeDtypeStruct((SSM_WIDTH, SSM_STATE), F32)
    abr, abi, bbr, bbi = pl.pallas_call(
        _ssm_disc_kernel, out_shape=[shp] * 4, name="ssm_disc",
    )(rep(a_re), rep(a_im), rep(jnp.broadcast_to(log_dt[:, None], a_re.shape)), tb(b_re), tb(b_im))
    eye = jnp.eye(SSM_GPB, dtype=F32)

    def in_map(bb):
        bb = bb.reshape(SSM_SLABS, SSM_GPB, SSM_GROUP, SSM_STATE)
        return jnp.einsum('jgcp,gG->jgcGp', bb, eye).reshape(SSM_SLABS, LANE, SSM_GPB * SSM_STATE).astype(BF16)

    flat = lambda x: x[::SSM_GROUP].reshape(1, SSM_FLAT)
    return flat(abr), flat(abi), in_map(bbr), in_map(bbi)


def _ssm_out_map(c):
    eye = jnp.eye(SSM_GPB, dtype=F32)
    c = c.reshape(SSM_SLABS, SSM_GPB, SSM_GROUP, SSM_STATE)
    return jnp.einsum('jgcp,gG->jgpGc', c, eye).reshape(SSM_SLABS, SSM_GPB * SSM_STATE, LANE).astype(BF16)


def _ssm_kernel(u_ref, h0r_ref, h0i_ref, abr_ref, abi_ref, bdr_ref, bdi_ref, cdr_ref, cdi_ref, d_ref,
                wg_ref, bg_ref, o_ref, hr_ref, hi_ref, sr_ref, si_ref, st_ref, *, n_scan):
    c_idx = pl.program_id(1)
    slab = SSM_GPB * SSM_STATE

    @pl.when(c_idx == 0)
    def _():
        st_ref[0:1, :] = h0r_ref[0]
        st_ref[1:2, :] = h0i_ref[0]

    u = u_ref[0]
    ub = u.astype(BF16)
    for j in range(SSM_SLABS):
        uj = ub[:, j * LANE:(j + 1) * LANE]
        sr_ref[:, j * slab:(j + 1) * slab] = _dot(uj, bdr_ref[j])
        si_ref[:, j * slab:(j + 1) * slab] = _dot(uj, bdi_ref[j])

    ar = abr_ref[...]
    ai = abi_ref[...]

    def step(t, carry):
        hr, hi = carry
        nr = ar * hr - ai * hi + sr_ref[pl.ds(t, 1), :]
        ni = ar * hi + ai * hr + si_ref[pl.ds(t, 1), :]
        sr_ref[pl.ds(t, 1), :] = nr
        si_ref[pl.ds(t, 1), :] = ni
        return nr, ni

    hr, hi = lax.fori_loop(0, n_scan, step, (st_ref[0:1, :], st_ref[1:2, :]))
    st_ref[0:1, :] = hr
    st_ref[1:2, :] = hi
    hr_ref[0] = hr
    hi_ref[0] = hi

    ys = []
    for j in range(SSM_SLABS):
        hrj = sr_ref[:, j * slab:(j + 1) * slab].astype(BF16)
        hij = si_ref[:, j * slab:(j + 1) * slab].astype(BF16)
        ys.append(_dot(hrj, cdr_ref[j]) - _dot(hij, cdi_ref[j]))
    y = jnp.concatenate(ys, axis=1) + d_ref[...] * u
    y = jax.nn.gelu(y)
    o_ref[0] = y * jax.nn.sigmoid(_dot(y.astype(BF16), wg_ref[...]) + bg_ref[...])


def _ssm(u, h0r, h0i, disc, cdr, cdi, d, w_glu, b_glu, t_valid=None):
    b, t, _ = u.shape
    tc = _row_tile(t, 256)
    n_scan = tc if t_valid is None else t_valid
    assert t_valid is None or t == tc
    abr, abi, bdr, bdi = disc
    full = lambda a: pl.BlockSpec(a.shape, lambda i, c: (0,) * a.ndim)
    state_spec = pl.BlockSpec((1, 1, SSM_FLAT), lambda i, c: (i, 0, 0))
    args = (abr, abi, bdr, bdi, cdr, cdi, d.reshape(1, SSM_WIDTH), w_glu.astype(BF16), b_glu.reshape(1, SSM_WIDTH))
    o, hr, hi = pl.pallas_call(
        functools.partial(_ssm_kernel, n_scan=n_scan),
        grid=(b, t // tc),
        in_specs=[pl.BlockSpec((1, tc, SSM_WIDTH), lambda i, c: (i, c, 0)), state_spec, state_spec]
                 + [full(a) for a in args],
        out_specs=[pl.BlockSpec((1, tc, SSM_WIDTH), lambda i, c: (i, c, 0)), state_spec, state_spec],
        out_shape=[jax.ShapeDtypeStruct((b, t, SSM_WIDTH), F32),
                   jax.ShapeDtypeStruct((b, 1, SSM_FLAT), F32),
                   jax.ShapeDtypeStruct((b, 1, SSM_FLAT), F32)],
        scratch_shapes=[pltpu.VMEM((tc, SSM_FLAT), F32), pltpu.VMEM((tc, SSM_FLAT), F32),
                        pltpu.VMEM((8, SSM_FLAT), F32)],
        compiler_params=_cparams(2),
        name="ssm",
    )(u, h0r.reshape(b, 1, SSM_FLAT), h0i.reshape(b, 1, SSM_FLAT), *args)
    shp = (b, N_SSM_GROUPS, SSM_STATE)
    return o, hr.reshape(shp), hi.reshape(shp)


def _merge_kernel(x_ref, oa_ref, ob_ref, oc_ref, gb_ref, w_ref, o_ref):
    acc = x_ref[...]
    for i, r in enumerate((oa_ref, ob_ref, oc_ref)):
        w = r.shape[1]
        off = sum((NSA_WIDTH, SSM_WIDTH, MEM_WIDTH)[:i])
        mixed = (r[...] * gb_ref[:, off:off + w]).astype(BF16)
        acc = acc + _dot(mixed, w_ref[off:off + w, :])
    o_ref[...] = acc


def _merge(x2, o_a, o_b, o_c, gb, w_o_bf16):
    n = x2.shape[0]
    tm = _row_tile(n, 512)
    row = lambda w: pl.BlockSpec((tm, w), lambda i: (i, 0))
    return pl.pallas_call(
        _merge_kernel,
        grid=(n // tm,),
        in_specs=[row(D_MODEL), row(NSA_WIDTH), row(SSM_WIDTH), row(MEM_WIDTH), row(MIX_WIDTH),
                  pl.BlockSpec((MIX_WIDTH, D_MODEL), lambda i: (0, 0))],
        out_specs=row(D_MODEL),
        out_shape=jax.ShapeDtypeStruct((n, D_MODEL), F32),
        compiler_params=_cparams(1),
        name="merge",
    )(x2, o_a, o_b, o_c, gb, w_o_bf16)


def _ffn_kernel(x_ref, g_ref, w1_ref, w2_ref, fg_ref, o_ref, h_ref, acc_ref, *, final_norm):
    k = pl.program_id(1)

    @pl.when(k == 0)
    def _():
        h_ref[...] = _rms(x_ref[...], g_ref[...]).astype(BF16)
        acc_ref[...] = jnp.zeros_like(acc_ref)

    a = jnp.square(jnp.maximum(_dot(h_ref[...], w1_ref[...]), 0.0))
    acc_ref[...] += _dot(a.astype(BF16), w2_ref[...])

    @pl.when(k == pl.num_programs(1) - 1)
    def _():
        y = x_ref[...] + acc_ref[...]
        if final_norm:
            y = _rms(y, fg_ref[...])
        o_ref[...] = y


def _ffn(x2, g, w1_bf16, w2_bf16, final_g, final_norm):
    n = x2.shape[0]
    tm = _row_tile(n, 1024)
    tf = 1024
    return pl.pallas_call(
        functools.partial(_ffn_kernel, final_norm=final_norm),
        grid=(n // tm, D_FF // tf),
        in_specs=[pl.BlockSpec((tm, D_MODEL), lambda i, k: (i, 0)),
                  pl.BlockSpec((1, D_MODEL), lambda i, k: (0, 0)),
                  pl.BlockSpec((D_MODEL, tf), lambda i, k: (0, k)),
                  pl.BlockSpec((tf, D_MODEL), lambda i, k: (k, 0)),
                  pl.BlockSpec((1, D_MODEL), lambda i, k: (0, 0))],
        out_specs=pl.BlockSpec((tm, D_MODEL), lambda i, k: (i, 0)),
        out_shape=jax.ShapeDtypeStruct((n, D_MODEL), F32),
        scratch_shapes=[pltpu.VMEM((tm, D_MODEL), BF16), pltpu.VMEM((tm, D_MODEL), F32)],
        compiler_params=_cparams(2),
        name="ffn",
    )(x2, g.reshape(1, D_MODEL), w1_bf16, w2_bf16, final_g.reshape(1, D_MODEL))


def _layer_params(l, norm1_g, w_in, cmp_pe, cmp_w1, cmp_w2, ssm_a_re, ssm_a_im, ssm_log_dt, ssm_b_re, ssm_b_im,
                  ssm_c_re, ssm_c_im, ssm_d, w_glu, b_glu, mem_norm_g, w_mem_kv, w_o, norm2_g, w_ff1, w_ff2):
    return dict(
        norm1_g=norm1_g[l], w_in=_pack_w_in(w_in[l]),
        pe=_flat_cmp_pe(cmp_pe[l]), w1=_expand_cmp_w1(cmp_w1[l]), w2=_expand_cmp_w2(cmp_w2[l]),
        disc=_ssm_discretise(ssm_a_re[l], ssm_a_im[l], ssm_log_dt[l], ssm_b_re[l], ssm_b_im[l]),
        cdr=_ssm_out_map(ssm_c_re[l]), cdi=_ssm_out_map(ssm_c_im[l]), d=ssm_d[l],
        w_glu=w_glu[l], b_glu=b_glu[l], mem_norm_g=mem_norm_g[l], w_mem_kv=w_mem_kv[l].astype(BF16),
        w_o=w_o[l].astype(BF16), norm2_g=norm2_g[l], w_ff1=w_ff1[l].astype(BF16), w_ff2=w_ff2[l].astype(BF16))


def _prompt_layer(x2, mem2, p, b, t, final_g, final_norm):
    q, kvc, kvs, kvw, u, mq, gb, gn = _in_proj(x2, p['norm1_g'], p['w_in'])
    lh = _cmp_lh(kvc.reshape(b * t // CMP_STRIDE, CMP_ROW), p['pe'], p['w1'])
    ckv = _cmp_tokens_prompt(lh.reshape(b, t // CMP_STRIDE, 4 * KV_WIDTH), p['w2'])
    o_a = _nsa_prompt(q, kvs, kvw, gn, ckv, b, t)
    zero = jnp.zeros((b, N_SSM_GROUPS, SSM_STATE), F32)
    o_b, hr, hi = _ssm(u.reshape(b, t, SSM_WIDTH), zero, zero, p['disc'], p['cdr'], p['cdi'], p['d'],
                       p['w_glu'], p['b_glu'])
    mkv = _norm_mm(mem2, p['mem_norm_g'], p['w_mem_kv'])
    o_c = _mem_attend(mq.reshape(b, t, MEM_WIDTH), mkv.reshape(b, N_MEM, 2 * MEM_WIDTH))
    x2 = _merge(x2, o_a, o_b.reshape(b * t, SSM_WIDTH), o_c.reshape(b * t, MEM_WIDTH), gb, p['w_o'])
    x2 = _ffn(x2, p['norm2_g'], p['w_ff1'], p['w_ff2'], final_g, final_norm)
    kv_shape = (b, t, 2, N_KV, HEAD_DIM)
    wl = min(WINDOW, t)
    return x2, dict(cmp=kvc.reshape(kv_shape), slc=kvs.reshape(kv_shape),
                    win=kvw.reshape(kv_shape)[:, t - wl:], hr=hr, hi=hi,
                    mkv=mkv.reshape(b, N_MEM, 2, MEM_HEADS, MEM_HEAD_DIM))


CHUNKS_PER_PAGE = PAGE_SIZE // CMP_STRIDE


def _cmp_lh_paged_kernel(pt_ref, cache_ref, pe_ref, w_ref, o_ref, x_ref, sem, *, pages):
    b = pl.program_id(0)
    s = pl.program_id(1)

    def page_copy(i):
        page = pt_ref[b, s * pages + i]
        return pltpu.make_async_copy(cache_ref.at[page], x_ref.at[pl.ds(i * CHUNKS_PER_PAGE, CHUNKS_PER_PAGE)],
                                     sem.at[i])

    for i in range(pages):
        page_copy(i).start()
    for i in range(pages):
        page_copy(i).wait()
    o_ref[0] = _cmp_lohi(x_ref[...], pe_ref, w_ref)


def _cmp_lh_paged(cache, page_table, pe_flat, w1_big):
    n_phys = cache.shape[0]
    b, n_pages = page_table.shape
    pages = math.gcd(n_pages, 32)
    rows = pages * CHUNKS_PER_PAGE
    grid_spec = pltpu.PrefetchScalarGridSpec(
        num_scalar_prefetch=1,
        grid=(b, n_pages // pages),
        in_specs=[pl.BlockSpec(memory_space=pl.ANY),
                  pl.BlockSpec((2, 1, CMP_ROW), lambda i, s, pt: (0, 0, 0)),
                  pl.BlockSpec((2, CMP_ROW, 2 * KV_WIDTH), lambda i, s, pt: (0, 0, 0))],
        out_specs=pl.BlockSpec((1, rows, 4 * KV_WIDTH), lambda i, s, pt: (i, s, 0)),
        scratch_shapes=[pltpu.VMEM((rows, CMP_ROW), F32), pltpu.SemaphoreType.DMA((pages,))])
    return pl.pallas_call(
        functools.partial(_cmp_lh_paged_kernel, pages=pages),
        grid_spec=grid_spec,
        out_shape=jax.ShapeDtypeStruct((b, n_pages * CHUNKS_PER_PAGE, 4 * KV_WIDTH), F32),
        compiler_params=_cparams(2),
        name="cmp_lh_paged",
    )(page_table, cache.reshape(n_phys, CHUNKS_PER_PAGE, CMP_ROW), pe_flat, w1_big)


TOK_PAD = 8


def _nsa_sample_select_kernel(lhc_ref, lht_ref, w2_ref, q_ref, pool_ref, oc_ref, idx_ref, lh_ref, *, past, n_tok):
    n_cache = lhc_ref.shape[1]
    n_tail = lht_ref.shape[1]
    r = lh_ref.shape[0]
    lh_ref[0:n_cache, :] = lhc_ref[0]
    lh_ref[n_cache:n_cache + n_tail, :] = lht_ref[0]
    lh_ref[n_cache + n_tail:r, :] = jnp.zeros((r - n_cache - n_tail, lh_ref.shape[1]), F32)
    ckv = _cmp_combine(lh_ref[...], w2_ref[...])

    rows = HPG * TOK_PAD
    row = lax.broadcasted_iota(jnp.int32, (rows, 1), 0)
    pos_q = past + (row & (TOK_PAD - 1))
    ci = lax.broadcasted_iota(jnp.int32, (1, r), 1)
    d_c = pos_q - (ci * CMP_STRIDE + (CMP_LEN - 1))
    lanes = pool_ref.shape[1]
    t_i = lax.broadcasted_iota(jnp.int32, (TOK_PAD, 1), 0)
    blk = lax.broadcasted_iota(jnp.int32, (1, lanes), 1)
    cur = (past + t_i) // SLC_BLOCK
    valid = blk <= cur
    forced = (blk == 0) | (blk == cur) | (blk == cur - 1)
    lane = lax.broadcasted_iota(jnp.int32, (1, LANE), 1)
    for g in range(N_KV):
        qg = q_ref[0, g]
        q = qg[:, :HEAD_DIM]
        slope = qg[:, HEAD_DIM:HEAD_DIM + 1].astype(F32)
        ck = ckv[:, g * 2 * HEAD_DIM:g * 2 * HEAD_DIM + HEAD_DIM].astype(BF16)
        cv = ckv[:, g * 2 * HEAD_DIM + HEAD_DIM:(g + 1) * 2 * HEAD_DIM].astype(BF16)
        s = _dot_nt(q, ck) - slope * d_c.astype(F32)
        p_c = _masked_probs(s, d_c >= 0)
        oc_ref[0, g] = _dot(p_c.astype(BF16), cv)
        ps = p_c[0:TOK_PAD]
        for h in range(1, HPG):
            ps = ps + p_c[h * TOK_PAD:(h + 1) * TOK_PAD]
        imp = _pool_blocks(ps, pool_ref[...])
        score = jnp.where(valid, jnp.where(forced, FORCED_SCORE, imp), -1.0)
        _, picks = _topk_select(score, blk.astype(F32), SLC_TOPK)
        idx = jnp.zeros((TOK_PAD, LANE), jnp.int32)
        for k, pk in enumerate(picks):
            idx = jnp.where(lane == k, pk.astype(jnp.int32), idx)
        idx_ref[0, g] = idx


def _nsa_sample_select(lh_cache, lh_tail, w2_big, qsel, past, n_tok):
    b, n_cache, _ = lh_cache.shape
    n_tail = lh_tail.shape[1]
    r = -(-(n_cache + n_tail) // LANE) * LANE
    n_b = past // SLC_BLOCK + 1
    assert n_b >= SLC_TOPK
    lanes = -(-n_b // LANE) * LANE
    pool = (jnp.arange(r)[:, None] // (SLC_BLOCK // CMP_STRIDE) == jnp.arange(lanes)[None, :]).astype(BF16)
    rows = HPG * TOK_PAD
    return pl.pallas_call(
        functools.partial(_nsa_sample_select_kernel, past=past, n_tok=n_tok),
        grid=(b,),
        in_specs=[pl.BlockSpec((1, n_cache, 4 * KV_WIDTH), lambda i: (i, 0, 0)),
                  pl.BlockSpec((1, n_tail, 4 * KV_WIDTH), lambda i: (i, 0, 0)),
                  pl.BlockSpec((2 * KV_WIDTH, 2 * KV_WIDTH), lambda i: (0, 0)),
                  pl.BlockSpec((1, N_KV, rows, LANE), lambda i: (i, 0, 0, 0)),
                  pl.BlockSpec((r, lanes), lambda i: (0, 0))],
        out_specs=[pl.BlockSpec((1, N_KV, rows, HEAD_DIM), lambda i: (i, 0, 0, 0)),
                   pl.BlockSpec((1, N_KV, TOK_PAD, LANE), lambda i: (i, 0, 0, 0))],
        out_shape=[jax.ShapeDtypeStruct((b, N_KV, rows, HEAD_DIM), F32),
                   jax.ShapeDtypeStruct((b, N_KV, TOK_PAD, LANE), jnp.int32)],
        scratch_shapes=[pltpu.VMEM((r, 4 * KV_WIDTH), F32)],
        compiler_params=_cparams(1),
        name="nsa_sample_select",
    )(lh_cache, lh_tail, w2_big, qsel, pool)


def _nsa_sample_attend_kernel(idx_ref, pt_ref, cache_ref, q_ref, oc_ref, gate_ref, snew_ref, win_ref, wnew_ref,
                              o_ref, buf_ref, sem, *, past, n_tok):
    b = pl.program_id(0)
    n_pages = past // PAGE_SIZE
    cur_blk = past // SLC_BLOCK
    per_page = PAGE_SIZE // SLC_BLOCK
    n_sel = SLC_TOPK * SLC_BLOCK

    def sel_block(g, t, k):
        return idx_ref[((b * N_KV + g) * n_tok + t) * SLC_TOPK + k]

    def block_copy(g, t, k):
        c = jnp.minimum(sel_block(g, t, k), cur_blk - 1)
        page = pt_ref[b * n_pages + c // per_page]
        src = cache_ref.at[page, pl.ds((c % per_page) * SLC_BLOCK, SLC_BLOCK), :]
        return pltpu.make_async_copy(src, buf_ref.at[g * n_tok + t, pl.ds(k * SLC_BLOCK, SLC_BLOCK), :],
                                     sem.at[g * n_tok + t])

    for g in range(N_KV):
        for t in range(n_tok):
            for k in range(SLC_TOPK):
                block_copy(g, t, k).start()

    rows = HPG * TOK_PAD
    row = lax.broadcasted_iota(jnp.int32, (rows, 1), 0)
    tok = row & (TOK_PAD - 1)
    pos_q = past + tok
    lane = lax.broadcasted_iota(jnp.int32, (1, n_sel), 1)
    r_new = lax.broadcasted_iota(jnp.int32, (1, TOK_PAD), 1)
    d_new = pos_q - (past + r_new)
    mask_new = (d_new >= 0) & (r_new < n_tok)
    wb = win_ref.shape[1]
    d_win = pos_q - (past - wb + lax.broadcasted_iota(jnp.int32, (1, wb), 1))
    mask_win = (d_win >= 0) & (d_win <= WINDOW)

    def joint_attend(q, slope, k_old, v_old, d_old, mask_old, k_new, v_new):
        s_o = jnp.where(mask_old, _dot_nt(q, k_old) - slope * d_old.astype(F32), NEG_INF)
        s_n = jnp.where(mask_new, _dot_nt(q, k_new) - slope * d_new.astype(F32), NEG_INF)
        m = jnp.maximum(jnp.max(s_o, axis=-1, keepdims=True), jnp.max(s_n, axis=-1, keepdims=True))
        e_o = jnp.where(mask_old, jnp.exp(s_o - m), 0.0)
        e_n = jnp.where(mask_new, jnp.exp(s_n - m), 0.0)
        l = jnp.sum(e_o, axis=-1, keepdims=True) + jnp.sum(e_n, axis=-1, keepdims=True)
        return (_dot(e_o.astype(BF16), v_old) + _dot(e_n.astype(BF16), v_new)) / jnp.maximum(l, 1e-30)

    for g in range(N_KV):
        ksl = slice(g * HEAD_DIM, (g + 1) * HEAD_DIM)
        vsl = slice(KV_WIDTH + g * HEAD_DIM, KV_WIDTH + (g + 1) * HEAD_DIM)
        qg = q_ref[0, g]
        q = qg[:, :HEAD_DIM]
        slope = qg[:, HEAD_DIM:HEAD_DIM + 1].astype(F32)
        k_snew = snew_ref[0][:, ksl].astype(BF16)
        v_snew = snew_ref[0][:, vsl].astype(BF16)
        o_s = jnp.zeros((rows, HEAD_DIM), F32)
        for t in range(n_tok):
            for k in range(SLC_TOPK):
                block_copy(g, t, k).wait()
            blk_of_lane = jnp.zeros((1, n_sel), jnp.int32)
            for k in range(SLC_TOPK):
                blk_of_lane = jnp.where(lane // SLC_BLOCK == k, sel_block(g, t, k), blk_of_lane)
            d_sel = pos_q - (blk_of_lane * SLC_BLOCK + (lane & (SLC_BLOCK - 1)))
            mask_sel = (d_sel >= 0) & (blk_of_lane < cur_blk)
            kv_sel = buf_ref[g * n_tok + t]
            o_t = joint_attend(q, slope, kv_sel[:, ksl].astype(BF16), kv_sel[:, vsl].astype(BF16), d_sel, mask_sel,
                               k_snew, v_snew)
            o_s = jnp.where(tok == t, o_t, o_s)
        win = win_ref[0]
        o_w = joint_attend(q, slope, win[:, ksl].astype(BF16), win[:, vsl].astype(BF16), d_win, mask_win,
                           wnew_ref[0][:, ksl].astype(BF16), wnew_ref[0][:, vsl].astype(BF16))
        gate = gate_ref[0, g]
        o_ref[0, g] = gate[:, 0:1] * oc_ref[0, g] + gate[:, 1:2] * o_s + gate[:, 2:3] * o_w


def _nsa_sample_attend(idx, page_table, cache_slc, qsel, o_c, gates, slc_new, win_buf, win_new, past, n_tok):
    b = qsel.shape[0]
    n_phys = cache_slc.shape[0]
    assert past % PAGE_SIZE == 0 and n_tok <= TOK_PAD
    rows = HPG * TOK_PAD
    wb = win_buf.shape[1]
    per_b = lambda shape: pl.BlockSpec((1,) + shape, lambda i, *_: (i,) + (0,) * len(shape))
    grid_spec = pltpu.PrefetchScalarGridSpec(
        num_scalar_prefetch=2,
        grid=(b,),
        in_specs=[pl.BlockSpec(memory_space=pl.ANY),
                  per_b((N_KV, rows, LANE)), per_b((N_KV, rows, HEAD_DIM)), per_b((N_KV, rows, 16)),
                  per_b((TOK_PAD, 2 * KV_WIDTH)), per_b((wb, 2 * KV_WIDTH)), per_b((TOK_PAD, 2 * KV_WIDTH))],
        out_specs=per_b((N_KV, rows, HEAD_DIM)),
        scratch_shapes=[pltpu.VMEM((N_KV * n_tok, SLC_TOPK * SLC_BLOCK, 2 * KV_WIDTH), F32),
                        pltpu.SemaphoreType.DMA((N_KV * n_tok,))])
    return pl.pallas_call(
        functools.partial(_nsa_sample_attend_kernel, past=past, n_tok=n_tok),
        grid_spec=grid_spec,
        out_shape=jax.ShapeDtypeStruct((b, N_KV, rows, HEAD_DIM), F32),
        compiler_params=_cparams(1),
        name="nsa_sample_attend",
    )(idx[:, :, :n_tok, :SLC_TOPK].reshape(-1), page_table.reshape(-1),
      cache_slc.reshape(n_phys, PAGE_SIZE, 2 * KV_WIDTH), qsel, o_c, gates, slc_new, win_buf, win_new)


def _pad_tokens(x, axis):
    pad = [(0, 0)] * x.ndim
    pad[axis] = (0, TOK_PAD - x.shape[axis])
    return jnp.pad(x, pad)


def _sample_layer(x2, p, b, s, cache_cmp, cache_slc, cache_win, h0r, h0i, cache_mem, page_table, final_g, final_norm):
    past = page_table.shape[1] * PAGE_SIZE
    assert past % SLC_BLOCK == 0 and s <= CMP_STRIDE
    q, kvc, kvs, kvw, u, mq, gb, gn = _in_proj(x2, p['norm1_g'], p['w_in'])
    lh_cache = _cmp_lh_paged(cache_cmp, page_table, p['pe'], p['w1'])
    tail = jnp.pad(kvc.reshape(b, s, 2 * KV_WIDTH), ((0, 0), (0, CHUNKS_PER_PAGE * CMP_STRIDE - s), (0, 0)))
    lh_tail = _cmp_lh(tail.reshape(b * CHUNKS_PER_PAGE, CMP_ROW), p['pe'], p['w1'])
    pos = past + jnp.arange(s, dtype=jnp.int32)
    qsel = _pad_tokens(_query_pack(q.reshape(b, s, NSA_WIDTH), pos), 2).reshape(b, N_KV, HPG * TOK_PAD, LANE)
    o_c, idx = _nsa_sample_select(lh_cache, lh_tail.reshape(b, CHUNKS_PER_PAGE, 4 * KV_WIDTH), p['w2'], qsel, past, s)
    gates = gn[:, :3 * N_HEADS].reshape(b, s, 3, N_KV, HPG).transpose(0, 3, 4, 1, 2)
    gates = jnp.pad(_pad_tokens(gates, 3), ((0, 0),) * 4 + ((0, 16 - 3),)).reshape(b, N_KV, HPG * TOK_PAD, 16)
    win_buf = cache_win.reshape(b, -1, 2 * KV_WIDTH)
    slc_new = kvs.reshape(b, s, 2 * KV_WIDTH)
    win_new = kvw.reshape(b, s, 2 * KV_WIDTH)
    o = _nsa_sample_attend(idx, page_table, cache_slc, qsel, o_c, gates, _pad_tokens(slc_new, 1), win_buf,
                           _pad_tokens(win_new, 1), past, s)
    o_a = o.reshape(b, N_KV, HPG, TOK_PAD, HEAD_DIM)[:, :, :, :s].transpose(0, 3, 1, 2, 4).reshape(b * s, NSA_WIDTH)
    o_b, hr, hi = _ssm(_pad_tokens(u.reshape(b, s, SSM_WIDTH), 1), h0r, h0i, p['disc'], p['cdr'], p['cdi'], p['d'],
                       p['w_glu'], p['b_glu'], t_valid=s)
    o_c2 = _mem_attend(_pad_tokens(mq.reshape(b, s, MEM_WIDTH), 1), cache_mem.reshape(b, N_MEM, 2 * MEM_WIDTH))
    x2 = _merge(x2, o_a, o_b[:, :s].reshape(b * s, SSM_WIDTH), o_c2[:, :s].reshape(b * s, MEM_WIDTH), gb, p['w_o'])
    x2 = _ffn(x2, p['norm2_g'], p['w_ff1'], p['w_ff2'], final_g, final_norm)
    kv_shape = (b, s, 2, N_KV, HEAD_DIM)
    new_win = jnp.concatenate([cache_win, kvw.reshape(kv_shape).astype(cache_win.dtype)], axis=1)[:, s:]
    return x2, dict(cmp=kvc.reshape(kv_shape), slc=kvs.reshape(kv_shape), win=new_win, hr=hr, hi=hi)


def kernel(x_prompt, x_sample, cache_cmp_kv, cache_slc_kv, cache_win_kv, state_ssm_re, state_ssm_im, cache_mem_kv,
           page_table, mem_prompt, norm1_g, w_in, cmp_pe, cmp_w1, cmp_w2, ssm_a_re, ssm_a_im, ssm_log_dt, ssm_b_re,
           ssm_b_im, ssm_c_re, ssm_c_im, ssm_d, w_glu, b_glu, mem_norm_g, w_mem_kv, w_o, norm2_g, w_ff1, w_ff2,
           final_norm_g):
    bp, t, _ = x_prompt.shape
    bs, s, _ = x_sample.shape
    depth = w_in.shape[0]
    xp = x_prompt.reshape(bp * t, D_MODEL)
    xs = x_sample.reshape(bs * s, D_MODEL)
    mem2 = mem_prompt.reshape(bp * N_MEM, D_MODEL)
    aux_p, aux_s = [], []
    for l in range(depth):
        p = _layer_params(l, norm1_g, w_in, cmp_pe, cmp_w1, cmp_w2, ssm_a_re, ssm_a_im, ssm_log_dt, ssm_b_re,
                          ssm_b_im, ssm_c_re, ssm_c_im, ssm_d, w_glu, b_glu, mem_norm_g, w_mem_kv, w_o, norm2_g,
                          w_ff1, w_ff2)
        last = l == depth - 1
        xp, ap = _prompt_layer(xp, mem2, p, bp, t, final_norm_g, last)
        xs, a_s = _sample_layer(xs, p, bs, s, cache_cmp_kv[l], cache_slc_kv[l], cache_win_kv[l], state_ssm_re[l],
                                state_ssm_im[l], cache_mem_kv[l], page_table, final_norm_g, last)
        aux_p.append(ap)
        aux_s.append(a_s)
    stack = lambda aux, key: jnp.stack([a[key] for a in aux])
    return (xp.reshape(bp, t, D_MODEL), xs.reshape(bs, s, D_MODEL),
            stack(aux_p, 'cmp'), stack(aux_s, 'cmp'), stack(aux_p, 'slc'), stack(aux_s, 'slc'),
            stack(aux_p, 'win'), stack(aux_s, 'win'), stack(aux_p, 'hr'), stack(aux_p, 'hi'),
            stack(aux_s, 'hr'), stack(aux_s, 'hi'), stack(aux_p, 'mkv'))
```

```python
import functools
import math

import jax
import jax.numpy as jnp
from jax import lax
from jax.experimental import pallas as pl
from jax.experimental.pallas import tpu as pltpu

F32 = jnp.float32
BF16 = jnp.bfloat16

D_MODEL = 1024
PAGE_SIZE = 128
N_HEADS = 8
HEAD_DIM = 64
N_KV = 2
HPG = N_HEADS // N_KV
CMP_STRIDE = 16
CMP_LEN = 2 * CMP_STRIDE
SLC_BLOCK = 64
SLC_TOPK = 16
WINDOW = 512
Q_BLOCK = 128
SSM_WIDTH = D_MODEL // 2
SSM_GROUP = 16
N_SSM_GROUPS = SSM_WIDTH // SSM_GROUP
SSM_STATE = 64
SSM_FLAT = N_SSM_GROUPS * SSM_STATE
N_MEM = 256
MEM_HEADS = 4
MEM_HEAD_DIM = 128
MEM_WIDTH = MEM_HEADS * MEM_HEAD_DIM
D_FF = 4 * D_MODEL
NSA_WIDTH = N_HEADS * HEAD_DIM
KV_WIDTH = N_KV * HEAD_DIM
MIX_WIDTH = NSA_WIDTH + SSM_WIDTH + MEM_WIDTH
OFF_KV = NSA_WIDTH
OFF_GN = OFF_KV + 6 * KV_WIDTH
OFF_U = OFF_GN + 3 * N_HEADS
OFF_MQ = OFF_U + SSM_WIDTH
OFF_GB = OFF_MQ + MEM_WIDTH
IN_WIDTH = OFF_GB + MIX_WIDTH
EPS = 1e-6
NEG_INF = -1e30
FORCED_SCORE = 1e9

LANE = 128
CMP_ROW = CMP_STRIDE * 2 * KV_WIDTH
SSM_GPB = LANE // SSM_GROUP
SSM_SLABS = SSM_WIDTH // LANE
VMEM_LIMIT = 56 * 2**20


def _cparams(n_grid, vmem=VMEM_LIMIT):
    return pltpu.CompilerParams(dimension_semantics=("arbitrary",) * n_grid, vmem_limit_bytes=vmem)


def _rms(x, g):
    return x * lax.rsqrt(jnp.mean(x * x, axis=-1, keepdims=True) + EPS) * g


def _dot(a, b):
    return jnp.dot(a, b, preferred_element_type=F32)


def _dot_nt(a, b):
    return lax.dot_general(a, b, (((1,), (1,)), ((), ())), preferred_element_type=F32)


def _masked_probs(s, mask):
    s = jnp.where(mask, s, NEG_INF)
    m = jnp.max(s, axis=-1, keepdims=True)
    e = jnp.where(mask, jnp.exp(s - m), 0.0)
    return e / jnp.maximum(jnp.sum(e, axis=-1, keepdims=True), 1e-30)


def _row_tile(n, pref):
    t = min(n, pref)
    assert n % t == 0
    return t


_SEGS = ((0, 512), (512, 768), (768, 1024), (1024, 1280), (1280, 1792), (1792, 2304), (2304, 3840), (3840, 3968))
_SIGMOID_FROM = 2304
_PACKED_WIDTH = 3968


def _pack_w_in(w):
    gn = jnp.pad(w[:, OFF_GN:OFF_U], ((0, 0), (0, LANE - 3 * N_HEADS)))
    return jnp.concatenate([w[:, :OFF_GN], w[:, OFF_U:], gn], axis=1).astype(BF16)


def _inproj_kernel(x_ref, g_ref, w_ref, *o_refs):
    h = _rms(x_ref[...], g_ref[...]).astype(BF16)
    for (a, b), o_ref in zip(_SEGS, o_refs):
        z = _dot(h, w_ref[:, a:b])
        if a >= _SIGMOID_FROM:
            z = jax.nn.sigmoid(z)
        o_ref[...] = z


def _in_proj(x2, g, w_packed):
    n = x2.shape[0]
    tm = _row_tile(n, 512)
    return pl.pallas_call(
        _inproj_kernel,
        grid=(n // tm,),
        in_specs=[pl.BlockSpec((tm, D_MODEL), lambda i: (i, 0)),
                  pl.BlockSpec((1, D_MODEL), lambda i: (0, 0)),
                  pl.BlockSpec((D_MODEL, _PACKED_WIDTH), lambda i: (0, 0))],
        out_specs=[pl.BlockSpec((tm, b - a), lambda i: (i, 0)) for a, b in _SEGS],
        out_shape=[jax.ShapeDtypeStruct((n, b - a), F32) for a, b in _SEGS],
        compiler_params=_cparams(1),
        name="in_proj",
    )(x2, g.reshape(1, D_MODEL), w_packed)


def _norm_mm_kernel(x_ref, g_ref, w_ref, o_ref):
    o_ref[...] = _dot(_rms(x_ref[...], g_ref[...]).astype(BF16), w_ref[...])


def _norm_mm(x2, g, w_bf16):
    n, k = x2.shape
    m = w_bf16.shape[1]
    tm = _row_tile(n, 256)
    return pl.pallas_call(
        _norm_mm_kernel,
        grid=(n // tm,),
        in_specs=[pl.BlockSpec((tm, k), lambda i: (i, 0)),
                  pl.BlockSpec((1, k), lambda i: (0, 0)),
                  pl.BlockSpec((k, m), lambda i: (0, 0))],
        out_specs=pl.BlockSpec((tm, m), lambda i: (i, 0)),
        out_shape=jax.ShapeDtypeStruct((n, m), F32),
        compiler_params=_cparams(1),
        name="norm_mm",
    )(x2, g.reshape(1, k), w_bf16)


def _expand_cmp_w1(w1):
    w = w1.reshape(2, 2, CMP_STRIDE, HEAD_DIM, HEAD_DIM)
    eye2 = jnp.eye(2, dtype=w1.dtype)
    big = jnp.einsum('khlde,kK,gG->hlkgdKGe', w, eye2, eye2)
    return big.reshape(2, CMP_ROW, 2 * KV_WIDTH).astype(BF16)


def _flat_cmp_pe(pe):
    p = pe.reshape(2, 2, CMP_STRIDE, HEAD_DIM)
    p = jnp.broadcast_to(p.transpose(1, 2, 0, 3)[:, :, :, None, :], (2, CMP_STRIDE, 2, N_KV, HEAD_DIM))
    return p.reshape(2, 1, CMP_ROW)


def _expand_cmp_w2(w2):
    eye2 = jnp.eye(2, dtype=w2.dtype)
    big = jnp.einsum('kde,kK,gG->kgdGKe', w2, eye2, eye2)
    return big.reshape(2 * KV_WIDTH, 2 * KV_WIDTH).astype(BF16)


def _cmp_lohi(x, pe_ref, w_ref):
    lo = _dot((x + pe_ref[0]).astype(BF16), w_ref[0])
    hi = _dot((x + pe_ref[1]).astype(BF16), w_ref[1])
    return jnp.concatenate([lo, hi], axis=1)


def _cmp_lh_kernel(x_ref, pe_ref, w_ref, o_ref):
    o_ref[...] = _cmp_lohi(x_ref[...], pe_ref, w_ref)


def _cmp_lh(x2, pe_flat, w1_big):
    n = x2.shape[0]
    tm = _row_tile(n, 256)
    return pl.pallas_call(
        _cmp_lh_kernel,
        grid=(n // tm,),
        in_specs=[pl.BlockSpec((tm, CMP_ROW), lambda i: (i, 0)),
                  pl.BlockSpec((2, 1, CMP_ROW), lambda i: (0, 0, 0)),
                  pl.BlockSpec((2, CMP_ROW, 2 * KV_WIDTH), lambda i: (0, 0, 0))],
        out_specs=pl.BlockSpec((tm, 4 * KV_WIDTH), lambda i: (i, 0)),
        out_shape=jax.ShapeDtypeStruct((n, 4 * KV_WIDTH), F32),
        compiler_params=_cparams(1),
        name="cmp_lh",
    )(x2, pe_flat, w1_big)


def _cmp_combine(lh, w2_big):
    r = lh.shape[0]
    hi_next = pltpu.roll(lh[:, 2 * KV_WIDTH:], r - 1, 0)
    return _dot(jax.nn.gelu(lh[:, :2 * KV_WIDTH] + hi_next).astype(BF16), w2_big)


def _cmp_combine_kernel(lh_ref, w2_ref, o_ref):
    o_ref[0] = _cmp_combine(lh_ref[0], w2_ref[...])


def _cmp_tokens_prompt(lh, w2_big):
    b, r, _ = lh.shape
    return pl.pallas_call(
        _cmp_combine_kernel,
        grid=(b,),
        in_specs=[pl.BlockSpec((1, r, 4 * KV_WIDTH), lambda i: (i, 0, 0)),
                  pl.BlockSpec((2 * KV_WIDTH, 2 * KV_WIDTH), lambda i: (0, 0))],
        out_specs=pl.BlockSpec((1, r, 2 * KV_WIDTH), lambda i: (i, 0, 0)),
        out_shape=jax.ShapeDtypeStruct((b, r, 2 * KV_WIDTH), F32),
        compiler_params=_cparams(1),
        name="cmp_combine",
    )(lh, w2_big)


def _alibi_slopes():
    return jnp.exp2(-8.0 * (jnp.arange(N_HEADS, dtype=F32) + 1.0) / N_HEADS)


def _query_pack(q, pos):
    b, t, _ = q.shape
    qh = q.reshape(b, t, N_HEADS, HEAD_DIM).transpose(0, 2, 1, 3) * (HEAD_DIM ** -0.5)
    slope = _alibi_slopes()[:, None]
    pos_hi = ((pos // SLC_BLOCK) * SLC_BLOCK).astype(F32)[None, :]
    pos_lo = (pos % SLC_BLOCK).astype(F32)[None, :]
    ones = jnp.ones((1, t), F32)
    c = jnp.stack([slope * ones, slope * ones, -slope * pos_hi, -slope * pos_lo], axis=-1)
    c = jnp.pad(c, ((0, 0), (0, 0), (0, HEAD_DIM - 4)))
    c = jnp.broadcast_to(c[None], (b, N_HEADS, t, HEAD_DIM))
    return jnp.concatenate([qh, c], axis=-1).astype(BF16)


def _key_consts(kpos):
    hi = ((kpos // SLC_BLOCK) * SLC_BLOCK).astype(F32)
    lo = (kpos % SLC_BLOCK).astype(F32)
    one = jnp.ones_like(hi)
    return jnp.pad(jnp.stack([hi, lo, one, one], axis=-1), ((0, 0), (0, HEAD_DIM - 4)))


def _split3(x):
    hi = x.astype(BF16)
    r1 = x - hi.astype(F32)
    mid = r1.astype(BF16)
    lo = (r1 - mid.astype(F32)).astype(BF16)
    return hi, mid, lo


def _pool_blocks(ps, pool):
    hi, mid, lo = _split3(ps)
    return _dot(hi, pool) + _dot(mid, pool) + _dot(lo, pool)


def _topk_select_cols(score, blk_f, k):
    sel = jnp.zeros(score.shape, F32)
    for _ in range(k):
        m = jnp.max(score, axis=0, keepdims=True)
        idx = jnp.min(jnp.where(score == m, blk_f, 1e9), axis=0, keepdims=True)
        pick = blk_f == idx
        sel = jnp.where(pick, 1.0, sel)
        score = jnp.where(pick, -3e38, score)
    return sel


def _topk_select(score, blk_f, k):
    sel = jnp.zeros(score.shape, F32)
    picks = []
    for _ in range(k):
        m = jnp.max(score, axis=-1, keepdims=True)
        idx = jnp.min(jnp.where(score == m, blk_f, 1e9), axis=-1, keepdims=True)
        pick = blk_f == idx
        sel = jnp.where(pick, 1.0, sel)
        score = jnp.where(pick, -3e38, score)
        picks.append(idx)
    return sel, picks


def _nsa_prompt_kernel(qp_ref, gn_ref, ckv_ref, kp_ref, vw_ref, pool_ref, o_ref, *, tk):
    j = pl.program_id(2)
    start = j * Q_BLOCK
    rows = HPG * Q_BLOCK
    qp = qp_ref[0].reshape(rows, LANE)
    q = qp[:, :HEAD_DIM]
    slope = qp[:, HEAD_DIM:HEAD_DIM + 1].astype(F32)
    row = lax.broadcasted_iota(jnp.int32, (rows, 1), 0)
    pos_q = start + (row & (Q_BLOCK - 1))

    ckv = ckv_ref[0]
    n_c = ckv.shape[0]
    ck = ckv[:, :HEAD_DIM].astype(BF16)
    cv = ckv[:, HEAD_DIM:].astype(BF16)
    ci = lax.broadcasted_iota(jnp.int32, (1, n_c), 1)
    d_c = pos_q - (ci * CMP_STRIDE + (CMP_LEN - 1))
    s = _dot_nt(q, ck) - slope * d_c.astype(F32)
    p_c = _masked_probs(s, d_c >= 0)
    o_c = _dot(p_c.astype(BF16), cv)
    ps = p_c[0:Q_BLOCK]
    for h in range(1, HPG):
        ps = ps + p_c[h * Q_BLOCK:(h + 1) * Q_BLOCK]
    hi3, mid3, lo3 = _split3(ps)
    pool_t = pool_ref[...]
    imp_t = _dot_nt(pool_t, hi3) + _dot_nt(pool_t, mid3) + _dot_nt(pool_t, lo3)
    blk = lax.broadcasted_iota(jnp.int32, (LANE, 1), 0)
    cur = (start + lax.broadcasted_iota(jnp.int32, (1, Q_BLOCK), 1)) // SLC_BLOCK
    valid = blk <= cur
    forced = (blk == 0) | (blk == cur) | (blk == cur - 1)
    score = jnp.where(valid, jnp.where(forced, FORCED_SCORE, imp_t), -1.0)
    sel_t = _topk_select_cols(score, blk.astype(F32), SLC_TOPK)
    mb = jnp.where((sel_t > 0.0) & valid, 0.0, NEG_INF).T.astype(BF16)
    qs = jnp.concatenate([jnp.concatenate([mb] * HPG, axis=0), qp], axis=1)

    def tile(i, carry, diag):
        m, l, acc = carry
        off = pl.multiple_of(i * tk, tk)
        s = _dot_nt(qs, kp_ref[0, 0, pl.ds(off, tk), :])
        if diag:
            kpos = off + lax.broadcasted_iota(jnp.int32, (1, tk), 1)
            s = jnp.where(kpos <= pos_q, s, NEG_INF)
        m2 = jnp.maximum(m, jnp.max(s, axis=-1, keepdims=True))
        a = jnp.exp(m - m2)
        e = jnp.exp(s - m2)
        l = a * l + jnp.sum(e, axis=-1, keepdims=True)
        acc = a * acc + _dot(e.astype(BF16), vw_ref[0, 0, pl.ds(off, tk), 0:HEAD_DIM])
        return m2, l, acc

    n_full = start // tk
    carry = (jnp.full((rows, 1), NEG_INF, F32), jnp.zeros((rows, 1), F32), jnp.zeros((rows, HEAD_DIM), F32))
    carry = lax.fori_loop(0, n_full, lambda i, c: tile(i, c, False), carry)
    _, l_s, acc_s = tile(n_full, carry, True)
    o_s = acc_s / jnp.maximum(l_s, 1e-30)

    wlen = WINDOW + Q_BLOCK
    woff = pl.multiple_of(jnp.maximum(start - WINDOW, 0), Q_BLOCK)
    kw = vw_ref[0, 0, pl.ds(woff, wlen), HEAD_DIM:2 * HEAD_DIM]
    vw = vw_ref[0, 0, pl.ds(woff, wlen), 2 * HEAD_DIM:3 * HEAD_DIM]
    d_w = pos_q - (woff + lax.broadcasted_iota(jnp.int32, (1, wlen), 1))
    s = _dot_nt(q, kw) - slope * d_w.astype(F32)
    p_w = _masked_probs(s, (d_w >= 0) & (d_w <= WINDOW))
    o_w = _dot(p_w.astype(BF16), vw)

    gn = gn_ref[0, 0]
    outs = []
    for h in range(HPG):
        sl = slice(h * Q_BLOCK, (h + 1) * Q_BLOCK)
        outs.append(gn[:, h:h + 1] * o_c[sl] + gn[:, HPG + h:HPG + h + 1] * o_s[sl]
                    + gn[:, 2 * HPG + h:2 * HPG + h + 1] * o_w[sl])
    o_ref[0] = jnp.concatenate(outs, axis=1)


def _gate_pack(gn, b, t):
    g = gn[:, :3 * N_HEADS].reshape(b, t, 3, N_KV, HPG).transpose(0, 3, 1, 2, 4).reshape(b, N_KV, t, 3 * HPG)
    return jnp.pad(g, ((0, 0), (0, 0), (0, 0), (0, 16 - 3 * HPG)))


def _nsa_prompt(q, kvs, kvw, gn, ckv, b, t):
    n_b = t // SLC_BLOCK
    assert n_b <= LANE and t % Q_BLOCK == 0 and t >= WINDOW + Q_BLOCK
    tk = min(512, t)
    pos = jnp.arange(t, dtype=jnp.int32)
    qp = _query_pack(q.reshape(b, t, NSA_WIDTH), pos)
    ks = kvs.reshape(b, t, 2, N_KV, HEAD_DIM).transpose(2, 0, 3, 1, 4)
    kw = kvw.reshape(b, t, 2, N_KV, HEAD_DIM).transpose(2, 0, 3, 1, 4)
    onehot = (pos[:, None] // SLC_BLOCK == jnp.arange(LANE, dtype=jnp.int32)[None, :]).astype(F32)
    kconst = jnp.concatenate([onehot, jnp.zeros((t, HEAD_DIM), F32), _key_consts(pos)], axis=1)
    kp = (jnp.pad(ks[0], ((0, 0), (0, 0), (0, 0), (LANE, HEAD_DIM))) + kconst[None, None]).astype(BF16)
    vw = jnp.concatenate([ks[1], kw[0], kw[1], jnp.zeros_like(kw[1])], axis=-1).astype(BF16)
    n_c = t // CMP_STRIDE
    pool = (jnp.arange(LANE)[:, None] == jnp.arange(n_c)[None, :] // (SLC_BLOCK // CMP_STRIDE)).astype(BF16)
    gn2 = _gate_pack(gn, b, t)
    return pl.pallas_call(
        functools.partial(_nsa_prompt_kernel, tk=tk),
        grid=(b, N_KV, t // Q_BLOCK),
        in_specs=[pl.BlockSpec((1, HPG, Q_BLOCK, LANE), lambda i, g, j: (i, g, j, 0)),
                  pl.BlockSpec((1, 1, Q_BLOCK, 16), lambda i, g, j: (i, g, j, 0)),
                  pl.BlockSpec((1, n_c, 2 * HEAD_DIM), lambda i, g, j: (i, 0, g)),
                  pl.BlockSpec((1, 1, t, 2 * LANE), lambda i, g, j: (i, g, 0, 0)),
                  pl.BlockSpec((1, 1, t, 2 * LANE), lambda i, g, j: (i, g, 0, 0)),
                  pl.BlockSpec((LANE, n_c), lambda i, g, j: (0, 0))],
        out_specs=pl.BlockSpec((1, Q_BLOCK, HPG * HEAD_DIM), lambda i, g, j: (i, j, g)),
        out_shape=jax.ShapeDtypeStruct((b, t, NSA_WIDTH), F32),
        compiler_params=_cparams(3),
        name="nsa_prompt",
    )(qp, gn2, ckv, kp, vw, pool).reshape(b * t, NSA_WIDTH)


def _mem_attend_kernel(q_ref, kv_ref, o_ref):
    q = q_ref[0]
    kv = kv_ref[...].reshape(N_MEM, 2 * MEM_WIDTH)
    scale = MEM_HEAD_DIM ** -0.5
    for h in range(MEM_HEADS):
        sl = slice(h * MEM_HEAD_DIM, (h + 1) * MEM_HEAD_DIM)
        k = kv[:, sl].astype(BF16)
        v = kv[:, MEM_WIDTH + h * MEM_HEAD_DIM:MEM_WIDTH + (h + 1) * MEM_HEAD_DIM].astype(BF16)
        s = _dot_nt(q[:, sl].astype(BF16), k) * scale
        e = jnp.exp(s - jnp.max(s, axis=-1, keepdims=True))
        p = e / jnp.sum(e, axis=-1, keepdims=True)
        o_ref[0, :, sl] = _dot(p.astype(BF16), v)


def _mem_attend(mq, mkv, layer=None):
    b, t, _ = mq.shape
    tm = _row_tile(t, 512)
    if layer is None:
        kv_spec = pl.BlockSpec((1, N_MEM, 2 * MEM_WIDTH), lambda i, j: (i, 0, 0))
    else:
        kv_spec = pl.BlockSpec((1, 1, N_MEM, 2 * MEM_WIDTH), lambda i, j: (layer, i, 0, 0))
    return pl.pallas_call(
        _mem_attend_kernel,
        grid=(b, t // tm),
        in_specs=[pl.BlockSpec((1, tm, MEM_WIDTH), lambda i, j: (i, j, 0)), kv_spec],
        out_specs=pl.BlockSpec((1, tm, MEM_WIDTH), lambda i, j: (i, j, 0)),
        out_shape=jax.ShapeDtypeStruct((b, t, MEM_WIDTH), F32),
        compiler_params=_cparams(2),
        name="mem_attend",
    )(mq, mkv)


def _ssm_disc_kernel(ar_ref, ai_ref, ldt_ref, br_ref, bi_ref, abr_ref, abi_ref, bbr_ref, bbi_ref):
    ar = ar_ref[...]
    ai = ai_ref[...]
    dt = jnp.exp(ldt_ref[...])
    mag = jnp.exp(dt * ar)
    abr = mag * jnp.cos(dt * ai)
    abi = mag * jnp.sin(dt * ai)
    den = ar * ar + ai * ai
    fr = ((abr - 1.0) * ar + abi * ai) / den
    fi = (abi * ar - (abr - 1.0) * ai) / den
    br = br_ref[...]
    bi = bi_ref[...]
    abr_ref[...] = abr
    abi_ref[...] = abi
    bbr_ref[...] = fr * br - fi * bi
    bbi_ref[...] = fr * bi + fi * br


def _ssm_discretise(a_re, a_im, log_dt, b_re, b_im):
    rep = lambda x: jnp.repeat(x, SSM_GROUP, axis=0)
    tb = lambda x: x.transpose(0, 2, 1).reshape(SSM_WIDTH, SSM_STATE)
    shp = jax.ShapeDtypeStruct((SSM_WIDTH, SSM_STATE), F32)
    abr, abi, bbr, bbi = pl.pallas_call(
        _ssm_disc_kernel, out_shape=[shp] * 4, name="ssm_disc",
    )(rep(a_re), rep(a_im), rep(jnp.broadcast_to(log_dt[:, None], a_re.shape)), tb(b_re), tb(b_im))
    eye = jnp.eye(SSM_GPB, dtype=F32)

    def in_map(bb):
        bb = bb.reshape(SSM_SLABS, SSM_GPB, SSM_GROUP, SSM_STATE)
        return jnp.einsum('jgcp,gG->jgcGp', bb, eye).reshape(SSM_SLABS, LANE, SSM_GPB * SSM_STATE).astype(BF16)

    flat = lambda x: x[::SSM_GROUP].reshape(1, SSM_FLAT)
    return flat(abr), flat(abi), in_map(bbr), in_map(bbi)


def _ssm_out_map(c):
    eye = jnp.eye(SSM_GPB, dtype=F32)
    c = c.reshape(SSM_SLABS, SSM_GPB, SSM_GROUP, SSM_STATE)
    return jnp.einsum('jgcp,gG->jgpGc', c, eye).reshape(SSM_SLABS, SSM_GPB * SSM_STATE, LANE).astype(BF16)


def _ssm_kernel(u_ref, h0r_ref, h0i_ref, abr_ref, abi_ref, bdr_ref, bdi_ref, cdr_ref, cdi_ref, d_ref,
                wg_ref, bg_ref, o_ref, hr_ref, hi_ref, sr_ref, si_ref, st_ref, *, n_scan):
    c_idx = pl.program_id(1)
    slab = SSM_GPB * SSM_STATE

    @pl.when(c_idx == 0)
    def _():
        st_ref[0:1, :] = h0r_ref[0]
        st_ref[1:2, :] = h0i_ref[0]

    u = u_ref[0]
    ub = u.astype(BF16)
    for j in range(SSM_SLABS):
        uj = ub[:, j * LANE:(j + 1) * LANE]
        sr_ref[:, j * slab:(j + 1) * slab] = _dot(uj, bdr_ref[j])
        si_ref[:, j * slab:(j + 1) * slab] = _dot(uj, bdi_ref[j])

    ar = abr_ref[...]
    ai = abi_ref[...]

    def step(t, carry):
        hr, hi = carry
        nr = ar * hr - ai * hi + sr_ref[pl.ds(t, 1), :]
        ni = ar * hi + ai * hr + si_ref[pl.ds(t, 1), :]
        sr_ref[pl.ds(t, 1), :] = nr
        si_ref[pl.ds(t, 1), :] = ni
        return nr, ni

    hr, hi = lax.fori_loop(0, n_scan, step, (st_ref[0:1, :], st_ref[1:2, :]))
    st_ref[0:1, :] = hr
    st_ref[1:2, :] = hi
    hr_ref[0] = hr
    hi_ref[0] = hi

    ys = []
    for j in range(SSM_SLABS):
        hrj = sr_ref[:, j * slab:(j + 1) * slab].astype(BF16)
        hij = si_ref[:, j * slab:(j + 1) * slab].astype(BF16)
        ys.append(_dot(hrj, cdr_ref[j]) - _dot(hij, cdi_ref[j]))
    y = jnp.concatenate(ys, axis=1) + d_ref[...] * u
    y = jax.nn.gelu(y)
    o_ref[0] = y * jax.nn.sigmoid(_dot(y.astype(BF16), wg_ref[...]) + bg_ref[...])


def _ssm(u, h0r, h0i, disc, cdr, cdi, d, w_glu, b_glu, t_valid=None):
    b, t, _ = u.shape
    tc = _row_tile(t, 256)
    n_scan = tc if t_valid is None else t_valid
    assert t_valid is None or t == tc
    abr, abi, bdr, bdi = disc
    full = lambda a: pl.BlockSpec(a.shape, lambda i, c: (0,) * a.ndim)
    state_spec = pl.BlockSpec((1, 1, SSM_FLAT), lambda i, c: (i, 0, 0))
    args = (abr, abi, bdr, bdi, cdr, cdi, d.reshape(1, SSM_WIDTH), w_glu.astype(BF16), b_glu.reshape(1, SSM_WIDTH))
    o, hr, hi = pl.pallas_call(
        functools.partial(_ssm_kernel, n_scan=n_scan),
        grid=(b, t // tc),
        in_specs=[pl.BlockSpec((1, tc, SSM_WIDTH), lambda i, c: (i, c, 0)), state_spec, state_spec]
                 + [full(a) for a in args],
        out_specs=[pl.BlockSpec((1, tc, SSM_WIDTH), lambda i, c: (i, c, 0)), state_spec, state_spec],
        out_shape=[jax.ShapeDtypeStruct((b, t, SSM_WIDTH), F32),
                   jax.ShapeDtypeStruct((b, 1, SSM_FLAT), F32),
                   jax.ShapeDtypeStruct((b, 1, SSM_FLAT), F32)],
        scratch_shapes=[pltpu.VMEM((tc, SSM_FLAT), F32), pltpu.VMEM((tc, SSM_FLAT), F32),
                        pltpu.VMEM((8, SSM_FLAT), F32)],
        compiler_params=_cparams(2),
        name="ssm",
    )(u, h0r.reshape(b, 1, SSM_FLAT), h0i.reshape(b, 1, SSM_FLAT), *args)
    shp = (b, N_SSM_GROUPS, SSM_STATE)
    return o, hr.reshape(shp), hi.reshape(shp)


def _merge_kernel(x_ref, oa_ref, ob_ref, oc_ref, gb_ref, w_ref, o_ref):
    acc = x_ref[...]
    for i, r in enumerate((oa_ref, ob_ref, oc_ref)):
        w = r.shape[1]
        off = sum((NSA_WIDTH, SSM_WIDTH, MEM_WIDTH)[:i])
        mixed = (r[...] * gb_ref[:, off:off + w]).astype(BF16)
        acc = acc + _dot(mixed, w_ref[off:off + w, :])
    o_ref[...] = acc


def _merge(x2, o_a, o_b, o_c, gb, w_o_bf16):
    n = x2.shape[0]
    tm = _row_tile(n, 512)
    row = lambda w: pl.BlockSpec((tm, w), lambda i: (i, 0))
    return pl.pallas_call(
        _merge_kernel,
        grid=(n // tm,),
        in_specs=[row(D_MODEL), row(NSA_WIDTH), row(SSM_WIDTH), row(MEM_WIDTH), row(MIX_WIDTH),
                  pl.BlockSpec((MIX_WIDTH, D_MODEL), lambda i: (0, 0))],
        out_specs=row(D_MODEL),
        out_shape=jax.ShapeDtypeStruct((n, D_MODEL), F32),
        compiler_params=_cparams(1),
        name="merge",
    )(x2, o_a, o_b, o_c, gb, w_o_bf16)


def _ffn_kernel(x_ref, g_ref, w1_ref, w2_ref, fg_ref, o_ref, h_ref, acc_ref, *, final_norm):
    k = pl.program_id(1)

    @pl.when(k == 0)
    def _():
        h_ref[...] = _rms(x_ref[...], g_ref[...]).astype(BF16)
        acc_ref[...] = jnp.zeros_like(acc_ref)

    a = jnp.square(jnp.maximum(_dot(h_ref[...], w1_ref[...]), 0.0))
    acc_ref[...] += _dot(a.astype(BF16), w2_ref[...])

    @pl.when(k == pl.num_programs(1) - 1)
    def _():
        y = x_ref[...] + acc_ref[...]
        if final_norm:
            y = _rms(y, fg_ref[...])
        o_ref[...] = y


def _ffn(x2, g, w1_bf16, w2_bf16, final_g, final_norm):
    n = x2.shape[0]
    tm = _row_tile(n, 1024)
    tf = 1024
    return pl.pallas_call(
        functools.partial(_ffn_kernel, final_norm=final_norm),
        grid=(n // tm, D_FF // tf),
        in_specs=[pl.BlockSpec((tm, D_MODEL), lambda i, k: (i, 0)),
                  pl.BlockSpec((1, D_MODEL), lambda i, k: (0, 0)),
                  pl.BlockSpec((D_MODEL, tf), lambda i, k: (0, k)),
                  pl.BlockSpec((tf, D_MODEL), lambda i, k: (k, 0)),
                  pl.BlockSpec((1, D_MODEL), lambda i, k: (0, 0))],
        out_specs=pl.BlockSpec((tm, D_MODEL), lambda i, k: (i, 0)),
        out_shape=jax.ShapeDtypeStruct((n, D_MODEL), F32),
        scratch_shapes=[pltpu.VMEM((tm, D_MODEL), BF16), pltpu.VMEM((tm, D_MODEL), F32)],
        compiler_params=_cparams(2),
        name="ffn",
    )(x2, g.reshape(1, D_MODEL), w1_bf16, w2_bf16, final_g.reshape(1, D_MODEL))


def _layer_params(l, norm1_g, w_in, cmp_pe, cmp_w1, cmp_w2, ssm_a_re, ssm_a_im, ssm_log_dt, ssm_b_re, ssm_b_im,
                  ssm_c_re, ssm_c_im, ssm_d, w_glu, b_glu, mem_norm_g, w_mem_kv, w_o, norm2_g, w_ff1, w_ff2):
    return dict(
        norm1_g=norm1_g[l], w_in=_pack_w_in(w_in[l]),
        pe=_flat_cmp_pe(cmp_pe[l]), w1=_expand_cmp_w1(cmp_w1[l]), w2=_expand_cmp_w2(cmp_w2[l]),
        disc=_ssm_discretise(ssm_a_re[l], ssm_a_im[l], ssm_log_dt[l], ssm_b_re[l], ssm_b_im[l]),
        cdr=_ssm_out_map(ssm_c_re[l]), cdi=_ssm_out_map(ssm_c_im[l]), d=ssm_d[l],
        w_glu=w_glu[l], b_glu=b_glu[l], mem_norm_g=mem_norm_g[l], w_mem_kv=w_mem_kv[l].astype(BF16),
        w_o=w_o[l].astype(BF16), norm2_g=norm2_g[l], w_ff1=w_ff1[l].astype(BF16), w_ff2=w_ff2[l].astype(BF16))


def _prompt_layer(x2, mem2, p, b, t, final_g, final_norm):
    q, kvc, kvs, kvw, u, mq, gb, gn = _in_proj(x2, p['norm1_g'], p['w_in'])
    lh = _cmp_lh(kvc.reshape(b * t // CMP_STRIDE, CMP_ROW), p['pe'], p['w1'])
    ckv = _cmp_tokens_prompt(lh.reshape(b, t // CMP_STRIDE, 4 * KV_WIDTH), p['w2'])
    o_a = _nsa_prompt(q, kvs, kvw, gn, ckv, b, t)
    zero = jnp.zeros((b, N_SSM_GROUPS, SSM_STATE), F32)
    o_b, hr, hi = _ssm(u.reshape(b, t, SSM_WIDTH), zero, zero, p['disc'], p['cdr'], p['cdi'], p['d'],
                       p['w_glu'], p['b_glu'])
    mkv = _norm_mm(mem2, p['mem_norm_g'], p['w_mem_kv'])
    o_c = _mem_attend(mq.reshape(b, t, MEM_WIDTH), mkv.reshape(b, N_MEM, 2 * MEM_WIDTH))
    x2 = _merge(x2, o_a, o_b.reshape(b * t, SSM_WIDTH), o_c.reshape(b * t, MEM_WIDTH), gb, p['w_o'])
    x2 = _ffn(x2, p['norm2_g'], p['w_ff1'], p['w_ff2'], final_g, final_norm)
    kv_shape = (b, t, 2, N_KV, HEAD_DIM)
    wl = min(WINDOW, t)
    return x2, dict(cmp=kvc.reshape(kv_shape), slc=kvs.reshape(kv_shape),
                    win=kvw.reshape(kv_shape)[:, t - wl:], hr=hr, hi=hi,
                    mkv=mkv.reshape(b, N_MEM, 2, MEM_HEADS, MEM_HEAD_DIM))


CHUNKS_PER_PAGE = PAGE_SIZE // CMP_STRIDE


def _cmp_lh_paged_kernel(pt_ref, cache_ref, pe_ref, w_ref, o_ref, x_ref, sem, *, pages, page_base):
    b = pl.program_id(0)
    s = pl.program_id(1)
    n_s = pl.num_programs(1)
    step = b * n_s + s
    slot = step % 2

    def page_copy(bb, ss, i, sl):
        page = page_base + pt_ref[bb, ss * pages + i]
        return pltpu.make_async_copy(cache_ref.at[page], x_ref.at[sl, pl.ds(i * CHUNKS_PER_PAGE, CHUNKS_PER_PAGE)],
                                     sem.at[sl, i])

    def start_all(bb, ss, sl):
        for i in range(pages):
            page_copy(bb, ss, i, sl).start()

    @pl.when(step == 0)
    def _():
        start_all(b, s, slot)

    nxt = step + 1

    @pl.when(nxt < pl.num_programs(0) * n_s)
    def _():
        start_all(nxt // n_s, nxt % n_s, 1 - slot)

    for i in range(pages):
        page_copy(b, s, i, slot).wait()
    o_ref[0] = _cmp_lohi(x_ref[slot], pe_ref, w_ref)


def _cmp_lh_paged(cache_chunks, page_base, page_table, pe_flat, w1_big):
    b, n_pages = page_table.shape
    pages = math.gcd(n_pages, 32)
    rows = pages * CHUNKS_PER_PAGE
    grid_spec = pltpu.PrefetchScalarGridSpec(
        num_scalar_prefetch=1,
        grid=(b, n_pages // pages),
        in_specs=[pl.BlockSpec(memory_space=pl.ANY),
                  pl.BlockSpec((2, 1, CMP_ROW), lambda i, s, pt: (0, 0, 0)),
                  pl.BlockSpec((2, CMP_ROW, 2 * KV_WIDTH), lambda i, s, pt: (0, 0, 0))],
        out_specs=pl.BlockSpec((1, rows, 4 * KV_WIDTH), lambda i, s, pt: (i, s, 0)),
        scratch_shapes=[pltpu.VMEM((2, rows, CMP_ROW), F32), pltpu.SemaphoreType.DMA((2, pages))])
    return pl.pallas_call(
        functools.partial(_cmp_lh_paged_kernel, pages=pages, page_base=page_base),
        grid_spec=grid_spec,
        out_shape=jax.ShapeDtypeStruct((b, n_pages * CHUNKS_PER_PAGE, 4 * KV_WIDTH), F32),
        compiler_params=_cparams(2),
        name="cmp_lh_paged",
    )(page_table, cache_chunks, pe_flat, w1_big)


TOK_PAD = 8


def _nsa_sample_select_kernel(lhc_ref, lht_ref, w2_ref, q_ref, pool_ref, oc_ref, idx_ref, lh_ref, *, past, n_tok):
    n_cache = lhc_ref.shape[1]
    n_tail = lht_ref.shape[1]
    r = lh_ref.shape[0]
    lh_ref[0:n_cache, :] = lhc_ref[0]
    lh_ref[n_cache:n_cache + n_tail, :] = lht_ref[0]
    lh_ref[n_cache + n_tail:r, :] = jnp.zeros((r - n_cache - n_tail, lh_ref.shape[1]), F32)
    ckv = _cmp_combine(lh_ref[...], w2_ref[...])

    rows = HPG * TOK_PAD
    row = lax.broadcasted_iota(jnp.int32, (rows, 1), 0)
    pos_q = past + (row & (TOK_PAD - 1))
    ci = lax.broadcasted_iota(jnp.int32, (1, r), 1)
    d_c = pos_q - (ci * CMP_STRIDE + (CMP_LEN - 1))
    lanes = pool_ref.shape[1]
    ps_all = []
    for g in range(N_KV):
        qg = q_ref[0, g]
        q = qg[:, :HEAD_DIM]
        slope = qg[:, HEAD_DIM:HEAD_DIM + 1].astype(F32)
        ck = ckv[:, g * 2 * HEAD_DIM:g * 2 * HEAD_DIM + HEAD_DIM].astype(BF16)
        cv = ckv[:, g * 2 * HEAD_DIM + HEAD_DIM:(g + 1) * 2 * HEAD_DIM].astype(BF16)
        s = _dot_nt(q, ck) - slope * d_c.astype(F32)
        p_c = _masked_probs(s, d_c >= 0)
        oc_ref[0, g] = _dot(p_c.astype(BF16), cv)
        ps = p_c[0:TOK_PAD]
        for h in range(1, HPG):
            ps = ps + p_c[h * TOK_PAD:(h + 1) * TOK_PAD]
        ps_all.append(ps)
    imp = _pool_blocks(jnp.concatenate(ps_all, axis=0), pool_ref[...])
    t_i = lax.broadcasted_iota(jnp.int32, (N_KV * TOK_PAD, 1), 0) & (TOK_PAD - 1)
    blk = lax.broadcasted_iota(jnp.int32, (1, lanes), 1)
    cur = (past + t_i) // SLC_BLOCK
    valid = blk <= cur
    forced = (blk == 0) | (blk == cur) | (blk == cur - 1)
    score = jnp.where(valid, jnp.where(forced, FORCED_SCORE, imp), -1.0)
    _, picks = _topk_select(score, blk.astype(F32), SLC_TOPK)
    lane = lax.broadcasted_iota(jnp.int32, (1, LANE), 1)
    idx = jnp.zeros((N_KV * TOK_PAD, LANE), jnp.int32)
    for k, pk in enumerate(picks):
        idx = jnp.where(lane == k, pk.astype(jnp.int32), idx)
    for g in range(N_KV):
        idx_ref[0, g] = idx[g * TOK_PAD:(g + 1) * TOK_PAD]


def _nsa_sample_select(lh_cache, lh_tail, w2_big, qsel, past, n_tok):
    b, n_cache, _ = lh_cache.shape
    n_tail = lh_tail.shape[1]
    r = -(-(n_cache + n_tail) // LANE) * LANE
    n_b = past // SLC_BLOCK + 1
    assert n_b >= SLC_TOPK
    lanes = -(-n_b // LANE) * LANE
    pool = (jnp.arange(r)[:, None] // (SLC_BLOCK // CMP_STRIDE) == jnp.arange(lanes)[None, :]).astype(BF16)
    rows = HPG * TOK_PAD
    return pl.pallas_call(
        functools.partial(_nsa_sample_select_kernel, past=past, n_tok=n_tok),
        grid=(b,),
        in_specs=[pl.BlockSpec((1, n_cache, 4 * KV_WIDTH), lambda i: (i, 0, 0)),
                  pl.BlockSpec((1, n_tail, 4 * KV_WIDTH), lambda i: (i, 0, 0)),
                  pl.BlockSpec((2 * KV_WIDTH, 2 * KV_WIDTH), lambda i: (0, 0)),
                  pl.BlockSpec((1, N_KV, rows, LANE), lambda i: (i, 0, 0, 0)),
                  pl.BlockSpec((r, lanes), lambda i: (0, 0))],
        out_specs=[pl.BlockSpec((1, N_KV, rows, HEAD_DIM), lambda i: (i, 0, 0, 0)),
                   pl.BlockSpec((1, N_KV, TOK_PAD, LANE), lambda i: (i, 0, 0, 0))],
        out_shape=[jax.ShapeDtypeStruct((b, N_KV, rows, HEAD_DIM), F32),
                   jax.ShapeDtypeStruct((b, N_KV, TOK_PAD, LANE), jnp.int32)],
        scratch_shapes=[pltpu.VMEM((r, 4 * KV_WIDTH), F32)],
        compiler_params=_cparams(1),
        name="nsa_sample_select",
    )(lh_cache, lh_tail, w2_big, qsel, pool)


def _rows_minor(cache):
    n = cache.ndim
    return cache.transpose(tuple(range(n - 4)) + (n - 3, n - 2, n - 1, n - 4))


def _nsa_sample_attend_kernel(idx_ref, pt_ref, cache_ref, q_ref, oc_ref, gate_ref, snew_ref, win_ref, wnew_ref,
                              o_ref, kbuf_ref, vbuf_ref, sem, *, layer, past, n_tok):
    b = pl.program_id(0)
    n_pages = past // PAGE_SIZE
    cur_blk = past // SLC_BLOCK
    per_page = PAGE_SIZE // SLC_BLOCK
    n_sel = SLC_TOPK * PAGE_SIZE

    def sel_block(g, t, k):
        return idx_ref[((b * N_KV + g) * n_tok + t) * SLC_TOPK + k]

    def page_copies(g, t, k):
        c = jnp.minimum(sel_block(g, t, k), cur_blk - 1)
        page = pt_ref[b * n_pages + c // per_page]
        slot = g * n_tok + t
        dst = pl.ds(k * PAGE_SIZE, PAGE_SIZE)
        return (pltpu.make_async_copy(cache_ref.at[layer, page, 0, g], kbuf_ref.at[slot, :, dst], sem.at[0, slot]),
                pltpu.make_async_copy(cache_ref.at[layer, page, 1, g], vbuf_ref.at[slot, :, dst], sem.at[1, slot]))

    for g in range(N_KV):
        for t in range(n_tok):
            for k in range(SLC_TOPK):
                for cp in page_copies(g, t, k):
                    cp.start()

    rows = HPG * TOK_PAD
    row = lax.broadcasted_iota(jnp.int32, (rows, 1), 0)
    tok = row & (TOK_PAD - 1)
    pos_q = past + tok
    lane = lax.broadcasted_iota(jnp.int32, (1, n_sel), 1)
    lane_slot = lane // PAGE_SIZE
    lane_blk = (lane & (PAGE_SIZE - 1)) // SLC_BLOCK
    r_new = lax.broadcasted_iota(jnp.int32, (1, TOK_PAD), 1)
    d_new = pos_q - (past + r_new)
    mask_new = (d_new >= 0) & (r_new < n_tok)
    wb = win_ref.shape[-1]
    d_win = pos_q - (past - wb + lax.broadcasted_iota(jnp.int32, (1, wb), 1))
    mask_win = (d_win >= 0) & (d_win <= WINDOW)

    def joint_attend(q, slope, kt_old, vt_old, d_old, mask_old, k_new, v_new):
        s_o = jnp.where(mask_old, _dot(q, kt_old) - slope * d_old.astype(F32), NEG_INF)
        s_n = jnp.where(mask_new, _dot_nt(q, k_new) - slope * d_new.astype(F32), NEG_INF)
        m = jnp.maximum(jnp.max(s_o, axis=-1, keepdims=True), jnp.max(s_n, axis=-1, keepdims=True))
        e_o = jnp.where(mask_old, jnp.exp(s_o - m), 0.0)
        e_n = jnp.where(mask_new, jnp.exp(s_n - m), 0.0)
        l = jnp.sum(e_o, axis=-1, keepdims=True) + jnp.sum(e_n, axis=-1, keepdims=True)
        return (_dot_nt(e_o.astype(BF16), vt_old) + _dot(e_n.astype(BF16), v_new)) / jnp.maximum(l, 1e-30)

    for g in range(N_KV):
        ksl = slice(g * HEAD_DIM, (g + 1) * HEAD_DIM)
        vsl = slice(KV_WIDTH + g * HEAD_DIM, KV_WIDTH + (g + 1) * HEAD_DIM)
        qg = q_ref[0, g]
        q = qg[:, :HEAD_DIM]
        slope = qg[:, HEAD_DIM:HEAD_DIM + 1].astype(F32)
        k_snew = snew_ref[0][:, ksl].astype(BF16)
        v_snew = snew_ref[0][:, vsl].astype(BF16)
        o_s = jnp.zeros((rows, HEAD_DIM), F32)
        for t in range(n_tok):
            for k in range(SLC_TOPK):
                for cp in page_copies(g, t, k):
                    cp.wait()
            blk_of_lane = jnp.zeros((1, n_sel), jnp.int32)
            for k in range(SLC_TOPK):
                blk_of_lane = jnp.where(lane_slot == k, sel_block(g, t, k), blk_of_lane)
            d_sel = pos_q - (blk_of_lane * SLC_BLOCK + (lane & (SLC_BLOCK - 1)))
            mask_sel = (d_sel >= 0) & (blk_of_lane < cur_blk) & (blk_of_lane % per_page == lane_blk)
            o_t = joint_attend(q, slope, kbuf_ref[g * n_tok + t].astype(BF16), vbuf_ref[g * n_tok + t].astype(BF16),
                               d_sel, mask_sel, k_snew, v_snew)
            o_s = jnp.where(tok == t, o_t, o_s)
        o_w = joint_attend(q, slope, win_ref[0, 0, 0, g].astype(BF16), win_ref[0, 0, 1, g].astype(BF16), d_win,
                           mask_win, wnew_ref[0][:, ksl].astype(BF16), wnew_ref[0][:, vsl].astype(BF16))
        gate = gate_ref[0, g]
        o_ref[0, g] = gate[:, 0:1] * oc_ref[0, g] + gate[:, 1:2] * o_s + gate[:, 2:3] * o_w


def _nsa_sample_attend(idx, page_table, cache_slc_t, cache_win_t, layer, qsel, o_c, gates, slc_new, win_new, past, n_tok):
    b = qsel.shape[0]
    assert past % PAGE_SIZE == 0 and n_tok <= TOK_PAD
    rows = HPG * TOK_PAD
    wb = cache_win_t.shape[-1]
    per_b = lambda shape: pl.BlockSpec((1,) + shape, lambda i, *_: (i,) + (0,) * len(shape))
    grid_spec = pltpu.PrefetchScalarGridSpec(
        num_scalar_prefetch=2,
        grid=(b,),
        in_specs=[pl.BlockSpec(memory_space=pl.ANY),
                  per_b((N_KV, rows, LANE)), per_b((N_KV, rows, HEAD_DIM)), per_b((N_KV, rows, 16)),
                  per_b((TOK_PAD, 2 * KV_WIDTH)),
                  pl.BlockSpec((1, 1, 2, N_KV, HEAD_DIM, wb), lambda i, *_: (layer, i, 0, 0, 0, 0)),
                  per_b((TOK_PAD, 2 * KV_WIDTH))],
        out_specs=per_b((N_KV, rows, HEAD_DIM)),
        scratch_shapes=[pltpu.VMEM((N_KV * n_tok, HEAD_DIM, SLC_TOPK * PAGE_SIZE), F32),
                        pltpu.VMEM((N_KV * n_tok, HEAD_DIM, SLC_TOPK * PAGE_SIZE), F32),
                        pltpu.SemaphoreType.DMA((2, N_KV * n_tok))])
    return pl.pallas_call(
        functools.partial(_nsa_sample_attend_kernel, layer=layer, past=past, n_tok=n_tok),
        grid_spec=grid_spec,
        out_shape=jax.ShapeDtypeStruct((b, N_KV, rows, HEAD_DIM), F32),
        compiler_params=_cparams(1),
        name="nsa_sample_attend",
    )(idx[:, :, :n_tok, :SLC_TOPK].reshape(-1), page_table.reshape(-1), cache_slc_t, qsel, o_c, gates, slc_new,
      cache_win_t, win_new)


def _pad_tokens(x, axis):
    pad = [(0, 0)] * x.ndim
    pad[axis] = (0, TOK_PAD - x.shape[axis])
    return jnp.pad(x, pad)


def _sample_layer(x2, p, b, s, layer, cmp_chunks, n_phys, cache_slc_t, cache_win_t, cache_win, h0r, h0i, cache_mem,
                  page_table, final_g, final_norm):
    past = page_table.shape[1] * PAGE_SIZE
    assert past % SLC_BLOCK == 0 and s <= CMP_STRIDE
    q, kvc, kvs, kvw, u, mq, gb, gn = _in_proj(x2, p['norm1_g'], p['w_in'])
    lh_cache = _cmp_lh_paged(cmp_chunks, layer * n_phys, page_table, p['pe'], p['w1'])
    tail = jnp.pad(kvc.reshape(b, s, 2 * KV_WIDTH), ((0, 0), (0, CHUNKS_PER_PAGE * CMP_STRIDE - s), (0, 0)))
    lh_tail = _cmp_lh(tail.reshape(b * CHUNKS_PER_PAGE, CMP_ROW), p['pe'], p['w1'])
    pos = past + jnp.arange(s, dtype=jnp.int32)
    qsel = _pad_tokens(_query_pack(q.reshape(b, s, NSA_WIDTH), pos), 2).reshape(b, N_KV, HPG * TOK_PAD, LANE)
    o_c, idx = _nsa_sample_select(lh_cache, lh_tail.reshape(b, CHUNKS_PER_PAGE, 4 * KV_WIDTH), p['w2'], qsel, past, s)
    gates = gn[:, :3 * N_HEADS].reshape(b, s, 3, N_KV, HPG).transpose(0, 3, 4, 1, 2)
    gates = jnp.pad(_pad_tokens(gates, 3), ((0, 0),) * 4 + ((0, 16 - 3),)).reshape(b, N_KV, HPG * TOK_PAD, 16)
    slc_new = kvs.reshape(b, s, 2 * KV_WIDTH)
    win_new = kvw.reshape(b, s, 2 * KV_WIDTH)
    o = _nsa_sample_attend(idx, page_table, cache_slc_t, cache_win_t, layer, qsel, o_c, gates,
                           _pad_tokens(slc_new, 1), _pad_tokens(win_new, 1), past, s)
    o_a = o.reshape(b, N_KV, HPG, TOK_PAD, HEAD_DIM)[:, :, :, :s].transpose(0, 3, 1, 2, 4).reshape(b * s, NSA_WIDTH)
    o_b, hr, hi = _ssm(_pad_tokens(u.reshape(b, s, SSM_WIDTH), 1), h0r, h0i, p['disc'], p['cdr'], p['cdi'], p['d'],
                       p['w_glu'], p['b_glu'], t_valid=s)
    o_c2 = _mem_attend(_pad_tokens(mq.reshape(b, s, MEM_WIDTH), 1), cache_mem, layer)
    x2 = _merge(x2, o_a, o_b[:, :s].reshape(b * s, SSM_WIDTH), o_c2[:, :s].reshape(b * s, MEM_WIDTH), gb, p['w_o'])
    x2 = _ffn(x2, p['norm2_g'], p['w_ff1'], p['w_ff2'], final_g, final_norm)
    kv_shape = (b, s, 2, N_KV, HEAD_DIM)
    new_win = jnp.concatenate([cache_win, kvw.reshape(kv_shape).astype(cache_win.dtype)], axis=1)[:, s:]
    return x2, dict(cmp=kvc.reshape(kv_shape), slc=kvs.reshape(kv_shape), win=new_win, hr=hr, hi=hi)


def kernel(x_prompt, x_sample, cache_cmp_kv, cache_slc_kv, cache_win_kv, state_ssm_re, state_ssm_im, cache_mem_kv,
           page_table, mem_prompt, norm1_g, w_in, cmp_pe, cmp_w1, cmp_w2, ssm_a_re, ssm_a_im, ssm_log_dt, ssm_b_re,
           ssm_b_im, ssm_c_re, ssm_c_im, ssm_d, w_glu, b_glu, mem_norm_g, w_mem_kv, w_o, norm2_g, w_ff1, w_ff2,
           final_norm_g):
    bp, t, _ = x_prompt.shape
    bs, s, _ = x_sample.shape
    depth = w_in.shape[0]
    xp = x_prompt.reshape(bp * t, D_MODEL)
    xs = x_sample.reshape(bs * s, D_MODEL)
    mem2 = mem_prompt.reshape(bp * N_MEM, D_MODEL)
    n_phys = cache_cmp_kv.shape[1]
    cmp_chunks = cache_cmp_kv.reshape(depth * n_phys, CHUNKS_PER_PAGE, CMP_ROW)
    cache_slc_t = _rows_minor(cache_slc_kv)
    cache_win_t = _rows_minor(cache_win_kv)
    cache_mem = cache_mem_kv.reshape(depth, bs, N_MEM, 2 * MEM_WIDTH)
    aux_p, aux_s = [], []
    for l in range(depth):
        p = _layer_params(l, norm1_g, w_in, cmp_pe, cmp_w1, cmp_w2, ssm_a_re, ssm_a_im, ssm_log_dt, ssm_b_re,
                          ssm_b_im, ssm_c_re, ssm_c_im, ssm_d, w_glu, b_glu, mem_norm_g, w_mem_kv, w_o, norm2_g,
                          w_ff1, w_ff2)
        last = l == depth - 1
        xp, ap = _prompt_layer(xp, mem2, p, bp, t, final_norm_g, last)
        xs, a_s = _sample_layer(xs, p, bs, s, l, cmp_chunks, n_phys, cache_slc_t, cache_win_t, cache_win_kv[l],
                                state_ssm_re[l], state_ssm_im[l], cache_mem, page_table, final_norm_g, last)
        aux_p.append(ap)
        aux_s.append(a_s)
    stack = lambda aux, key: jnp.stack([a[key] for a in aux])
    return (xp.reshape(bp, t, D_MODEL), xs.reshape(bs, s, D_MODEL),
            stack(aux_p, 'cmp'), stack(aux_s, 'cmp'), stack(aux_p, 'slc'), stack(aux_s, 'slc'),
            stack(aux_p, 'win'), stack(aux_s, 'win'), stack(aux_p, 'hr'), stack(aux_p, 'hi'),
            stack(aux_s, 'hr'), stack(aux_s, 'hi'), stack(aux_p, 'mkv'))
```

```python
import functools
import math

import jax
import jax.numpy as jnp
from jax import lax
from jax.experimental import pallas as pl
from jax.experimental.pallas import tpu as pltpu

F32 = jnp.float32
BF16 = jnp.bfloat16

D_MODEL = 1024
PAGE_SIZE = 128
N_HEADS = 8
HEAD_DIM = 64
N_KV = 2
HPG = N_HEADS // N_KV
CMP_STRIDE = 16
CMP_LEN = 2 * CMP_STRIDE
SLC_BLOCK = 64
SLC_TOPK = 16
WINDOW = 512
Q_BLOCK = 128
SSM_WIDTH = D_MODEL // 2
SSM_GROUP = 16
N_SSM_GROUPS = SSM_WIDTH // SSM_GROUP
SSM_STATE = 64
SSM_FLAT = N_SSM_GROUPS * SSM_STATE
N_MEM = 256
MEM_HEADS = 4
MEM_HEAD_DIM = 128
MEM_WIDTH = MEM_HEADS * MEM_HEAD_DIM
D_FF = 4 * D_MODEL
NSA_WIDTH = N_HEADS * HEAD_DIM
KV_WIDTH = N_KV * HEAD_DIM
MIX_WIDTH = NSA_WIDTH + SSM_WIDTH + MEM_WIDTH
OFF_KV = NSA_WIDTH
OFF_GN = OFF_KV + 6 * KV_WIDTH
OFF_U = OFF_GN + 3 * N_HEADS
OFF_MQ = OFF_U + SSM_WIDTH
OFF_GB = OFF_MQ + MEM_WIDTH
IN_WIDTH = OFF_GB + MIX_WIDTH
EPS = 1e-6
NEG_INF = -1e30
FORCED_SCORE = 1e9

LANE = 128
CMP_ROW = CMP_STRIDE * 2 * KV_WIDTH
SSM_GPB = LANE // SSM_GROUP
SSM_SLABS = SSM_WIDTH // LANE
VMEM_LIMIT = 56 * 2**20


def _cparams(n_grid, vmem=VMEM_LIMIT):
    return pltpu.CompilerParams(dimension_semantics=("arbitrary",) * n_grid, vmem_limit_bytes=vmem)


def _rms(x, g):
    return x * lax.rsqrt(jnp.mean(x * x, axis=-1, keepdims=True) + EPS) * g


def _dot(a, b):
    return jnp.dot(a, b, preferred_element_type=F32)


def _dot_nt(a, b):
    return lax.dot_general(a, b, (((1,), (1,)), ((), ())), preferred_element_type=F32)


def _masked_probs(s, mask):
    s = jnp.where(mask, s, NEG_INF)
    m = jnp.max(s, axis=-1, keepdims=True)
    e = jnp.where(mask, jnp.exp(s - m), 0.0)
    return e / jnp.maximum(jnp.sum(e, axis=-1, keepdims=True), 1e-30)


def _row_tile(n, pref):
    t = min(n, pref)
    assert n % t == 0
    return t


_SEGS = ((0, 512), (512, 768), (768, 1024), (1024, 1280), (1280, 1792), (1792, 2304), (2304, 3840), (3840, 3968))
_SIGMOID_FROM = 2304
_PACKED_WIDTH = 3968


def _pack_w_in(w):
    gn = jnp.pad(w[:, OFF_GN:OFF_U], ((0, 0), (0, LANE - 3 * N_HEADS)))
    return jnp.concatenate([w[:, :OFF_GN], w[:, OFF_U:], gn], axis=1).astype(BF16)


def _inproj_kernel(x_ref, g_ref, w_ref, *o_refs):
    h = _rms(x_ref[...], g_ref[...]).astype(BF16)
    for (a, b), o_ref in zip(_SEGS, o_refs):
        z = _dot(h, w_ref[:, a:b])
        if a >= _SIGMOID_FROM:
            z = jax.nn.sigmoid(z)
        o_ref[...] = z


def _in_proj(x2, g, w_packed):
    n = x2.shape[0]
    tm = _row_tile(n, 512)
    return pl.pallas_call(
        _inproj_kernel,
        grid=(n // tm,),
        in_specs=[pl.BlockSpec((tm, D_MODEL), lambda i: (i, 0)),
                  pl.BlockSpec((1, D_MODEL), lambda i: (0, 0)),
                  pl.BlockSpec((D_MODEL, _PACKED_WIDTH), lambda i: (0, 0))],
        out_specs=[pl.BlockSpec((tm, b - a), lambda i: (i, 0)) for a, b in _SEGS],
        out_shape=[jax.ShapeDtypeStruct((n, b - a), F32) for a, b in _SEGS],
        compiler_params=_cparams(1),
        name="in_proj",
    )(x2, g.reshape(1, D_MODEL), w_packed)


def _norm_mm_kernel(x_ref, g_ref, w_ref, o_ref):
    o_ref[...] = _dot(_rms(x_ref[...], g_ref[...]).astype(BF16), w_ref[...])


def _norm_mm(x2, g, w_bf16):
    n, k = x2.shape
    m = w_bf16.shape[1]
    tm = _row_tile(n, 256)
    return pl.pallas_call(
        _norm_mm_kernel,
        grid=(n // tm,),
        in_specs=[pl.BlockSpec((tm, k), lambda i: (i, 0)),
                  pl.BlockSpec((1, k), lambda i: (0, 0)),
                  pl.BlockSpec((k, m), lambda i: (0, 0))],
        out_specs=pl.BlockSpec((tm, m), lambda i: (i, 0)),
        out_shape=jax.ShapeDtypeStruct((n, m), F32),
        compiler_params=_cparams(1),
        name="norm_mm",
    )(x2, g.reshape(1, k), w_bf16)


def _expand_cmp_w1(w1):
    w = w1.reshape(2, 2, CMP_STRIDE, HEAD_DIM, HEAD_DIM)
    eye2 = jnp.eye(2, dtype=w1.dtype)
    big = jnp.einsum('khlde,kK,gG->hlkgdKGe', w, eye2, eye2)
    return big.reshape(2, CMP_ROW, 2 * KV_WIDTH).astype(BF16)


def _flat_cmp_pe(pe):
    p = pe.reshape(2, 2, CMP_STRIDE, HEAD_DIM)
    p = jnp.broadcast_to(p.transpose(1, 2, 0, 3)[:, :, :, None, :], (2, CMP_STRIDE, 2, N_KV, HEAD_DIM))
    return p.reshape(2, 1, CMP_ROW)


def _expand_cmp_w2(w2):
    eye2 = jnp.eye(2, dtype=w2.dtype)
    big = jnp.einsum('kde,kK,gG->kgdGKe', w2, eye2, eye2)
    return big.reshape(2 * KV_WIDTH, 2 * KV_WIDTH).astype(BF16)


def _cmp_lohi(x, pe_ref, w_ref):
    lo = _dot((x + pe_ref[0]).astype(BF16), w_ref[0])
    hi = _dot((x + pe_ref[1]).astype(BF16), w_ref[1])
    return jnp.concatenate([lo, hi], axis=1)


def _cmp_lh_kernel(x_ref, pe_ref, w_ref, o_ref):
    o_ref[...] = _cmp_lohi(x_ref[...], pe_ref, w_ref)


def _cmp_lh(x2, pe_flat, w1_big):
    n = x2.shape[0]
    tm = _row_tile(n, 256)
    return pl.pallas_call(
        _cmp_lh_kernel,
        grid=(n // tm,),
        in_specs=[pl.BlockSpec((tm, CMP_ROW), lambda i: (i, 0)),
                  pl.BlockSpec((2, 1, CMP_ROW), lambda i: (0, 0, 0)),
                  pl.BlockSpec((2, CMP_ROW, 2 * KV_WIDTH), lambda i: (0, 0, 0))],
        out_specs=pl.BlockSpec((tm, 4 * KV_WIDTH), lambda i: (i, 0)),
        out_shape=jax.ShapeDtypeStruct((n, 4 * KV_WIDTH), F32),
        compiler_params=_cparams(1),
        name="cmp_lh",
    )(x2, pe_flat, w1_big)


def _cmp_combine(lh, w2_big):
    r = lh.shape[0]
    hi_next = pltpu.roll(lh[:, 2 * KV_WIDTH:], r - 1, 0)
    return _dot(jax.nn.gelu(lh[:, :2 * KV_WIDTH] + hi_next).astype(BF16), w2_big)


def _cmp_combine_kernel(lh_ref, w2_ref, o_ref):
    o_ref[0] = _cmp_combine(lh_ref[0], w2_ref[...])


def _cmp_tokens_prompt(lh, w2_big):
    b, r, _ = lh.shape
    return pl.pallas_call(
        _cmp_combine_kernel,
        grid=(b,),
        in_specs=[pl.BlockSpec((1, r, 4 * KV_WIDTH), lambda i: (i, 0, 0)),
                  pl.BlockSpec((2 * KV_WIDTH, 2 * KV_WIDTH), lambda i: (0, 0))],
        out_specs=pl.BlockSpec((1, r, 2 * KV_WIDTH), lambda i: (i, 0, 0)),
        out_shape=jax.ShapeDtypeStruct((b, r, 2 * KV_WIDTH), F32),
        compiler_params=_cparams(1),
        name="cmp_combine",
    )(lh, w2_big)


def _alibi_slopes():
    return jnp.exp2(-8.0 * (jnp.arange(N_HEADS, dtype=F32) + 1.0) / N_HEADS)


def _query_pack(q, pos):
    b, t, _ = q.shape
    qh = q.reshape(b, t, N_HEADS, HEAD_DIM).transpose(0, 2, 1, 3) * (HEAD_DIM ** -0.5)
    slope = _alibi_slopes()[:, None]
    pos_hi = ((pos // SLC_BLOCK) * SLC_BLOCK).astype(F32)[None, :]
    pos_lo = (pos % SLC_BLOCK).astype(F32)[None, :]
    ones = jnp.ones((1, t), F32)
    c = jnp.stack([slope * ones, slope * ones, -slope * pos_hi, -slope * pos_lo], axis=-1)
    c = jnp.pad(c, ((0, 0), (0, 0), (0, HEAD_DIM - 4)))
    c = jnp.broadcast_to(c[None], (b, N_HEADS, t, HEAD_DIM))
    return jnp.concatenate([qh, c], axis=-1).astype(BF16)


def _key_consts(kpos):
    hi = ((kpos // SLC_BLOCK) * SLC_BLOCK).astype(F32)
    lo = (kpos % SLC_BLOCK).astype(F32)
    one = jnp.ones_like(hi)
    return jnp.pad(jnp.stack([hi, lo, one, one], axis=-1), ((0, 0), (0, HEAD_DIM - 4)))


def _split3(x):
    hi = x.astype(BF16)
    r1 = x - hi.astype(F32)
    mid = r1.astype(BF16)
    lo = (r1 - mid.astype(F32)).astype(BF16)
    return hi, mid, lo


def _pool_blocks_t(ps, pool_t):
    hi, mid, lo = _split3(ps)
    return _dot_nt(pool_t, hi) + _dot_nt(pool_t, mid) + _dot_nt(pool_t, lo)


def _topk_select_cols(score, blk_f, k):
    sel = jnp.zeros(score.shape, F32)
    picks = []
    for _ in range(k):
        m = jnp.max(score, axis=0, keepdims=True)
        idx = jnp.min(jnp.where(score == m, blk_f, 1e9), axis=0, keepdims=True)
        pick = blk_f == idx
        sel = jnp.where(pick, 1.0, sel)
        score = jnp.where(pick, -3e38, score)
        picks.append(idx)
    return sel, picks


def _add_token_bias(s, bias):
    rows, n = s.shape
    q = bias.shape[0]
    return (s.reshape(rows // q, q, n) + bias[None]).reshape(rows, n)


def _biased_probs(s, bias, row_any=None):
    x = _add_token_bias(s, bias)
    e = jnp.exp(x - jnp.max(x, axis=-1, keepdims=True))
    inv = 1.0 / jnp.maximum(jnp.sum(e, axis=-1, keepdims=True), 1e-30)
    if row_any is not None:
        inv = jnp.where(row_any, inv, 0.0)
    return e * inv


def _nsa_prompt_kernel(qp_ref, gn_ref, ckp_ref, kp_ref, vw_ref, pool_ref, o_ref, *, tk):
    j = pl.program_id(2)
    start = j * Q_BLOCK
    rows = HPG * Q_BLOCK
    qp = qp_ref[0].reshape(rows, LANE)
    pos_t = start + lax.broadcasted_iota(jnp.int32, (Q_BLOCK, 1), 0)
    row = lax.broadcasted_iota(jnp.int32, (rows, 1), 0)

    ckp = ckp_ref[0, 0]
    n_c = ckp.shape[0]
    c_end = lax.broadcasted_iota(jnp.int32, (1, n_c), 1) * CMP_STRIDE + (CMP_LEN - 1)
    bias_c = jnp.where(c_end <= pos_t, 0.0, NEG_INF)
    any_c = start + (row & (Q_BLOCK - 1)) >= CMP_LEN - 1
    p_c = _biased_probs(_dot_nt(qp, ckp[:, :LANE]), bias_c, any_c)
    o_c = _dot(p_c.astype(BF16), ckp[:, LANE:LANE + HEAD_DIM])
    ps = p_c[0:Q_BLOCK]
    for h in range(1, HPG):
        ps = ps + p_c[h * Q_BLOCK:(h + 1) * Q_BLOCK]
    imp_t = _pool_blocks_t(ps, pool_ref[...])
    blk = lax.broadcasted_iota(jnp.int32, (LANE, 1), 0)
    cur = (start + lax.broadcasted_iota(jnp.int32, (1, Q_BLOCK), 1)) // SLC_BLOCK
    valid = blk <= cur
    forced = (blk == 0) | (blk == cur) | (blk == cur - 1)
    score = jnp.where(valid, jnp.where(forced, FORCED_SCORE, imp_t), -1.0)
    sel_t, _ = _topk_select_cols(score, blk.astype(F32), SLC_TOPK)
    mb = jnp.where((sel_t > 0.0) & valid, 0.0, NEG_INF).T.astype(BF16)
    qs = jnp.concatenate([jnp.concatenate([mb] * HPG, axis=0), qp], axis=1)

    def tile(i, carry, diag):
        m, l, acc = carry
        off = pl.multiple_of(i * tk, tk)
        s = _dot_nt(qs, kp_ref[0, 0, pl.ds(off, tk), :])
        if diag:
            kpos = off + lax.broadcasted_iota(jnp.int32, (1, tk), 1)
            s = _add_token_bias(s, jnp.where(kpos <= pos_t, 0.0, NEG_INF))
        m2 = jnp.maximum(m, jnp.max(s, axis=-1, keepdims=True))
        a = jnp.exp(m - m2)
        e = jnp.exp(s - m2)
        l = a * l + jnp.sum(e, axis=-1, keepdims=True)
        acc = a * acc + _dot(e.astype(BF16), vw_ref[0, 0, pl.ds(off, tk), 0:HEAD_DIM])
        return m2, l, acc

    n_full = start // tk
    carry = (jnp.full((rows, 1), NEG_INF, F32), jnp.zeros((rows, 1), F32), jnp.zeros((rows, HEAD_DIM), F32))
    carry = lax.fori_loop(0, n_full, lambda i, c: tile(i, c, False), carry)
    _, l_s, acc_s = tile(n_full, carry, True)
    o_s = acc_s / jnp.maximum(l_s, 1e-30)

    wlen = WINDOW + Q_BLOCK
    woff = pl.multiple_of(jnp.maximum(start - WINDOW, 0), Q_BLOCK)
    d_w = pos_t - (woff + lax.broadcasted_iota(jnp.int32, (1, wlen), 1))
    bias_w = jnp.where((d_w >= 0) & (d_w <= WINDOW), 0.0, NEG_INF)
    p_w = _biased_probs(_dot_nt(qp, vw_ref[0, 0, pl.ds(woff, wlen), LANE:2 * LANE]), bias_w)
    o_w = _dot(p_w.astype(BF16), vw_ref[0, 0, pl.ds(woff, wlen), HEAD_DIM:2 * HEAD_DIM])

    gn = gn_ref[0, 0]
    outs = []
    for h in range(HPG):
        sl = slice(h * Q_BLOCK, (h + 1) * Q_BLOCK)
        outs.append(gn[:, h:h + 1] * o_c[sl] + gn[:, HPG + h:HPG + h + 1] * o_s[sl]
                    + gn[:, 2 * HPG + h:2 * HPG + h + 1] * o_w[sl])
    o_ref[0] = jnp.concatenate(outs, axis=1)


def _gate_pack(gn, b, t):
    g = gn[:, :3 * N_HEADS].reshape(b, t, 3, N_KV, HPG).transpose(0, 3, 1, 2, 4).reshape(b, N_KV, t, 3 * HPG)
    return jnp.pad(g, ((0, 0), (0, 0), (0, 0), (0, 16 - 3 * HPG)))


def _nsa_prompt(q, kvs, kvw, gn, ckv, b, t):
    n_b = t // SLC_BLOCK
    assert n_b <= LANE and t % Q_BLOCK == 0 and t >= WINDOW + Q_BLOCK
    tk = min(512, t)
    pos = jnp.arange(t, dtype=jnp.int32)
    qp = _query_pack(q.reshape(b, t, NSA_WIDTH), pos)
    ks = kvs.reshape(b, t, 2, N_KV, HEAD_DIM).transpose(2, 0, 3, 1, 4)
    kw = kvw.reshape(b, t, 2, N_KV, HEAD_DIM).transpose(2, 0, 3, 1, 4)
    onehot = (pos[:, None] // SLC_BLOCK == jnp.arange(LANE, dtype=jnp.int32)[None, :]).astype(F32)
    kconst = jnp.concatenate([onehot, jnp.zeros((t, HEAD_DIM), F32), _key_consts(pos)], axis=1)
    kp = (jnp.pad(ks[0], ((0, 0), (0, 0), (0, 0), (LANE, HEAD_DIM))) + kconst[None, None]).astype(BF16)
    vw = jnp.concatenate([ks[1], kw[1], kw[0], jnp.broadcast_to(_key_consts(pos), kw[0].shape)], axis=-1).astype(BF16)
    n_c = t // CMP_STRIDE
    c4 = ckv.reshape(b, n_c, N_KV, 2, HEAD_DIM).transpose(3, 0, 2, 1, 4)
    c_end = jnp.arange(n_c, dtype=jnp.int32) * CMP_STRIDE + (CMP_LEN - 1)
    ckp = jnp.concatenate([c4[0], jnp.broadcast_to(_key_consts(c_end), c4[0].shape), c4[1], jnp.zeros_like(c4[1])],
                          axis=-1).astype(BF16)
    pool = (jnp.arange(LANE)[:, None] == jnp.arange(n_c)[None, :] // (SLC_BLOCK // CMP_STRIDE)).astype(BF16)
    gn2 = _gate_pack(gn, b, t)
    return pl.pallas_call(
        functools.partial(_nsa_prompt_kernel, tk=tk),
        grid=(b, N_KV, t // Q_BLOCK),
        in_specs=[pl.BlockSpec((1, HPG, Q_BLOCK, LANE), lambda i, g, j: (i, g, j, 0)),
                  pl.BlockSpec((1, 1, Q_BLOCK, 16), lambda i, g, j: (i, g, j, 0)),
                  pl.BlockSpec((1, 1, n_c, 2 * LANE), lambda i, g, j: (i, g, 0, 0)),
                  pl.BlockSpec((1, 1, t, 2 * LANE), lambda i, g, j: (i, g, 0, 0)),
                  pl.BlockSpec((1, 1, t, 2 * LANE), lambda i, g, j: (i, g, 0, 0)),
                  pl.BlockSpec((LANE, n_c), lambda i, g, j: (0, 0))],
        out_specs=pl.BlockSpec((1, Q_BLOCK, HPG * HEAD_DIM), lambda i, g, j: (i, j, g)),
        out_shape=jax.ShapeDtypeStruct((b, t, NSA_WIDTH), F32),
        compiler_params=_cparams(3),
        name="nsa_prompt",
    )(qp, gn2, ckp, kp, vw, pool).reshape(b * t, NSA_WIDTH)


def _mem_attend_kernel(q_ref, kv_ref, o_ref):
    q = q_ref[0]
    kv = kv_ref[...].reshape(N_MEM, 2 * MEM_WIDTH)
    scale = MEM_HEAD_DIM ** -0.5
    for h in range(MEM_HEADS):
        sl = slice(h * MEM_HEAD_DIM, (h + 1) * MEM_HEAD_DIM)
        k = kv[:, sl].astype(BF16)
        v = kv[:, MEM_WIDTH + h * MEM_HEAD_DIM:MEM_WIDTH + (h + 1) * MEM_HEAD_DIM].astype(BF16)
        s = _dot_nt(q[:, sl].astype(BF16), k) * scale
        e = jnp.exp(s - jnp.max(s, axis=-1, keepdims=True))
        p = e / jnp.sum(e, axis=-1, keepdims=True)
        o_ref[0, :, sl] = _dot(p.astype(BF16), v)


def _mem_attend(mq, mkv, layer=None):
    b, t, _ = mq.shape
    tm = _row_tile(t, 512)
    if layer is None:
        kv_spec = pl.BlockSpec((1, N_MEM, 2 * MEM_WIDTH), lambda i, j: (i, 0, 0))
    else:
        kv_spec = pl.BlockSpec((1, 1, N_MEM, 2 * MEM_WIDTH), lambda i, j: (layer, i, 0, 0))
    return pl.pallas_call(
        _mem_attend_kernel,
        grid=(b, t // tm),
        in_specs=[pl.BlockSpec((1, tm, MEM_WIDTH), lambda i, j: (i, j, 0)), kv_spec],
        out_specs=pl.BlockSpec((1, tm, MEM_WIDTH), lambda i, j: (i, j, 0)),
        out_shape=jax.ShapeDtypeStruct((b, t, MEM_WIDTH), F32),
        compiler_params=_cparams(2),
        name="mem_attend",
    )(mq, mkv)


def _ssm_disc_kernel(ar_ref, ai_ref, ldt_ref, br_ref, bi_ref, abr_ref, abi_ref, bbr_ref, bbi_ref):
    ar = ar_ref[...]
    ai = ai_ref[...]
    dt = jnp.exp(ldt_ref[...])
    mag = jnp.exp(dt * ar)
    abr = mag * jnp.cos(dt * ai)
    abi = mag * jnp.sin(dt * ai)
    den = ar * ar + ai * ai
    fr = ((abr - 1.0) * ar + abi * ai) / den
    fi = (abi * ar - (abr - 1.0) * ai) / den
    br = br_ref[...]
    bi = bi_ref[...]
    abr_ref[...] = abr
    abi_ref[...] = abi
    bbr_ref[...] = fr * br - fi * bi
    bbi_ref[...] = fr * bi + fi * br


def _ssm_discretise(a_re, a_im, log_dt, b_re, b_im):
    rep = lambda x: jnp.repeat(x, SSM_GROUP, axis=0)
    tb = lambda x: x.transpose(0, 2, 1).reshape(SSM_WIDTH, SSM_STATE)
    shp = jax.ShapeDtypeStruct((SSM_WIDTH, SSM_STATE), F32)
    abr, abi, bbr, bbi = pl.pallas_call(
        _ssm_disc_kernel, out_shape=[shp] * 4, name="ssm_disc",
    )(rep(a_re), rep(a_im), rep(jnp.broadcast_to(log_dt[:, None], a_re.shape)), tb(b_re), tb(b_im))
    eye = jnp.eye(SSM_GPB, dtype=F32)

    def in_map(bb):
        bb = bb.reshape(SSM_SLABS, SSM_GPB, SSM_GROUP, SSM_STATE)
        return jnp.einsum('jgcp,gG->jgcGp', bb, eye).reshape(SSM_SLABS, LANE, SSM_GPB * SSM_STATE).astype(BF16)

    flat = lambda x: x[::SSM_GROUP].reshape(1, SSM_FLAT)
    return flat(abr), flat(abi), in_map(bbr), in_map(bbi)


def _ssm_out_map(c):
    eye = jnp.eye(SSM_GPB, dtype=F32)
    c = c.reshape(SSM_SLABS, SSM_GPB, SSM_GROUP, SSM_STATE)
    return jnp.einsum('jgcp,gG->jgpGc', c, eye).reshape(SSM_SLABS, SSM_GPB * SSM_STATE, LANE).astype(BF16)


def _ssm_kernel(u_ref, h0r_ref, h0i_ref, abr_ref, abi_ref, bdr_ref, bdi_ref, cdr_ref, cdi_ref, d_ref,
                wg_ref, bg_ref, o_ref, hr_ref, hi_ref, sr_ref, si_ref, st_ref, *, n_scan):
    c_idx = pl.program_id(1)
    slab = SSM_GPB * SSM_STATE

    @pl.when(c_idx == 0)
    def _():
        st_ref[0:1, :] = h0r_ref[0]
        st_ref[1:2, :] = h0i_ref[0]

    u = u_ref[0]
    ub = u.astype(BF16)
    for j in range(SSM_SLABS):
        uj = ub[:, j * LANE:(j + 1) * LANE]
        sr_ref[:, j * slab:(j + 1) * slab] = _dot(uj, bdr_ref[j])
        si_ref[:, j * slab:(j + 1) * slab] = _dot(uj, bdi_ref[j])

    ar = abr_ref[...]
    ai = abi_ref[...]

    def step(t, carry):
        hr, hi = carry
        nr = ar * hr - ai * hi + sr_ref[pl.ds(t, 1), :]
        ni = ar * hi + ai * hr + si_ref[pl.ds(t, 1), :]
        sr_ref[pl.ds(t, 1), :] = nr
        si_ref[pl.ds(t, 1), :] = ni
        return nr, ni

    hr, hi = lax.fori_loop(0, n_scan, step, (st_ref[0:1, :], st_ref[1:2, :]))
    st_ref[0:1, :] = hr
    st_ref[1:2, :] = hi
    hr_ref[0] = hr
    hi_ref[0] = hi

    ys = []
    for j in range(SSM_SLABS):
        hrj = sr_ref[:, j * slab:(j + 1) * slab].astype(BF16)
        hij = si_ref[:, j * slab:(j + 1) * slab].astype(BF16)
        ys.append(_dot(hrj, cdr_ref[j]) - _dot(hij, cdi_ref[j]))
    y = jnp.concatenate(ys, axis=1) + d_ref[...] * u
    y = jax.nn.gelu(y)
    o_ref[0] = y * jax.nn.sigmoid(_dot(y.astype(BF16), wg_ref[...]) + bg_ref[...])


def _ssm(u, h0r, h0i, disc, cdr, cdi, d, w_glu, b_glu, t_valid=None):
    b, t, _ = u.shape
    tc = _row_tile(t, 256)
    n_scan = tc if t_valid is None else t_valid
    assert t_valid is None or t == tc
    abr, abi, bdr, bdi = disc
    full = lambda a: pl.BlockSpec(a.shape, lambda i, c: (0,) * a.ndim)
    state_spec = pl.BlockSpec((1, 1, SSM_FLAT), lambda i, c: (i, 0, 0))
    args = (abr, abi, bdr, bdi, cdr, cdi, d.reshape(1, SSM_WIDTH), w_glu.astype(BF16), b_glu.reshape(1, SSM_WIDTH))
    o, hr, hi = pl.pallas_call(
        functools.partial(_ssm_kernel, n_scan=n_scan),
        grid=(b, t // tc),
        in_specs=[pl.BlockSpec((1, tc, SSM_WIDTH), lambda i, c: (i, c, 0)), state_spec, state_spec]
                 + [full(a) for a in args],
        out_specs=[pl.BlockSpec((1, tc, SSM_WIDTH), lambda i, c: (i, c, 0)), state_spec, state_spec],
        out_shape=[jax.ShapeDtypeStruct((b, t, SSM_WIDTH), F32),
                   jax.ShapeDtypeStruct((b, 1, SSM_FLAT), F32),
                   jax.ShapeDtypeStruct((b, 1, SSM_FLAT), F32)],
        scratch_shapes=[pltpu.VMEM((tc, SSM_FLAT), F32), pltpu.VMEM((tc, SSM_FLAT), F32),
                        pltpu.VMEM((8, SSM_FLAT), F32)],
        compiler_params=_cparams(2),
        name="ssm",
    )(u, h0r.reshape(b, 1, SSM_FLAT), h0i.reshape(b, 1, SSM_FLAT), *args)
    shp = (b, N_SSM_GROUPS, SSM_STATE)
    return o, hr.reshape(shp), hi.reshape(shp)


def _merge_kernel(x_ref, oa_ref, ob_ref, oc_ref, gb_ref, w_ref, o_ref):
    acc = x_ref[...]
    for i, r in enumerate((oa_ref, ob_ref, oc_ref)):
        w = r.shape[1]
        off = sum((NSA_WIDTH, SSM_WIDTH, MEM_WIDTH)[:i])
        mixed = (r[...] * gb_ref[:, off:off + w]).astype(BF16)
        acc = acc + _dot(mixed, w_ref[off:off + w, :])
    o_ref[...] = acc


def _merge(x2, o_a, o_b, o_c, gb, w_o_bf16):
    n = x2.shape[0]
    tm = _row_tile(n, 512)
    row = lambda w: pl.BlockSpec((tm, w), lambda i: (i, 0))
    return pl.pallas_call(
        _merge_kernel,
        grid=(n // tm,),
        in_specs=[row(D_MODEL), row(NSA_WIDTH), row(SSM_WIDTH), row(MEM_WIDTH), row(MIX_WIDTH),
                  pl.BlockSpec((MIX_WIDTH, D_MODEL), lambda i: (0, 0))],
        out_specs=row(D_MODEL),
        out_shape=jax.ShapeDtypeStruct((n, D_MODEL), F32),
        compiler_params=_cparams(1),
        name="merge",
    )(x2, o_a, o_b, o_c, gb, w_o_bf16)


def _ffn_kernel(x_ref, g_ref, w1_ref, w2_ref, fg_ref, o_ref, h_ref, acc_ref, *, final_norm):
    k = pl.program_id(1)

    @pl.when(k == 0)
    def _():
        h_ref[...] = _rms(x_ref[...], g_ref[...]).astype(BF16)
        acc_ref[...] = jnp.zeros_like(acc_ref)

    a = jnp.square(jnp.maximum(_dot(h_ref[...], w1_ref[...]), 0.0))
    acc_ref[...] += _dot(a.astype(BF16), w2_ref[...])

    @pl.when(k == pl.num_programs(1) - 1)
    def _():
        y = x_ref[...] + acc_ref[...]
        if final_norm:
            y = _rms(y, fg_ref[...])
        o_ref[...] = y


def _ffn(x2, g, w1_bf16, w2_bf16, final_g, final_norm):
    n = x2.shape[0]
    tm = _row_tile(n, 1024)
    tf = 1024
    return pl.pallas_call(
        functools.partial(_ffn_kernel, final_norm=final_norm),
        grid=(n // tm, D_FF // tf),
        in_specs=[pl.BlockSpec((tm, D_MODEL), lambda i, k: (i, 0)),
                  pl.BlockSpec((1, D_MODEL), lambda i, k: (0, 0)),
                  pl.BlockSpec((D_MODEL, tf), lambda i, k: (0, k)),
                  pl.BlockSpec((tf, D_MODEL), lambda i, k: (k, 0)),
                  pl.BlockSpec((1, D_MODEL), lambda i, k: (0, 0))],
        out_specs=pl.BlockSpec((tm, D_MODEL), lambda i, k: (i, 0)),
        out_shape=jax.ShapeDtypeStruct((n, D_MODEL), F32),
        scratch_shapes=[pltpu.VMEM((tm, D_MODEL), BF16), pltpu.VMEM((tm, D_MODEL), F32)],
        compiler_params=_cparams(2),
        name="ffn",
    )(x2, g.reshape(1, D_MODEL), w1_bf16, w2_bf16, final_g.reshape(1, D_MODEL))


def _layer_params(l, norm1_g, w_in, cmp_pe, cmp_w1, cmp_w2, ssm_a_re, ssm_a_im, ssm_log_dt, ssm_b_re, ssm_b_im,
                  ssm_c_re, ssm_c_im, ssm_d, w_glu, b_glu, mem_norm_g, w_mem_kv, w_o, norm2_g, w_ff1, w_ff2):
    return dict(
        norm1_g=norm1_g[l], w_in=_pack_w_in(w_in[l]),
        pe=_flat_cmp_pe(cmp_pe[l]), w1=_expand_cmp_w1(cmp_w1[l]), w2=_expand_cmp_w2(cmp_w2[l]),
        disc=_ssm_discretise(ssm_a_re[l], ssm_a_im[l], ssm_log_dt[l], ssm_b_re[l], ssm_b_im[l]),
        cdr=_ssm_out_map(ssm_c_re[l]), cdi=_ssm_out_map(ssm_c_im[l]), d=ssm_d[l],
        w_glu=w_glu[l], b_glu=b_glu[l], mem_norm_g=mem_norm_g[l], w_mem_kv=w_mem_kv[l].astype(BF16),
        w_o=w_o[l].astype(BF16), norm2_g=norm2_g[l], w_ff1=w_ff1[l].astype(BF16), w_ff2=w_ff2[l].astype(BF16))


def _prompt_layer(x2, mem2, p, b, t, final_g, final_norm):
    q, kvc, kvs, kvw, u, mq, gb, gn = _in_proj(x2, p['norm1_g'], p['w_in'])
    lh = _cmp_lh(kvc.reshape(b * t // CMP_STRIDE, CMP_ROW), p['pe'], p['w1'])
    ckv = _cmp_tokens_prompt(lh.reshape(b, t // CMP_STRIDE, 4 * KV_WIDTH), p['w2'])
    o_a = _nsa_prompt(q, kvs, kvw, gn, ckv, b, t)
    zero = jnp.zeros((b, N_SSM_GROUPS, SSM_STATE), F32)
    o_b, hr, hi = _ssm(u.reshape(b, t, SSM_WIDTH), zero, zero, p['disc'], p['cdr'], p['cdi'], p['d'],
                       p['w_glu'], p['b_glu'])
    mkv = _norm_mm(mem2, p['mem_norm_g'], p['w_mem_kv'])
    o_c = _mem_attend(mq.reshape(b, t, MEM_WIDTH), mkv.reshape(b, N_MEM, 2 * MEM_WIDTH))
    x2 = _merge(x2, o_a, o_b.reshape(b * t, SSM_WIDTH), o_c.reshape(b * t, MEM_WIDTH), gb, p['w_o'])
    x2 = _ffn(x2, p['norm2_g'], p['w_ff1'], p['w_ff2'], final_g, final_norm)
    kv_shape = (b, t, 2, N_KV, HEAD_DIM)
    wl = min(WINDOW, t)
    return x2, dict(cmp=kvc.reshape(kv_shape), slc=kvs.reshape(kv_shape),
                    win=kvw.reshape(kv_shape)[:, t - wl:], hr=hr, hi=hi,
                    mkv=mkv.reshape(b, N_MEM, 2, MEM_HEADS, MEM_HEAD_DIM))


CHUNKS_PER_PAGE = PAGE_SIZE // CMP_STRIDE


def _cmp_lh_paged_kernel(pt_ref, cache_ref, pe_ref, w_ref, o_ref, buf_ref, stage_ref, xl_ref, sem, *, pages, layer):
    b = pl.program_id(0)
    s = pl.program_id(1)
    n_s = pl.num_programs(1)
    step = b * n_s + s
    slot = step % 2

    def page_copy(bb, ss, i, sl):
        page = pt_ref[bb, ss * pages + i]
        return pltpu.make_async_copy(cache_ref.at[layer, page], buf_ref.at[sl, i], sem.at[sl, i])

    def start_all(bb, ss, sl):
        for i in range(pages):
            page_copy(bb, ss, i, sl).start()

    @pl.when(step == 0)
    def _():
        start_all(b, s, slot)

    nxt = step + 1

    @pl.when(nxt < pl.num_programs(0) * n_s)
    def _():
        start_all(nxt // n_s, nxt % n_s, 1 - slot)

    for i in range(pages):
        page_copy(b, s, i, slot).wait()

    n_stage = stage_ref.shape[0]

    def relayout(it, carry):
        for u in range(n_stage):
            i = it * n_stage + u
            xt = buf_ref[slot, i].reshape(2 * KV_WIDTH, PAGE_SIZE).T
            base = pl.multiple_of(i * CHUNKS_PER_PAGE, CHUNKS_PER_PAGE)
            for h in range(2 * KV_WIDTH // LANE):
                stage_ref[u, h] = xt[:, h * LANE:(h + 1) * LANE]
                for l in range(CMP_STRIDE):
                    xl_ref[l, pl.ds(base, CHUNKS_PER_PAGE), h * LANE:(h + 1) * LANE] = (
                        stage_ref[u, h, pl.ds(l, CHUNKS_PER_PAGE, stride=CMP_STRIDE), :])
        return carry

    lax.fori_loop(0, pages // n_stage, relayout, 0)
    lo = jnp.zeros((pages * CHUNKS_PER_PAGE, 2 * KV_WIDTH), F32)
    hi = jnp.zeros((pages * CHUNKS_PER_PAGE, 2 * KV_WIDTH), F32)
    for l in range(CMP_STRIDE):
        x = xl_ref[l]
        lo = lo + _dot((x + pe_ref[0, l]).astype(BF16), w_ref[0, l])
        hi = hi + _dot((x + pe_ref[1, l]).astype(BF16), w_ref[1, l])
    o_ref[0] = jnp.concatenate([lo, hi], axis=1)


def _cmp_lh_paged(cache_t, layer, page_table, pe_flat, w1_big):
    b, n_pages = page_table.shape
    pages = math.gcd(n_pages, 32)
    rows = pages * CHUNKS_PER_PAGE
    page_shape = cache_t.shape[2:]
    grid_spec = pltpu.PrefetchScalarGridSpec(
        num_scalar_prefetch=1,
        grid=(b, n_pages // pages),
        in_specs=[pl.BlockSpec(memory_space=pl.ANY),
                  pl.BlockSpec((2, CMP_STRIDE, 1, 2 * KV_WIDTH), lambda i, s, pt: (0, 0, 0, 0)),
                  pl.BlockSpec((2, CMP_STRIDE, 2 * KV_WIDTH, 2 * KV_WIDTH), lambda i, s, pt: (0, 0, 0, 0))],
        out_specs=pl.BlockSpec((1, rows, 4 * KV_WIDTH), lambda i, s, pt: (i, s, 0)),
        scratch_shapes=[pltpu.VMEM((2, pages) + page_shape, F32),
                        pltpu.VMEM((math.gcd(pages, 8), 2 * KV_WIDTH // LANE, PAGE_SIZE, LANE), F32),
                        pltpu.VMEM((CMP_STRIDE, rows, 2 * KV_WIDTH), F32),
                        pltpu.SemaphoreType.DMA((2, pages))])
    return pl.pallas_call(
        functools.partial(_cmp_lh_paged_kernel, pages=pages, layer=layer),
        grid_spec=grid_spec,
        out_shape=jax.ShapeDtypeStruct((b, n_pages * CHUNKS_PER_PAGE, 4 * KV_WIDTH), F32),
        compiler_params=_cparams(2),
        name="cmp_lh_paged",
    )(page_table, cache_t, pe_flat.reshape(2, CMP_STRIDE, 1, 2 * KV_WIDTH),
      w1_big.reshape(2, CMP_STRIDE, 2 * KV_WIDTH, 2 * KV_WIDTH))


TOK_PAD = 8


def _nsa_sample_select_kernel(lhc_ref, lht_ref, w2_ref, q_ref, pool_ref, oc_ref, idx_ref, lh_ref, *, past, n_tok):
    n_cache = lhc_ref.shape[1]
    n_tail = lht_ref.shape[1]
    r = lh_ref.shape[0]
    lh_ref[0:n_cache, :] = lhc_ref[0]
    lh_ref[n_cache:n_cache + n_tail, :] = lht_ref[0]
    lh_ref[n_cache + n_tail:r, :] = jnp.zeros((r - n_cache - n_tail, lh_ref.shape[1]), F32)
    ckv = _cmp_combine(lh_ref[...], w2_ref[...])

    rows = HPG * TOK_PAD
    row = lax.broadcasted_iota(jnp.int32, (rows, 1), 0)
    pos_q = past + (row & (TOK_PAD - 1))
    ci = lax.broadcasted_iota(jnp.int32, (1, r), 1)
    d_c = pos_q - (ci * CMP_STRIDE + (CMP_LEN - 1))
    n_blk = pool_ref.shape[0]
    ps_all = []
    for g in range(N_KV):
        qg = q_ref[0, g]
        q = qg[:, :HEAD_DIM]
        slope = qg[:, HEAD_DIM:HEAD_DIM + 1].astype(F32)
        ck = ckv[:, g * 2 * HEAD_DIM:g * 2 * HEAD_DIM + HEAD_DIM].astype(BF16)
        cv = ckv[:, g * 2 * HEAD_DIM + HEAD_DIM:(g + 1) * 2 * HEAD_DIM].astype(BF16)
        s = _dot_nt(q, ck) - slope * d_c.astype(F32)
        p_c = _masked_probs(s, d_c >= 0)
        oc_ref[0, g] = _dot(p_c.astype(BF16), cv)
        ps = p_c[0:TOK_PAD]
        for h in range(1, HPG):
            ps = ps + p_c[h * TOK_PAD:(h + 1) * TOK_PAD]
        ps_all.append(ps)
    ps = jnp.concatenate(ps_all + [jnp.zeros((LANE - N_KV * TOK_PAD, r), F32)], axis=0)
    imp_t = _pool_blocks_t(ps, pool_ref[...])
    blk = lax.broadcasted_iota(jnp.int32, (n_blk, 1), 0)
    cur = (past + (lax.broadcasted_iota(jnp.int32, (1, LANE), 1) & (TOK_PAD - 1))) // SLC_BLOCK
    valid = blk <= cur
    forced = (blk == 0) | (blk == cur) | (blk == cur - 1)
    score = jnp.where(valid, jnp.where(forced, FORCED_SCORE, imp_t), -1.0)
    _, picks = _topk_select_cols(score, blk.astype(F32), SLC_TOPK)
    for k, pk in enumerate(picks):
        idx_ref[0, k:k + 1, :] = pk.astype(jnp.int32)


def _nsa_sample_select(lh_cache, lh_tail, w2_big, qsel, past, n_tok):
    b, n_cache, _ = lh_cache.shape
    n_tail = lh_tail.shape[1]
    r = -(-(n_cache + n_tail) // LANE) * LANE
    n_b = past // SLC_BLOCK + 1
    assert n_b >= SLC_TOPK
    n_blk = -(-n_b // 8) * 8
    pool = (jnp.arange(n_blk)[:, None] == jnp.arange(r)[None, :] // (SLC_BLOCK // CMP_STRIDE)).astype(BF16)
    rows = HPG * TOK_PAD
    o_c, idx = pl.pallas_call(
        functools.partial(_nsa_sample_select_kernel, past=past, n_tok=n_tok),
        grid=(b,),
        in_specs=[pl.BlockSpec((1, n_cache, 4 * KV_WIDTH), lambda i: (i, 0, 0)),
                  pl.BlockSpec((1, n_tail, 4 * KV_WIDTH), lambda i: (i, 0, 0)),
                  pl.BlockSpec((2 * KV_WIDTH, 2 * KV_WIDTH), lambda i: (0, 0)),
                  pl.BlockSpec((1, N_KV, rows, LANE), lambda i: (i, 0, 0, 0)),
                  pl.BlockSpec((n_blk, r), lambda i: (0, 0))],
        out_specs=[pl.BlockSpec((1, N_KV, rows, HEAD_DIM), lambda i: (i, 0, 0, 0)),
                   pl.BlockSpec((1, SLC_TOPK, LANE), lambda i: (i, 0, 0))],
        out_shape=[jax.ShapeDtypeStruct((b, N_KV, rows, HEAD_DIM), F32),
                   jax.ShapeDtypeStruct((b, SLC_TOPK, LANE), jnp.int32)],
        scratch_shapes=[pltpu.VMEM((r, 4 * KV_WIDTH), F32)],
        compiler_params=_cparams(1),
        name="nsa_sample_select",
    )(lh_cache, lh_tail, w2_big, qsel, pool)
    idx = idx[:, :, :N_KV * TOK_PAD].reshape(b, SLC_TOPK, N_KV, TOK_PAD)[:, :, :, :n_tok]
    return o_c, idx.transpose(0, 2, 3, 1).reshape(-1)


def _rows_minor(cache):
    n = cache.ndim
    return cache.transpose(tuple(range(n - 4)) + (n - 3, n - 2, n - 1, n - 4))


def _nsa_sample_attend_kernel(idx_ref, pt_ref, cache_ref, q_ref, oc_ref, gate_ref, snew_ref, win_ref, wnew_ref,
                              o_ref, kbuf_ref, vbuf_ref, sem, *, layer, past, n_tok):
    b = pl.program_id(0)
    n_pages = past // PAGE_SIZE
    cur_blk = past // SLC_BLOCK
    per_page = PAGE_SIZE // SLC_BLOCK
    n_sel = SLC_TOPK * PAGE_SIZE

    def sel_block(g, t, k):
        return idx_ref[((b * N_KV + g) * n_tok + t) * SLC_TOPK + k]

    def page_copies(g, t, k):
        c = jnp.minimum(sel_block(g, t, k), cur_blk - 1)
        page = pt_ref[b * n_pages + c // per_page]
        slot = g * n_tok + t
        dst = pl.ds(k * PAGE_SIZE, PAGE_SIZE)
        return (pltpu.make_async_copy(cache_ref.at[layer, page, 0, g], kbuf_ref.at[slot, :, dst], sem.at[0, slot]),
                pltpu.make_async_copy(cache_ref.at[layer, page, 1, g], vbuf_ref.at[slot, :, dst], sem.at[1, slot]))

    for g in range(N_KV):
        for t in range(n_tok):
            for k in range(SLC_TOPK):
                for cp in page_copies(g, t, k):
                    cp.start()

    rows = HPG * TOK_PAD
    row = lax.broadcasted_iota(jnp.int32, (rows, 1), 0)
    tok = row & (TOK_PAD - 1)
    pos_q = past + tok
    lane = lax.broadcasted_iota(jnp.int32, (1, n_sel), 1)
    lane_slot = lane // PAGE_SIZE
    lane_blk = (lane & (PAGE_SIZE - 1)) // SLC_BLOCK
    r_new = lax.broadcasted_iota(jnp.int32, (1, TOK_PAD), 1)
    d_new = pos_q - (past + r_new)
    mask_new = (d_new >= 0) & (r_new < n_tok)
    wb = win_ref.shape[-1]
    d_win = pos_q - (past - wb + lax.broadcasted_iota(jnp.int32, (1, wb), 1))
    mask_win = (d_win >= 0) & (d_win <= WINDOW)

    def joint_attend(q, slope, kt_old, vt_old, d_old, mask_old, k_new, v_new):
        s_o = jnp.where(mask_old, _dot(q, kt_old) - slope * d_old.astype(F32), NEG_INF)
        s_n = jnp.where(mask_new, _dot_nt(q, k_new) - slope * d_new.astype(F32), NEG_INF)
        m = jnp.maximum(jnp.max(s_o, axis=-1, keepdims=True), jnp.max(s_n, axis=-1, keepdims=True))
        e_o = jnp.where(mask_old, jnp.exp(s_o - m), 0.0)
        e_n = jnp.where(mask_new, jnp.exp(s_n - m), 0.0)
        l = jnp.sum(e_o, axis=-1, keepdims=True) + jnp.sum(e_n, axis=-1, keepdims=True)
        return (_dot_nt(e_o.astype(BF16), vt_old) + _dot(e_n.astype(BF16), v_new)) / jnp.maximum(l, 1e-30)

    for g in range(N_KV):
        ksl = slice(g * HEAD_DIM, (g + 1) * HEAD_DIM)
        vsl = slice(KV_WIDTH + g * HEAD_DIM, KV_WIDTH + (g + 1) * HEAD_DIM)
        qg = q_ref[0, g]
        q = qg[:, :HEAD_DIM]
        slope = qg[:, HEAD_DIM:HEAD_DIM + 1].astype(F32)
        k_snew = snew_ref[0][:, ksl].astype(BF16)
        v_snew = snew_ref[0][:, vsl].astype(BF16)
        o_s = jnp.zeros((rows, HEAD_DIM), F32)
        for t in range(n_tok):
            for k in range(SLC_TOPK):
                for cp in page_copies(g, t, k):
                    cp.wait()
            blk_of_lane = jnp.zeros((1, n_sel), jnp.int32)
            for k in range(SLC_TOPK):
                blk_of_lane = jnp.where(lane_slot == k, sel_block(g, t, k), blk_of_lane)
            d_sel = pos_q - (blk_of_lane * SLC_BLOCK + (lane & (SLC_BLOCK - 1)))
            mask_sel = (d_sel >= 0) & (blk_of_lane < cur_blk) & (blk_of_lane % per_page == lane_blk)
            o_t = joint_attend(q, slope, kbuf_ref[g * n_tok + t].astype(BF16), vbuf_ref[g * n_tok + t].astype(BF16),
                               d_sel, mask_sel, k_snew, v_snew)
            o_s = jnp.where(tok == t, o_t, o_s)
        o_w = joint_attend(q, slope, win_ref[0, 0, 0, g].astype(BF16), win_ref[0, 0, 1, g].astype(BF16), d_win,
                           mask_win, wnew_ref[0][:, ksl].astype(BF16), wnew_ref[0][:, vsl].astype(BF16))
        gate = gate_ref[0, g]
        o_ref[0, g] = gate[:, 0:1] * oc_ref[0, g] + gate[:, 1:2] * o_s + gate[:, 2:3] * o_w


def _nsa_sample_attend(idx, page_table, cache_slc_t, cache_win_t, layer, qsel, o_c, gates, slc_new, win_new, past, n_tok):
    b = qsel.shape[0]
    assert past % PAGE_SIZE == 0 and n_tok <= TOK_PAD
    rows = HPG * TOK_PAD
    wb = cache_win_t.shape[-1]
    per_b = lambda shape: pl.BlockSpec((1,) + shape, lambda i, *_: (i,) + (0,) * len(shape))
    grid_spec = pltpu.PrefetchScalarGridSpec(
        num_scalar_prefetch=2,
        grid=(b,),
        in_specs=[pl.BlockSpec(memory_space=pl.ANY),
                  per_b((N_KV, rows, LANE)), per_b((N_KV, rows, HEAD_DIM)), per_b((N_KV, rows, 16)),
                  per_b((TOK_PAD, 2 * KV_WIDTH)),
                  pl.BlockSpec((1, 1, 2, N_KV, HEAD_DIM, wb), lambda i, *_: (layer, i, 0, 0, 0, 0)),
                  per_b((TOK_PAD, 2 * KV_WIDTH))],
        out_specs=per_b((N_KV, rows, HEAD_DIM)),
        scratch_shapes=[pltpu.VMEM((N_KV * n_tok, HEAD_DIM, SLC_TOPK * PAGE_SIZE), F32),
                        pltpu.VMEM((N_KV * n_tok, HEAD_DIM, SLC_TOPK * PAGE_SIZE), F32),
                        pltpu.SemaphoreType.DMA((2, N_KV * n_tok))])
    return pl.pallas_call(
        functools.partial(_nsa_sample_attend_kernel, layer=layer, past=past, n_tok=n_tok),
        grid_spec=grid_spec,
        out_shape=jax.ShapeDtypeStruct((b, N_KV, rows, HEAD_DIM), F32),
        compiler_params=_cparams(1),
        name="nsa_sample_attend",
    )(idx, page_table.reshape(-1), cache_slc_t, qsel, o_c, gates, slc_new, cache_win_t, win_new)


def _pad_tokens(x, axis):
    pad = [(0, 0)] * x.ndim
    pad[axis] = (0, TOK_PAD - x.shape[axis])
    return jnp.pad(x, pad)


def _sample_layer(x2, p, b, s, layer, cache_cmp_t, cache_slc_t, cache_win_t, cache_win, h0r, h0i, cache_mem,
                  page_table, final_g, final_norm):
    past = page_table.shape[1] * PAGE_SIZE
    assert past % SLC_BLOCK == 0 and s <= CMP_STRIDE
    q, kvc, kvs, kvw, u, mq, gb, gn = _in_proj(x2, p['norm1_g'], p['w_in'])
    lh_cache = _cmp_lh_paged(cache_cmp_t, layer, page_table, p['pe'], p['w1'])
    tail = jnp.pad(kvc.reshape(b, s, 2 * KV_WIDTH), ((0, 0), (0, CHUNKS_PER_PAGE * CMP_STRIDE - s), (0, 0)))
    lh_tail = _cmp_lh(tail.reshape(b * CHUNKS_PER_PAGE, CMP_ROW), p['pe'], p['w1'])
    pos = past + jnp.arange(s, dtype=jnp.int32)
    qsel = _pad_tokens(_query_pack(q.reshape(b, s, NSA_WIDTH), pos), 2).reshape(b, N_KV, HPG * TOK_PAD, LANE)
    o_c, idx = _nsa_sample_select(lh_cache, lh_tail.reshape(b, CHUNKS_PER_PAGE, 4 * KV_WIDTH), p['w2'], qsel, past, s)
    gates = gn[:, :3 * N_HEADS].reshape(b, s, 3, N_KV, HPG).transpose(0, 3, 4, 1, 2)
    gates = jnp.pad(_pad_tokens(gates, 3), ((0, 0),) * 4 + ((0, 16 - 3),)).reshape(b, N_KV, HPG * TOK_PAD, 16)
    slc_new = kvs.reshape(b, s, 2 * KV_WIDTH)
    win_new = kvw.reshape(b, s, 2 * KV_WIDTH)
    o = _nsa_sample_attend(idx, page_table, cache_slc_t, cache_win_t, layer, qsel, o_c, gates,
                           _pad_tokens(slc_new, 1), _pad_tokens(win_new, 1), past, s)
    o_a = o.reshape(b, N_KV, HPG, TOK_PAD, HEAD_DIM)[:, :, :, :s].transpose(0, 3, 1, 2, 4).reshape(b * s, NSA_WIDTH)
    o_b, hr, hi = _ssm(_pad_tokens(u.reshape(b, s, SSM_WIDTH), 1), h0r, h0i, p['disc'], p['cdr'], p['cdi'], p['d'],
                       p['w_glu'], p['b_glu'], t_valid=s)
    o_c2 = _mem_attend(_pad_tokens(mq.reshape(b, s, MEM_WIDTH), 1), cache_mem, layer)
    x2 = _merge(x2, o_a, o_b[:, :s].reshape(b * s, SSM_WIDTH), o_c2[:, :s].reshape(b * s, MEM_WIDTH), gb, p['w_o'])
    x2 = _ffn(x2, p['norm2_g'], p['w_ff1'], p['w_ff2'], final_g, final_norm)
    kv_shape = (b, s, 2, N_KV, HEAD_DIM)
    new_win = jnp.concatenate([cache_win, kvw.reshape(kv_shape).astype(cache_win.dtype)], axis=1)[:, s:]
    return x2, dict(cmp=kvc.reshape(kv_shape), slc=kvs.reshape(kv_shape), win=new_win, hr=hr, hi=hi)


def kernel(x_prompt, x_sample, cache_cmp_kv, cache_slc_kv, cache_win_kv, state_ssm_re, state_ssm_im, cache_mem_kv,
           page_table, mem_prompt, norm1_g, w_in, cmp_pe, cmp_w1, cmp_w2, ssm_a_re, ssm_a_im, ssm_log_dt, ssm_b_re,
           ssm_b_im, ssm_c_re, ssm_c_im, ssm_d, w_glu, b_glu, mem_norm_g, w_mem_kv, w_o, norm2_g, w_ff1, w_ff2,
           final_norm_g):
    bp, t, _ = x_prompt.shape
    bs, s, _ = x_sample.shape
    depth = w_in.shape[0]
    xp = x_prompt.reshape(bp * t, D_MODEL)
    xs = x_sample.reshape(bs * s, D_MODEL)
    mem2 = mem_prompt.reshape(bp * N_MEM, D_MODEL)
    cache_cmp_t = _rows_minor(cache_cmp_kv)
    cache_slc_t = _rows_minor(cache_slc_kv)
    cache_win_t = _rows_minor(cache_win_kv)
    cache_mem = cache_mem_kv.reshape(depth, bs, N_MEM, 2 * MEM_WIDTH)
    aux_p, aux_s = [], []
    for l in range(depth):
        p = _layer_params(l, norm1_g, w_in, cmp_pe, cmp_w1, cmp_w2, ssm_a_re, ssm_a_im, ssm_log_dt, ssm_b_re,
                          ssm_b_im, ssm_c_re, ssm_c_im, ssm_d, w_glu, b_glu, mem_norm_g, w_mem_kv, w_o, norm2_g,
                          w_ff1, w_ff2)
        last = l == depth - 1
        xp, ap = _prompt_layer(xp, mem2, p, bp, t, final_norm_g, last)
        xs, a_s = _sample_layer(xs, p, bs, s, l, cache_cmp_t, cache_slc_t, cache_win_t, cache_win_kv[l],
                                state_ssm_re[l], state_ssm_im[l], cache_mem, page_table, final_norm_g, last)
        aux_p.append(ap)
        aux_s.append(a_s)
    stack = lambda aux, key: jnp.stack([a[key] for a in aux])
    return (xp.reshape(bp, t, D_MODEL), xs.reshape(bs, s, D_MODEL),
            stack(aux_p, 'cmp'), stack(aux_s, 'cmp'), stack(aux_p, 'slc'), stack(aux_s, 'slc'),
            stack(aux_p, 'win'), stack(aux_s, 'win'), stack(aux_p, 'hr'), stack(aux_p, 'hi'),
            stack(aux_s, 'hr'), stack(aux_s, 'hi'), stack(aux_p, 'mkv'))
```

```python
import functools
import math

import jax
import jax.numpy as jnp
from jax import lax
from jax.experimental import pallas as pl
from jax.experimental.pallas import tpu as pltpu

F32 = jnp.float32
BF16 = jnp.bfloat16

D_MODEL = 1024
PAGE_SIZE = 128
N_HEADS = 8
HEAD_DIM = 64
N_KV = 2
HPG = N_HEADS // N_KV
CMP_STRIDE = 16
CMP_LEN = 2 * CMP_STRIDE
SLC_BLOCK = 64
SLC_TOPK = 16
WINDOW = 512
Q_BLOCK = 128
SSM_WIDTH = D_MODEL // 2
SSM_GROUP = 16
N_SSM_GROUPS = SSM_WIDTH // SSM_GROUP
SSM_STATE = 64
SSM_FLAT = N_SSM_GROUPS * SSM_STATE
N_MEM = 256
MEM_HEADS = 4
MEM_HEAD_DIM = 128
MEM_WIDTH = MEM_HEADS * MEM_HEAD_DIM
D_FF = 4 * D_MODEL
NSA_WIDTH = N_HEADS * HEAD_DIM
KV_WIDTH = N_KV * HEAD_DIM
MIX_WIDTH = NSA_WIDTH + SSM_WIDTH + MEM_WIDTH
OFF_KV = NSA_WIDTH
OFF_GN = OFF_KV + 6 * KV_WIDTH
OFF_U = OFF_GN + 3 * N_HEADS
OFF_MQ = OFF_U + SSM_WIDTH
OFF_GB = OFF_MQ + MEM_WIDTH
IN_WIDTH = OFF_GB + MIX_WIDTH
EPS = 1e-6
NEG_INF = -1e30
FORCED_SCORE = 1e9

LANE = 128
CMP_ROW = CMP_STRIDE * 2 * KV_WIDTH
SSM_GPB = LANE // SSM_GROUP
SSM_SLABS = SSM_WIDTH // LANE
VMEM_LIMIT = 56 * 2**20


def _cparams(n_grid, vmem=VMEM_LIMIT):
    return pltpu.CompilerParams(dimension_semantics=("arbitrary",) * n_grid, vmem_limit_bytes=vmem)


def _rms(x, g):
    return x * lax.rsqrt(jnp.mean(x * x, axis=-1, keepdims=True) + EPS) * g


def _dot(a, b):
    return jnp.dot(a, b, preferred_element_type=F32)


def _dot_nt(a, b):
    return lax.dot_general(a, b, (((1,), (1,)), ((), ())), preferred_element_type=F32)


def _masked_probs(s, mask):
    s = jnp.where(mask, s, NEG_INF)
    m = jnp.max(s, axis=-1, keepdims=True)
    e = jnp.where(mask, jnp.exp(s - m), 0.0)
    return e / jnp.maximum(jnp.sum(e, axis=-1, keepdims=True), 1e-30)


def _row_tile(n, pref):
    t = min(n, pref)
    assert n % t == 0
    return t


_SEGS = ((0, 512), (512, 768), (768, 1024), (1024, 1280), (1280, 1792), (1792, 2304), (2304, 3840), (3840, 3968))
_SIGMOID_FROM = 2304
_PACKED_WIDTH = 3968


def _pack_w_in(w):
    gn = jnp.pad(w[:, OFF_GN:OFF_U], ((0, 0), (0, LANE - 3 * N_HEADS)))
    return jnp.concatenate([w[:, :OFF_GN], w[:, OFF_U:], gn], axis=1).astype(BF16)


def _inproj_kernel(x_ref, g_ref, w_ref, *o_refs):
    h = _rms(x_ref[...], g_ref[...]).astype(BF16)
    for (a, b), o_ref in zip(_SEGS, o_refs):
        z = _dot(h, w_ref[:, a:b])
        if a >= _SIGMOID_FROM:
            z = jax.nn.sigmoid(z)
        o_ref[...] = z


def _in_proj(x2, g, w_packed):
    n = x2.shape[0]
    tm = _row_tile(n, 512)
    return pl.pallas_call(
        _inproj_kernel,
        grid=(n // tm,),
        in_specs=[pl.BlockSpec((tm, D_MODEL), lambda i: (i, 0)),
                  pl.BlockSpec((1, D_MODEL), lambda i: (0, 0)),
                  pl.BlockSpec((D_MODEL, _PACKED_WIDTH), lambda i: (0, 0))],
        out_specs=[pl.BlockSpec((tm, b - a), lambda i: (i, 0)) for a, b in _SEGS],
        out_shape=[jax.ShapeDtypeStruct((n, b - a), F32) for a, b in _SEGS],
        compiler_params=_cparams(1),
        name="in_proj",
    )(x2, g.reshape(1, D_MODEL), w_packed)


def _norm_mm_kernel(x_ref, g_ref, w_ref, o_ref):
    o_ref[...] = _dot(_rms(x_ref[...], g_ref[...]).astype(BF16), w_ref[...])


def _norm_mm(x2, g, w_bf16):
    n, k = x2.shape
    m = w_bf16.shape[1]
    tm = _row_tile(n, 256)
    return pl.pallas_call(
        _norm_mm_kernel,
        grid=(n // tm,),
        in_specs=[pl.BlockSpec((tm, k), lambda i: (i, 0)),
                  pl.BlockSpec((1, k), lambda i: (0, 0)),
                  pl.BlockSpec((k, m), lambda i: (0, 0))],
        out_specs=pl.BlockSpec((tm, m), lambda i: (i, 0)),
        out_shape=jax.ShapeDtypeStruct((n, m), F32),
        compiler_params=_cparams(1),
        name="norm_mm",
    )(x2, g.reshape(1, k), w_bf16)


def _expand_cmp_w1(w1):
    w = w1.reshape(2, 2, CMP_STRIDE, HEAD_DIM, HEAD_DIM).astype(BF16)
    blocks = []
    for kv in range(2):
        for g in range(N_KV):
            off = (kv * N_KV + g) * HEAD_DIM
            blocks.append(jnp.pad(w[kv], ((0, 0), (0, 0), (0, 0), (off, 2 * KV_WIDTH - HEAD_DIM - off))))
    return jnp.concatenate(blocks, axis=2)


def _flat_cmp_pe(pe):
    p = pe.reshape(2, 2, CMP_STRIDE, HEAD_DIM)
    p = jnp.broadcast_to(p.transpose(1, 2, 0, 3)[:, :, :, None, :], (2, CMP_STRIDE, 2, N_KV, HEAD_DIM))
    return p.reshape(2, 1, CMP_ROW)


def _expand_cmp_w2(w2):
    eye2 = jnp.eye(2, dtype=w2.dtype)
    big = jnp.einsum('kde,kK,gG->kgdGKe', w2, eye2, eye2)
    return big.reshape(2 * KV_WIDTH, 2 * KV_WIDTH).astype(BF16)


def _cmp_lohi(x, pe_ref, w_ref):
    lo = _dot((x + pe_ref[0]).astype(BF16), w_ref[0])
    hi = _dot((x + pe_ref[1]).astype(BF16), w_ref[1])
    return jnp.concatenate([lo, hi], axis=1)


def _cmp_lh_kernel(x_ref, pe_ref, w_ref, o_ref):
    o_ref[...] = _cmp_lohi(x_ref[...], pe_ref, w_ref)


def _cmp_lh(x2, pe_flat, w1_big):
    n = x2.shape[0]
    tm = _row_tile(n, 256)
    return pl.pallas_call(
        _cmp_lh_kernel,
        grid=(n // tm,),
        in_specs=[pl.BlockSpec((tm, CMP_ROW), lambda i: (i, 0)),
                  pl.BlockSpec((2, 1, CMP_ROW), lambda i: (0, 0, 0)),
                  pl.BlockSpec((2, CMP_ROW, 2 * KV_WIDTH), lambda i: (0, 0, 0))],
        out_specs=pl.BlockSpec((tm, 4 * KV_WIDTH), lambda i: (i, 0)),
        out_shape=jax.ShapeDtypeStruct((n, 4 * KV_WIDTH), F32),
        compiler_params=_cparams(1),
        name="cmp_lh",
    )(x2, pe_flat, w1_big.reshape(2, CMP_ROW, 2 * KV_WIDTH))


def _cmp_combine(lh, w2_big):
    r = lh.shape[0]
    hi_next = pltpu.roll(lh[:, 2 * KV_WIDTH:], r - 1, 0)
    return _dot(jax.nn.gelu(lh[:, :2 * KV_WIDTH] + hi_next).astype(BF16), w2_big)


def _cmp_combine_kernel(lh_ref, w2_ref, o_ref):
    o_ref[0] = _cmp_combine(lh_ref[0], w2_ref[...])


def _cmp_tokens_prompt(lh, w2_big):
    b, r, _ = lh.shape
    return pl.pallas_call(
        _cmp_combine_kernel,
        grid=(b,),
        in_specs=[pl.BlockSpec((1, r, 4 * KV_WIDTH), lambda i: (i, 0, 0)),
                  pl.BlockSpec((2 * KV_WIDTH, 2 * KV_WIDTH), lambda i: (0, 0))],
        out_specs=pl.BlockSpec((1, r, 2 * KV_WIDTH), lambda i: (i, 0, 0)),
        out_shape=jax.ShapeDtypeStruct((b, r, 2 * KV_WIDTH), F32),
        compiler_params=_cparams(1),
        name="cmp_combine",
    )(lh, w2_big)


def _alibi_slopes():
    return jnp.exp2(-8.0 * (jnp.arange(N_HEADS, dtype=F32) + 1.0) / N_HEADS)


def _query_pack(q, pos):
    b, t, _ = q.shape
    qh = q.reshape(b, t, N_HEADS, HEAD_DIM).transpose(0, 2, 1, 3) * (HEAD_DIM ** -0.5)
    slope = _alibi_slopes()[:, None]
    pos_hi = ((pos // SLC_BLOCK) * SLC_BLOCK).astype(F32)[None, :]
    pos_lo = (pos % SLC_BLOCK).astype(F32)[None, :]
    ones = jnp.ones((1, t), F32)
    c = jnp.stack([slope * ones, slope * ones, -slope * pos_hi, -slope * pos_lo], axis=-1)
    c = jnp.pad(c, ((0, 0), (0, 0), (0, HEAD_DIM - 4)))
    c = jnp.broadcast_to(c[None], (b, N_HEADS, t, HEAD_DIM))
    return jnp.concatenate([qh, c], axis=-1).astype(BF16)


def _key_consts(kpos):
    hi = ((kpos // SLC_BLOCK) * SLC_BLOCK).astype(F32)
    lo = (kpos % SLC_BLOCK).astype(F32)
    one = jnp.ones_like(hi)
    return jnp.pad(jnp.stack([hi, lo, one, one], axis=-1), ((0, 0), (0, HEAD_DIM - 4)))


def _split3(x):
    hi = x.astype(BF16)
    r1 = x - hi.astype(F32)
    mid = r1.astype(BF16)
    lo = (r1 - mid.astype(F32)).astype(BF16)
    return hi, mid, lo


def _pool_blocks_t(ps, pool_t):
    hi, mid, lo = _split3(ps)
    return _dot_nt(pool_t, hi) + _dot_nt(pool_t, mid) + _dot_nt(pool_t, lo)


def _topk_select_cols(score, blk_f, k):
    sel = jnp.zeros(score.shape, F32)
    picks = []
    for _ in range(k):
        m = jnp.max(score, axis=0, keepdims=True)
        idx = jnp.min(jnp.where(score == m, blk_f, 1e9), axis=0, keepdims=True)
        pick = blk_f == idx
        sel = jnp.where(pick, 1.0, sel)
        score = jnp.where(pick, -3e38, score)
        picks.append(idx)
    return sel, picks


def _add_token_bias(s, bias):
    rows, n = s.shape
    q = bias.shape[0]
    return (s.reshape(rows // q, q, n) + bias[None]).reshape(rows, n)


def _biased_probs(s, bias, row_any=None):
    x = _add_token_bias(s, bias)
    e = jnp.exp(x - jnp.max(x, axis=-1, keepdims=True))
    inv = 1.0 / jnp.maximum(jnp.sum(e, axis=-1, keepdims=True), 1e-30)
    if row_any is not None:
        inv = jnp.where(row_any, inv, 0.0)
    return e * inv


def _nsa_prompt_kernel(qp_ref, gn_ref, ckp_ref, kp_ref, vw_ref, pool_ref, o_ref, *, tk):
    j = pl.program_id(2)
    start = j * Q_BLOCK
    rows = HPG * Q_BLOCK
    qp = qp_ref[0].reshape(rows, LANE)
    pos_t = start + lax.broadcasted_iota(jnp.int32, (Q_BLOCK, 1), 0)
    row = lax.broadcasted_iota(jnp.int32, (rows, 1), 0)

    ckp = ckp_ref[0, 0]
    n_c = ckp.shape[0]
    c_end = lax.broadcasted_iota(jnp.int32, (1, n_c), 1) * CMP_STRIDE + (CMP_LEN - 1)
    bias_c = jnp.where(c_end <= pos_t, 0.0, NEG_INF)
    any_c = start + (row & (Q_BLOCK - 1)) >= CMP_LEN - 1
    p_c = _biased_probs(_dot_nt(qp, ckp[:, :LANE]), bias_c, any_c)
    o_c = _dot(p_c.astype(BF16), ckp[:, LANE:LANE + HEAD_DIM])
    ps = p_c[0:Q_BLOCK]
    for h in range(1, HPG):
        ps = ps + p_c[h * Q_BLOCK:(h + 1) * Q_BLOCK]

    wlen = WINDOW + Q_BLOCK
    woff = pl.multiple_of(jnp.maximum(start - WINDOW, 0), Q_BLOCK)
    d_w = pos_t - (woff + lax.broadcasted_iota(jnp.int32, (1, wlen), 1))
    bias_w = jnp.where((d_w >= 0) & (d_w <= WINDOW), 0.0, NEG_INF)
    p_w = _biased_probs(_dot_nt(qp, vw_ref[0, 0, pl.ds(woff, wlen), LANE:2 * LANE]), bias_w)
    o_w = _dot(p_w.astype(BF16), vw_ref[0, 0, pl.ds(woff, wlen), HEAD_DIM:2 * HEAD_DIM])

    imp_t = _pool_blocks_t(ps, pool_ref[...])
    blk = lax.broadcasted_iota(jnp.int32, (LANE, 1), 0)
    cur = (start + lax.broadcasted_iota(jnp.int32, (1, Q_BLOCK), 1)) // SLC_BLOCK
    valid = blk <= cur
    forced = (blk == 0) | (blk == cur) | (blk == cur - 1)
    score = jnp.where(valid, jnp.where(forced, FORCED_SCORE, imp_t), -1.0)
    sel_t, _ = _topk_select_cols(score, blk.astype(F32), SLC_TOPK)
    mb = jnp.where((sel_t > 0.0) & valid, 0.0, NEG_INF).T.astype(BF16)
    qs = jnp.concatenate([jnp.concatenate([mb] * HPG, axis=0), qp], axis=1)

    def tile(i, state, diag):
        m, l, acc = state
        off = pl.multiple_of(i * tk, tk)
        s = _dot_nt(qs, kp_ref[0, 0, pl.ds(off, tk), :])
        if diag:
            kpos = off + lax.broadcasted_iota(jnp.int32, (1, tk), 1)
            s = _add_token_bias(s, jnp.where(kpos <= pos_t, 0.0, NEG_INF))
        m2 = jnp.maximum(m, jnp.max(s, axis=-1, keepdims=True))
        a = jnp.exp(m - m2)
        e = jnp.exp(s - m2)
        l = a * l + jnp.sum(e, axis=-1, keepdims=True)
        acc = a * acc + _dot(e.astype(BF16), vw_ref[0, 0, pl.ds(off, tk), 0:HEAD_DIM])
        return m2, l, acc

    n_full = start // tk
    carry = (jnp.full((rows, 1), NEG_INF, F32), jnp.zeros((rows, 1), F32), jnp.zeros((rows, HEAD_DIM), F32))
    carry = lax.fori_loop(0, n_full, lambda i, c: tile(i, c, False), carry)
    _, l_s, acc_s = tile(n_full, carry, True)
    o_s = acc_s / jnp.maximum(l_s, 1e-30)

    gn = gn_ref[0, 0]
    outs = []
    for h in range(HPG):
        sl = slice(h * Q_BLOCK, (h + 1) * Q_BLOCK)
        outs.append(gn[:, h:h + 1] * o_c[sl] + gn[:, HPG + h:HPG + h + 1] * o_s[sl]
                    + gn[:, 2 * HPG + h:2 * HPG + h + 1] * o_w[sl])
    o_ref[0] = jnp.concatenate(outs, axis=1)


def _gate_pack(gn, b, t):
    g = gn[:, :3 * N_HEADS].reshape(b, t, 3, N_KV, HPG).transpose(0, 3, 1, 2, 4).reshape(b, N_KV, t, 3 * HPG)
    return jnp.pad(g, ((0, 0), (0, 0), (0, 0), (0, 16 - 3 * HPG)))


def _nsa_prompt(q, kvs, kvw, gn, ckv, b, t):
    n_b = t // SLC_BLOCK
    assert n_b <= LANE and t % Q_BLOCK == 0 and t >= WINDOW + Q_BLOCK
    tk = min(1024, t)
    pos = jnp.arange(t, dtype=jnp.int32)
    qp = _query_pack(q.reshape(b, t, NSA_WIDTH), pos)
    ks = kvs.reshape(b, t, 2, N_KV, HEAD_DIM).transpose(2, 0, 3, 1, 4)
    kw = kvw.reshape(b, t, 2, N_KV, HEAD_DIM).transpose(2, 0, 3, 1, 4)
    onehot = (pos[:, None] // SLC_BLOCK == jnp.arange(LANE, dtype=jnp.int32)[None, :]).astype(F32)
    kconst = jnp.concatenate([onehot, jnp.zeros((t, HEAD_DIM), F32), _key_consts(pos)], axis=1)
    kp = (jnp.pad(ks[0], ((0, 0), (0, 0), (0, 0), (LANE, HEAD_DIM))) + kconst[None, None]).astype(BF16)
    vw = jnp.concatenate([ks[1], kw[1], kw[0], jnp.broadcast_to(_key_consts(pos), kw[0].shape)], axis=-1).astype(BF16)
    n_c = t // CMP_STRIDE
    c4 = ckv.reshape(b, n_c, N_KV, 2, HEAD_DIM).transpose(3, 0, 2, 1, 4)
    c_end = jnp.arange(n_c, dtype=jnp.int32) * CMP_STRIDE + (CMP_LEN - 1)
    ckp = jnp.concatenate([c4[0], jnp.broadcast_to(_key_consts(c_end), c4[0].shape), c4[1], jnp.zeros_like(c4[1])],
                          axis=-1).astype(BF16)
    pool = (jnp.arange(LANE)[:, None] == jnp.arange(n_c)[None, :] // (SLC_BLOCK // CMP_STRIDE)).astype(BF16)
    gn2 = _gate_pack(gn, b, t)
    return pl.pallas_call(
        functools.partial(_nsa_prompt_kernel, tk=tk),
        grid=(b, N_KV, t // Q_BLOCK),
        in_specs=[pl.BlockSpec((1, HPG, Q_BLOCK, LANE), lambda i, g, j: (i, g, j, 0)),
                  pl.BlockSpec((1, 1, Q_BLOCK, 16), lambda i, g, j: (i, g, j, 0)),
                  pl.BlockSpec((1, 1, n_c, 2 * LANE), lambda i, g, j: (i, g, 0, 0)),
                  pl.BlockSpec((1, 1, t, 2 * LANE), lambda i, g, j: (i, g, 0, 0)),
                  pl.BlockSpec((1, 1, t, 2 * LANE), lambda i, g, j: (i, g, 0, 0)),
                  pl.BlockSpec((LANE, n_c), lambda i, g, j: (0, 0))],
        out_specs=pl.BlockSpec((1, Q_BLOCK, HPG * HEAD_DIM), lambda i, g, j: (i, j, g)),
        out_shape=jax.ShapeDtypeStruct((b, t, NSA_WIDTH), F32),
        compiler_params=_cparams(3),
        name="nsa_prompt",
    )(qp, gn2, ckp, kp, vw, pool).reshape(b * t, NSA_WIDTH)


def _mem_attend_kernel(q_ref, kv_ref, o_ref):
    q = q_ref[0]
    kv = kv_ref[...].reshape(N_MEM, 2 * MEM_WIDTH)
    scale = MEM_HEAD_DIM ** -0.5
    for h in range(MEM_HEADS):
        sl = slice(h * MEM_HEAD_DIM, (h + 1) * MEM_HEAD_DIM)
        k = kv[:, sl].astype(BF16)
        v = kv[:, MEM_WIDTH + h * MEM_HEAD_DIM:MEM_WIDTH + (h + 1) * MEM_HEAD_DIM].astype(BF16)
        s = _dot_nt(q[:, sl].astype(BF16), k) * scale
        e = jnp.exp(s - jnp.max(s, axis=-1, keepdims=True))
        p = e / jnp.sum(e, axis=-1, keepdims=True)
        o_ref[0, :, sl] = _dot(p.astype(BF16), v)


def _mem_attend(mq, mkv, layer=None):
    b, t, _ = mq.shape
    tm = _row_tile(t, 512)
    if layer is None:
        kv_spec = pl.BlockSpec((1, N_MEM, 2 * MEM_WIDTH), lambda i, j: (i, 0, 0))
    else:
        kv_spec = pl.BlockSpec((1, 1, N_MEM, 2 * MEM_WIDTH), lambda i, j: (layer, i, 0, 0))
    return pl.pallas_call(
        _mem_attend_kernel,
        grid=(b, t // tm),
        in_specs=[pl.BlockSpec((1, tm, MEM_WIDTH), lambda i, j: (i, j, 0)), kv_spec],
        out_specs=pl.BlockSpec((1, tm, MEM_WIDTH), lambda i, j: (i, j, 0)),
        out_shape=jax.ShapeDtypeStruct((b, t, MEM_WIDTH), F32),
        compiler_params=_cparams(2),
        name="mem_attend",
    )(mq, mkv)


def _ssm_disc_kernel(ar_ref, ai_ref, ldt_ref, br_ref, bi_ref, abr_ref, abi_ref, bbr_ref, bbi_ref):
    ar = ar_ref[...]
    ai = ai_ref[...]
    dt = jnp.exp(ldt_ref[...])
    mag = jnp.exp(dt * ar)
    abr = mag * jnp.cos(dt * ai)
    abi = mag * jnp.sin(dt * ai)
    den = ar * ar + ai * ai
    fr = ((abr - 1.0) * ar + abi * ai) / den
    fi = (abi * ar - (abr - 1.0) * ai) / den
    br = br_ref[...]
    bi = bi_ref[...]
    abr_ref[...] = abr
    abi_ref[...] = abi
    bbr_ref[...] = fr * br - fi * bi
    bbi_ref[...] = fr * bi + fi * br


def _ssm_discretise(a_re, a_im, log_dt, b_re, b_im):
    rep = lambda x: jnp.repeat(x, SSM_GROUP, axis=0)
    tb = lambda x: x.transpose(0, 2, 1).reshape(SSM_WIDTH, SSM_STATE)
    shp = jax.ShapeDtypeStruct((SSM_WIDTH, SSM_STATE), F32)
    abr, abi, bbr, bbi = pl.pallas_call(
        _ssm_disc_kernel, out_shape=[shp] * 4, name="ssm_disc",
    )(rep(a_re), rep(a_im), rep(jnp.broadcast_to(log_dt[:, None], a_re.shape)), tb(b_re), tb(b_im))
    eye = jnp.eye(SSM_GPB, dtype=F32)

    def in_map(bb):
        bb = bb.reshape(SSM_SLABS, SSM_GPB, SSM_GROUP, SSM_STATE)
        return jnp.einsum('jgcp,gG->jgcGp', bb, eye).reshape(SSM_SLABS, LANE, SSM_GPB * SSM_STATE).astype(BF16)

    flat = lambda x: x[::SSM_GROUP].reshape(1, SSM_FLAT)
    return flat(abr), flat(abi), in_map(bbr), in_map(bbi)


def _ssm_out_map(c):
    eye = jnp.eye(SSM_GPB, dtype=F32)
    c = c.reshape(SSM_SLABS, SSM_GPB, SSM_GROUP, SSM_STATE)
    return jnp.einsum('jgcp,gG->jgpGc', c, eye).reshape(SSM_SLABS, SSM_GPB * SSM_STATE, LANE).astype(BF16)


def _ssm_kernel(u_ref, h0r_ref, h0i_ref, abr_ref, abi_ref, bdr_ref, bdi_ref, cdr_ref, cdi_ref, d_ref,
                wg_ref, bg_ref, o_ref, hr_ref, hi_ref, sr_ref, si_ref, st_ref, *, n_scan):
    c_idx = pl.program_id(1)
    slab = SSM_GPB * SSM_STATE

    @pl.when(c_idx == 0)
    def _():
        st_ref[0:1, :] = h0r_ref[0]
        st_ref[1:2, :] = h0i_ref[0]

    u = u_ref[0]
    ub = u.astype(BF16)
    for j in range(SSM_SLABS):
        uj = ub[:, j * LANE:(j + 1) * LANE]
        sr_ref[:, j * slab:(j + 1) * slab] = _dot(uj, bdr_ref[j])
        si_ref[:, j * slab:(j + 1) * slab] = _dot(uj, bdi_ref[j])

    ar = abr_ref[...]
    ai = abi_ref[...]

    def step(t, carry):
        hr, hi = carry
        nr = ar * hr - ai * hi + sr_ref[pl.ds(t, 1), :]
        ni = ar * hi + ai * hr + si_ref[pl.ds(t, 1), :]
        sr_ref[pl.ds(t, 1), :] = nr
        si_ref[pl.ds(t, 1), :] = ni
        return nr, ni

    hr, hi = lax.fori_loop(0, n_scan, step, (st_ref[0:1, :], st_ref[1:2, :]))
    st_ref[0:1, :] = hr
    st_ref[1:2, :] = hi
    hr_ref[0] = hr
    hi_ref[0] = hi

    ys = []
    for j in range(SSM_SLABS):
        hrj = sr_ref[:, j * slab:(j + 1) * slab].astype(BF16)
        hij = si_ref[:, j * slab:(j + 1) * slab].astype(BF16)
        ys.append(_dot(hrj, cdr_ref[j]) - _dot(hij, cdi_ref[j]))
    y = jnp.concatenate(ys, axis=1) + d_ref[...] * u
    y = jax.nn.gelu(y)
    o_ref[0] = y * jax.nn.sigmoid(_dot(y.astype(BF16), wg_ref[...]) + bg_ref[...])


def _ssm(u, h0r, h0i, disc, cdr, cdi, d, w_glu, b_glu, t_valid=None):
    b, t, _ = u.shape
    tc = _row_tile(t, 256)
    n_scan = tc if t_valid is None else t_valid
    assert t_valid is None or t == tc
    abr, abi, bdr, bdi = disc
    full = lambda a: pl.BlockSpec(a.shape, lambda i, c: (0,) * a.ndim)
    state_spec = pl.BlockSpec((1, 1, SSM_FLAT), lambda i, c: (i, 0, 0))
    args = (abr, abi, bdr, bdi, cdr, cdi, d.reshape(1, SSM_WIDTH), w_glu.astype(BF16), b_glu.reshape(1, SSM_WIDTH))
    o, hr, hi = pl.pallas_call(
        functools.partial(_ssm_kernel, n_scan=n_scan),
        grid=(b, t // tc),
        in_specs=[pl.BlockSpec((1, tc, SSM_WIDTH), lambda i, c: (i, c, 0)), state_spec, state_spec]
                 + [full(a) for a in args],
        out_specs=[pl.BlockSpec((1, tc, SSM_WIDTH), lambda i, c: (i, c, 0)), state_spec, state_spec],
        out_shape=[jax.ShapeDtypeStruct((b, t, SSM_WIDTH), F32),
                   jax.ShapeDtypeStruct((b, 1, SSM_FLAT), F32),
                   jax.ShapeDtypeStruct((b, 1, SSM_FLAT), F32)],
        scratch_shapes=[pltpu.VMEM((tc, SSM_FLAT), F32), pltpu.VMEM((tc, SSM_FLAT), F32),
                        pltpu.VMEM((8, SSM_FLAT), F32)],
        compiler_params=_cparams(2),
        name="ssm",
    )(u, h0r.reshape(b, 1, SSM_FLAT), h0i.reshape(b, 1, SSM_FLAT), *args)
    shp = (b, N_SSM_GROUPS, SSM_STATE)
    return o, hr.reshape(shp), hi.reshape(shp)


def _merge_kernel(x_ref, oa_ref, ob_ref, oc_ref, gb_ref, w_ref, o_ref):
    acc = x_ref[...]
    for i, r in enumerate((oa_ref, ob_ref, oc_ref)):
        w = r.shape[1]
        off = sum((NSA_WIDTH, SSM_WIDTH, MEM_WIDTH)[:i])
        mixed = (r[...] * gb_ref[:, off:off + w]).astype(BF16)
        acc = acc + _dot(mixed, w_ref[off:off + w, :])
    o_ref[...] = acc


def _merge(x2, o_a, o_b, o_c, gb, w_o_bf16):
    n = x2.shape[0]
    tm = _row_tile(n, 512)
    row = lambda w: pl.BlockSpec((tm, w), lambda i: (i, 0))
    return pl.pallas_call(
        _merge_kernel,
        grid=(n // tm,),
        in_specs=[row(D_MODEL), row(NSA_WIDTH), row(SSM_WIDTH), row(MEM_WIDTH), row(MIX_WIDTH),
                  pl.BlockSpec((MIX_WIDTH, D_MODEL), lambda i: (0, 0))],
        out_specs=row(D_MODEL),
        out_shape=jax.ShapeDtypeStruct((n, D_MODEL), F32),
        compiler_params=_cparams(1),
        name="merge",
    )(x2, o_a, o_b, o_c, gb, w_o_bf16)


def _ffn_kernel(x_ref, g_ref, w1_ref, w2_ref, fg_ref, o_ref, h_ref, acc_ref, *, final_norm):
    k = pl.program_id(1)

    @pl.when(k == 0)
    def _():
        h_ref[...] = _rms(x_ref[...], g_ref[...]).astype(BF16)
        acc_ref[...] = jnp.zeros_like(acc_ref)

    a = jnp.square(jnp.maximum(_dot(h_ref[...], w1_ref[...]), 0.0))
    acc_ref[...] += _dot(a.astype(BF16), w2_ref[...])

    @pl.when(k == pl.num_programs(1) - 1)
    def _():
        y = x_ref[...] + acc_ref[...]
        if final_norm:
            y = _rms(y, fg_ref[...])
        o_ref[...] = y


def _ffn(x2, g, w1_bf16, w2_bf16, final_g, final_norm):
    n = x2.shape[0]
    tm = _row_tile(n, 1024)
    tf = 1024
    return pl.pallas_call(
        functools.partial(_ffn_kernel, final_norm=final_norm),
        grid=(n // tm, D_FF // tf),
        in_specs=[pl.BlockSpec((tm, D_MODEL), lambda i, k: (i, 0)),
                  pl.BlockSpec((1, D_MODEL), lambda i, k: (0, 0)),
                  pl.BlockSpec((D_MODEL, tf), lambda i, k: (0, k)),
                  pl.BlockSpec((tf, D_MODEL), lambda i, k: (k, 0)),
                  pl.BlockSpec((1, D_MODEL), lambda i, k: (0, 0))],
        out_specs=pl.BlockSpec((tm, D_MODEL), lambda i, k: (i, 0)),
        out_shape=jax.ShapeDtypeStruct((n, D_MODEL), F32),
        scratch_shapes=[pltpu.VMEM((tm, D_MODEL), BF16), pltpu.VMEM((tm, D_MODEL), F32)],
        compiler_params=_cparams(2),
        name="ffn",
    )(x2, g.reshape(1, D_MODEL), w1_bf16, w2_bf16, final_g.reshape(1, D_MODEL))


def _layer_params(l, norm1_g, w_in, cmp_pe, cmp_w1, cmp_w2, ssm_a_re, ssm_a_im, ssm_log_dt, ssm_b_re, ssm_b_im,
                  ssm_c_re, ssm_c_im, ssm_d, w_glu, b_glu, mem_norm_g, w_mem_kv, w_o, norm2_g, w_ff1, w_ff2):
    return dict(
        norm1_g=norm1_g[l], w_in=_pack_w_in(w_in[l]),
        pe=_flat_cmp_pe(cmp_pe[l]), w1=_expand_cmp_w1(cmp_w1[l]), w2=_expand_cmp_w2(cmp_w2[l]),
        disc=_ssm_discretise(ssm_a_re[l], ssm_a_im[l], ssm_log_dt[l], ssm_b_re[l], ssm_b_im[l]),
        cdr=_ssm_out_map(ssm_c_re[l]), cdi=_ssm_out_map(ssm_c_im[l]), d=ssm_d[l],
        w_glu=w_glu[l], b_glu=b_glu[l], mem_norm_g=mem_norm_g[l], w_mem_kv=w_mem_kv[l].astype(BF16),
        w_o=w_o[l].astype(BF16), norm2_g=norm2_g[l], w_ff1=w_ff1[l].astype(BF16), w_ff2=w_ff2[l].astype(BF16))


def _prompt_layer(x2, mem2, p, b, t, final_g, final_norm):
    q, kvc, kvs, kvw, u, mq, gb, gn = _in_proj(x2, p['norm1_g'], p['w_in'])
    lh = _cmp_lh(kvc.reshape(b * t // CMP_STRIDE, CMP_ROW), p['pe'], p['w1'])
    ckv = _cmp_tokens_prompt(lh.reshape(b, t // CMP_STRIDE, 4 * KV_WIDTH), p['w2'])
    o_a = _nsa_prompt(q, kvs, kvw, gn, ckv, b, t)
    zero = jnp.zeros((b, N_SSM_GROUPS, SSM_STATE), F32)
    o_b, hr, hi = _ssm(u.reshape(b, t, SSM_WIDTH), zero, zero, p['disc'], p['cdr'], p['cdi'], p['d'],
                       p['w_glu'], p['b_glu'])
    mkv = _norm_mm(mem2, p['mem_norm_g'], p['w_mem_kv'])
    o_c = _mem_attend(mq.reshape(b, t, MEM_WIDTH), mkv.reshape(b, N_MEM, 2 * MEM_WIDTH))
    x2 = _merge(x2, o_a, o_b.reshape(b * t, SSM_WIDTH), o_c.reshape(b * t, MEM_WIDTH), gb, p['w_o'])
    x2 = _ffn(x2, p['norm2_g'], p['w_ff1'], p['w_ff2'], final_g, final_norm)
    kv_shape = (b, t, 2, N_KV, HEAD_DIM)
    wl = min(WINDOW, t)
    return x2, dict(cmp=kvc.reshape(kv_shape), slc=kvs.reshape(kv_shape),
                    win=kvw.reshape(kv_shape)[:, t - wl:], hr=hr, hi=hi,
                    mkv=mkv.reshape(b, N_MEM, 2, MEM_HEADS, MEM_HEAD_DIM))


CHUNKS_PER_PAGE = PAGE_SIZE // CMP_STRIDE


def _cmp_lh_paged_kernel(pt_ref, cache_ref, pe_ref, w_ref, o_ref, buf_ref, stage_ref, xl_ref, sem, *,
                         pages, layer, n_s, n_groups):
    step = pl.program_id(0)
    slot = step % 2

    def page_copy(grp, i, sl):
        page = pt_ref[grp // n_s, (grp % n_s) * pages + i]
        return pltpu.make_async_copy(cache_ref.at[layer, page], buf_ref.at[sl, i], sem.at[sl, i])

    @pl.when(step == 0)
    def _():
        for i in range(pages):
            page_copy(step, i, slot).start()

    @pl.when(step + 1 < n_groups)
    def _():
        for i in range(pages):
            page_copy(step + 1, i, 1 - slot).start()

    for i in range(pages):
        page_copy(step, i, slot).wait()

    n_stage = stage_ref.shape[0]

    def relayout(it, carry):
        for u in range(n_stage):
            i = it * n_stage + u
            xt = buf_ref[slot, i].reshape(2 * KV_WIDTH, PAGE_SIZE).T
            base = pl.multiple_of(i * CHUNKS_PER_PAGE, CHUNKS_PER_PAGE)
            for h in range(2 * KV_WIDTH // LANE):
                stage_ref[u, h] = xt[:, h * LANE:(h + 1) * LANE]
                for l in range(CMP_STRIDE):
                    xl_ref[l, pl.ds(base, CHUNKS_PER_PAGE), h * LANE:(h + 1) * LANE] = (
                        stage_ref[u, h, pl.ds(l, CHUNKS_PER_PAGE, stride=CMP_STRIDE), :])
        return carry

    lax.fori_loop(0, pages // n_stage, relayout, 0)
    lo = jnp.zeros((pages * CHUNKS_PER_PAGE, 2 * KV_WIDTH), F32)
    hi = jnp.zeros((pages * CHUNKS_PER_PAGE, 2 * KV_WIDTH), F32)
    for l in range(CMP_STRIDE):
        x = xl_ref[l]
        lo = lo + _dot((x + pe_ref[0, l]).astype(BF16), w_ref[0, l])
        hi = hi + _dot((x + pe_ref[1, l]).astype(BF16), w_ref[1, l])
    o_ref[0] = jnp.concatenate([lo, hi], axis=1)


def _cmp_lh_paged(cache_t, layer, page_table, pe_flat, w1_big):
    b, n_pages = page_table.shape
    pages = math.gcd(n_pages, 32)
    rows = pages * CHUNKS_PER_PAGE
    n_s = n_pages // pages
    n_groups = b * n_s
    page_shape = cache_t.shape[2:]

    grid_spec = pltpu.PrefetchScalarGridSpec(
        num_scalar_prefetch=1,
        grid=(n_groups,),
        in_specs=[pl.BlockSpec(memory_space=pl.ANY),
                  pl.BlockSpec((2, CMP_STRIDE, 1, 2 * KV_WIDTH), lambda t, pt: (0, 0, 0, 0)),
                  pl.BlockSpec((2, CMP_STRIDE, 2 * KV_WIDTH, 2 * KV_WIDTH), lambda t, pt: (0, 0, 0, 0))],
        out_specs=pl.BlockSpec((1, rows, 4 * KV_WIDTH), lambda t, pt: (t // n_s, t % n_s, 0)),
        scratch_shapes=[pltpu.VMEM((2, pages) + page_shape, F32),
                        pltpu.VMEM((math.gcd(pages, 8), 2 * KV_WIDTH // LANE, PAGE_SIZE, LANE), F32),
                        pltpu.VMEM((CMP_STRIDE, rows, 2 * KV_WIDTH), F32),
                        pltpu.SemaphoreType.DMA((2, pages))])
    return pl.pallas_call(
        functools.partial(_cmp_lh_paged_kernel, pages=pages, layer=layer, n_s=n_s, n_groups=n_groups),
        grid_spec=grid_spec,
        out_shape=jax.ShapeDtypeStruct((b, n_pages * CHUNKS_PER_PAGE, 4 * KV_WIDTH), F32),
        compiler_params=_cparams(1),
        name="cmp_lh_paged",
    )(page_table, cache_t, pe_flat.reshape(2, CMP_STRIDE, 1, 2 * KV_WIDTH), w1_big)


TOK_PAD = 8


def _nsa_sample_select_kernel(lhc_ref, lht_ref, w2_ref, q_ref, pool_ref, oc_ref, idx_ref, lh_ref, *, past, n_tok):
    n_cache = lhc_ref.shape[1]
    n_tail = lht_ref.shape[1]
    r = lh_ref.shape[0]
    lh_ref[0:n_cache, :] = lhc_ref[0]
    lh_ref[n_cache:n_cache + n_tail, :] = lht_ref[0]
    lh_ref[n_cache + n_tail:r, :] = jnp.zeros((r - n_cache - n_tail, lh_ref.shape[1]), F32)
    ckv = _cmp_combine(lh_ref[...], w2_ref[...])

    rows = HPG * TOK_PAD
    row = lax.broadcasted_iota(jnp.int32, (rows, 1), 0)
    pos_q = past + (row & (TOK_PAD - 1))
    ci = lax.broadcasted_iota(jnp.int32, (1, r), 1)
    d_c = pos_q - (ci * CMP_STRIDE + (CMP_LEN - 1))
    n_blk = pool_ref.shape[0]
    ps_all = []
    for g in range(N_KV):
        qg = q_ref[0, g]
        q = qg[:, :HEAD_DIM]
        slope = qg[:, HEAD_DIM:HEAD_DIM + 1].astype(F32)
        ck = ckv[:, g * 2 * HEAD_DIM:g * 2 * HEAD_DIM + HEAD_DIM].astype(BF16)
        cv = ckv[:, g * 2 * HEAD_DIM + HEAD_DIM:(g + 1) * 2 * HEAD_DIM].astype(BF16)
        s = _dot_nt(q, ck) - slope * d_c.astype(F32)
        p_c = _masked_probs(s, d_c >= 0)
        oc_ref[0, g] = _dot(p_c.astype(BF16), cv)
        ps = p_c[0:TOK_PAD]
        for h in range(1, HPG):
            ps = ps + p_c[h * TOK_PAD:(h + 1) * TOK_PAD]
        ps_all.append(ps)
    ps = jnp.concatenate(ps_all + [jnp.zeros((LANE - N_KV * TOK_PAD, r), F32)], axis=0)
    imp_t = _pool_blocks_t(ps, pool_ref[...])
    blk = lax.broadcasted_iota(jnp.int32, (n_blk, 1), 0)
    cur = (past + (lax.broadcasted_iota(jnp.int32, (1, LANE), 1) & (TOK_PAD - 1))) // SLC_BLOCK
    valid = blk <= cur
    forced = (blk == 0) | (blk == cur) | (blk == cur - 1)
    score = jnp.where(valid, jnp.where(forced, FORCED_SCORE, imp_t), -1.0)
    _, picks = _topk_select_cols(score, blk.astype(F32), SLC_TOPK)
    for k, pk in enumerate(picks):
        idx_ref[0, k:k + 1, :] = pk.astype(jnp.int32)


def _nsa_sample_select(lh_cache, lh_tail, w2_big, qsel, past, n_tok):
    b, n_cache, _ = lh_cache.shape
    n_tail = lh_tail.shape[1]
    r = -(-(n_cache + n_tail) // LANE) * LANE
    n_b = past // SLC_BLOCK + 1
    assert n_b >= SLC_TOPK
    n_blk = -(-n_b // 8) * 8
    pool = (jnp.arange(n_blk)[:, None] == jnp.arange(r)[None, :] // (SLC_BLOCK // CMP_STRIDE)).astype(BF16)
    rows = HPG * TOK_PAD
    o_c, idx = pl.pallas_call(
        functools.partial(_nsa_sample_select_kernel, past=past, n_tok=n_tok),
        grid=(b,),
        in_specs=[pl.BlockSpec((1, n_cache, 4 * KV_WIDTH), lambda i: (i, 0, 0)),
                  pl.BlockSpec((1, n_tail, 4 * KV_WIDTH), lambda i: (i, 0, 0)),
                  pl.BlockSpec((2 * KV_WIDTH, 2 * KV_WIDTH), lambda i: (0, 0)),
                  pl.BlockSpec((1, N_KV, rows, LANE), lambda i: (i, 0, 0, 0)),
                  pl.BlockSpec((n_blk, r), lambda i: (0, 0))],
        out_specs=[pl.BlockSpec((1, N_KV, rows, HEAD_DIM), lambda i: (i, 0, 0, 0)),
                   pl.BlockSpec((1, SLC_TOPK, LANE), lambda i: (i, 0, 0))],
        out_shape=[jax.ShapeDtypeStruct((b, N_KV, rows, HEAD_DIM), F32),
                   jax.ShapeDtypeStruct((b, SLC_TOPK, LANE), jnp.int32)],
        scratch_shapes=[pltpu.VMEM((r, 4 * KV_WIDTH), F32)],
        compiler_params=_cparams(1),
        name="nsa_sample_select",
    )(lh_cache, lh_tail, w2_big, qsel, pool)
    idx = idx[:, :, :N_KV * TOK_PAD].reshape(b, SLC_TOPK, N_KV, TOK_PAD)[:, :, :, :n_tok]
    return o_c, idx.transpose(0, 2, 3, 1).reshape(-1)


def _rows_minor(cache):
    n = cache.ndim
    return cache.transpose(tuple(range(n - 4)) + (n - 3, n - 2, n - 1, n - 4))


def _nsa_sample_attend_kernel(idx_ref, pt_ref, cache_ref, q_ref, oc_ref, gate_ref, snew_ref, win_ref, wnew_ref,
                              o_ref, kvbuf_ref, sem, *, layer, past, n_tok):
    b = pl.program_id(0)
    n_pages = past // PAGE_SIZE
    cur_blk = past // SLC_BLOCK
    per_page = PAGE_SIZE // SLC_BLOCK
    n_sel = SLC_TOPK * PAGE_SIZE

    half = b % 2

    def sel_block(g, t, k, bb=b):
        return idx_ref[((bb * N_KV + g) * n_tok + t) * SLC_TOPK + k]

    def page_copies(bb, hf, g, t, k):
        c = jnp.minimum(sel_block(g, t, k, bb), cur_blk - 1)
        page = pt_ref[bb * n_pages + c // per_page]
        slot = g * n_tok + t
        dst = pl.ds(k * PAGE_SIZE, PAGE_SIZE)
        return tuple(pltpu.make_async_copy(cache_ref.at[layer, page, kv, g], kvbuf_ref.at[hf, slot, kv, :, dst],
                                           sem.at[hf, kv, slot]) for kv in range(2))

    def start_all(bb, hf):
        for g in range(N_KV):
            for t in range(n_tok):
                for k in range(SLC_TOPK):
                    for cp in page_copies(bb, hf, g, t, k):
                        cp.start()

    @pl.when(b == 0)
    def _():
        start_all(b, half)

    @pl.when(b + 1 < pl.num_programs(0))
    def _():
        start_all(b + 1, 1 - half)

    rows = HPG * TOK_PAD
    row = lax.broadcasted_iota(jnp.int32, (rows, 1), 0)
    tok = row & (TOK_PAD - 1)
    pos_q = past + tok
    lane = lax.broadcasted_iota(jnp.int32, (1, n_sel), 1)
    lane_slot = lane // PAGE_SIZE
    lane_blk = (lane & (PAGE_SIZE - 1)) // SLC_BLOCK
    r_new = lax.broadcasted_iota(jnp.int32, (1, TOK_PAD), 1)
    d_new = pos_q - (past + r_new)
    mask_new = (d_new >= 0) & (r_new < n_tok)
    wb = win_ref.shape[-1]
    d_win = pos_q - (past - wb + lax.broadcasted_iota(jnp.int32, (1, wb), 1))
    mask_win = (d_win >= 0) & (d_win <= WINDOW)

    def joint_attend(q, slope, kt_old, vt_old, d_old, mask_old, k_new, v_new):
        s_o = jnp.where(mask_old, _dot(q, kt_old) - slope * d_old.astype(F32), NEG_INF)
        s_n = jnp.where(mask_new, _dot_nt(q, k_new) - slope * d_new.astype(F32), NEG_INF)
        m = jnp.maximum(jnp.max(s_o, axis=-1, keepdims=True), jnp.max(s_n, axis=-1, keepdims=True))
        e_o = jnp.where(mask_old, jnp.exp(s_o - m), 0.0)
        e_n = jnp.where(mask_new, jnp.exp(s_n - m), 0.0)
        l = jnp.sum(e_o, axis=-1, keepdims=True) + jnp.sum(e_n, axis=-1, keepdims=True)
        return (_dot_nt(e_o.astype(BF16), vt_old) + _dot(e_n.astype(BF16), v_new)) / jnp.maximum(l, 1e-30)

    for g in range(N_KV):
        ksl = slice(g * HEAD_DIM, (g + 1) * HEAD_DIM)
        vsl = slice(KV_WIDTH + g * HEAD_DIM, KV_WIDTH + (g + 1) * HEAD_DIM)
        qg = q_ref[0, g]
        q = qg[:, :HEAD_DIM]
        slope = qg[:, HEAD_DIM:HEAD_DIM + 1].astype(F32)
        k_snew = snew_ref[0][:, ksl].astype(BF16)
        v_snew = snew_ref[0][:, vsl].astype(BF16)
        o_s = jnp.zeros((rows, HEAD_DIM), F32)
        for t in range(n_tok):
            for k in range(SLC_TOPK):
                for cp in page_copies(b, half, g, t, k):
                    cp.wait()
            blk_of_lane = jnp.zeros((1, n_sel), jnp.int32)
            for k in range(SLC_TOPK):
                blk_of_lane = jnp.where(lane_slot == k, sel_block(g, t, k), blk_of_lane)
            d_sel = pos_q - (blk_of_lane * SLC_BLOCK + (lane & (SLC_BLOCK - 1)))
            mask_sel = (d_sel >= 0) & (blk_of_lane < cur_blk) & (blk_of_lane % per_page == lane_blk)
            o_t = joint_attend(q, slope, kvbuf_ref[half, g * n_tok + t, 0].astype(BF16),
                               kvbuf_ref[half, g * n_tok + t, 1].astype(BF16), d_sel, mask_sel, k_snew, v_snew)
            o_s = jnp.where(tok == t, o_t, o_s)
        o_w = joint_attend(q, slope, win_ref[0, 0, 0, g].astype(BF16), win_ref[0, 0, 1, g].astype(BF16), d_win,
                           mask_win, wnew_ref[0][:, ksl].astype(BF16), wnew_ref[0][:, vsl].astype(BF16))
        gate = gate_ref[0, g]
        o_ref[0, g] = gate[:, 0:1] * oc_ref[0, g] + gate[:, 1:2] * o_s + gate[:, 2:3] * o_w


def _nsa_sample_attend(idx, page_table, cache_slc_t, cache_win_t, layer, qsel, o_c, gates, slc_new, win_new, past, n_tok):
    b = qsel.shape[0]
    assert past % PAGE_SIZE == 0 and n_tok <= TOK_PAD
    rows = HPG * TOK_PAD
    wb = cache_win_t.shape[-1]
    per_b = lambda shape: pl.BlockSpec((1,) + shape, lambda i, *_: (i,) + (0,) * len(shape))
    grid_spec = pltpu.PrefetchScalarGridSpec(
        num_scalar_prefetch=2,
        grid=(b,),
        in_specs=[pl.BlockSpec(memory_space=pl.ANY),
                  per_b((N_KV, rows, LANE)), per_b((N_KV, rows, HEAD_DIM)), per_b((N_KV, rows, 16)),
                  per_b((TOK_PAD, 2 * KV_WIDTH)),
                  pl.BlockSpec((1, 1, 2, N_KV, HEAD_DIM, wb), lambda i, *_: (layer, i, 0, 0, 0, 0)),
                  per_b((TOK_PAD, 2 * KV_WIDTH))],
        out_specs=per_b((N_KV, rows, HEAD_DIM)),
        scratch_shapes=[pltpu.VMEM((2, N_KV * n_tok, 2, HEAD_DIM, SLC_TOPK * PAGE_SIZE), F32),
                        pltpu.SemaphoreType.DMA((2, 2, N_KV * n_tok))])
    return pl.pallas_call(
        functools.partial(_nsa_sample_attend_kernel, layer=layer, past=past, n_tok=n_tok),
        grid_spec=grid_spec,
        out_shape=jax.ShapeDtypeStruct((b, N_KV, rows, HEAD_DIM), F32),
        compiler_params=_cparams(1),
        name="nsa_sample_attend",
    )(idx, page_table.reshape(-1), cache_slc_t, qsel, o_c, gates, slc_new, cache_win_t, win_new)


def _pad_tokens(x, axis):
    pad = [(0, 0)] * x.ndim
    pad[axis] = (0, TOK_PAD - x.shape[axis])
    return jnp.pad(x, pad)


def _sample_layer(x2, p, b, s, layer, cache_cmp_t, cache_slc_t, cache_win_t, cache_win, h0r, h0i, cache_mem,
                  page_table, final_g, final_norm):
    past = page_table.shape[1] * PAGE_SIZE
    assert past % SLC_BLOCK == 0 and s <= CMP_STRIDE
    q, kvc, kvs, kvw, u, mq, gb, gn = _in_proj(x2, p['norm1_g'], p['w_in'])
    lh_cache = _cmp_lh_paged(cache_cmp_t, layer, page_table, p['pe'], p['w1'])
    tail = jnp.pad(kvc.reshape(b, s, 2 * KV_WIDTH), ((0, 0), (0, CHUNKS_PER_PAGE * CMP_STRIDE - s), (0, 0)))
    lh_tail = _cmp_lh(tail.reshape(b * CHUNKS_PER_PAGE, CMP_ROW), p['pe'], p['w1'])
    pos = past + jnp.arange(s, dtype=jnp.int32)
    qsel = _pad_tokens(_query_pack(q.reshape(b, s, NSA_WIDTH), pos), 2).reshape(b, N_KV, HPG * TOK_PAD, LANE)
    o_c, idx = _nsa_sample_select(lh_cache, lh_tail.reshape(b, CHUNKS_PER_PAGE, 4 * KV_WIDTH), p['w2'], qsel, past, s)
    gates = gn[:, :3 * N_HEADS].reshape(b, s, 3, N_KV, HPG).transpose(0, 3, 4, 1, 2)
    gates = jnp.pad(_pad_tokens(gates, 3), ((0, 0),) * 4 + ((0, 16 - 3),)).reshape(b, N_KV, HPG * TOK_PAD, 16)
    slc_new = kvs.reshape(b, s, 2 * KV_WIDTH)
    win_new = kvw.reshape(b, s, 2 * KV_WIDTH)
    o = _nsa_sample_attend(idx, page_table, cache_slc_t, cache_win_t, layer, qsel, o_c, gates,
                           _pad_tokens(slc_new, 1), _pad_tokens(win_new, 1), past, s)
    o_a = o.reshape(b, N_KV, HPG, TOK_PAD, HEAD_DIM)[:, :, :, :s].transpose(0, 3, 1, 2, 4).reshape(b * s, NSA_WIDTH)
    o_b, hr, hi = _ssm(_pad_tokens(u.reshape(b, s, SSM_WIDTH), 1), h0r, h0i, p['disc'], p['cdr'], p['cdi'], p['d'],
                       p['w_glu'], p['b_glu'], t_valid=s)
    o_c2 = _mem_attend(_pad_tokens(mq.reshape(b, s, MEM_WIDTH), 1), cache_mem, layer)
    x2 = _merge(x2, o_a, o_b[:, :s].reshape(b * s, SSM_WIDTH), o_c2[:, :s].reshape(b * s, MEM_WIDTH), gb, p['w_o'])
    x2 = _ffn(x2, p['norm2_g'], p['w_ff1'], p['w_ff2'], final_g, final_norm)
    kv_shape = (b, s, 2, N_KV, HEAD_DIM)
    new_win = jnp.concatenate([cache_win, kvw.reshape(kv_shape).astype(cache_win.dtype)], axis=1)[:, s:]
    return x2, dict(cmp=kvc.reshape(kv_shape), slc=kvs.reshape(kv_shape), win=new_win, hr=hr, hi=hi)


def kernel(x_prompt, x_sample, cache_cmp_kv, cache_slc_kv, cache_win_kv, state_ssm_re, state_ssm_im, cache_mem_kv,
           page_table, mem_prompt, norm1_g, w_in, cmp_pe, cmp_w1, cmp_w2, ssm_a_re, ssm_a_im, ssm_log_dt, ssm_b_re,
           ssm_b_im, ssm_c_re, ssm_c_im, ssm_d, w_glu, b_glu, mem_norm_g, w_mem_kv, w_o, norm2_g, w_ff1, w_ff2,
           final_norm_g):
    bp, t, _ = x_prompt.shape
    bs, s, _ = x_sample.shape
    depth = w_in.shape[0]
    xp = x_prompt.reshape(bp * t, D_MODEL)
    xs = x_sample.reshape(bs * s, D_MODEL)
    mem2 = mem_prompt.reshape(bp * N_MEM, D_MODEL)
    cache_cmp_t = _rows_minor(cache_cmp_kv)
    cache_slc_t = _rows_minor(cache_slc_kv)
    cache_win_t = _rows_minor(cache_win_kv)
    cache_mem = cache_mem_kv.reshape(depth, bs, N_MEM, 2 * MEM_WIDTH)
    aux_p, aux_s = [], []
    for l in range(depth):
        p = _layer_params(l, norm1_g, w_in, cmp_pe, cmp_w1, cmp_w2, ssm_a_re, ssm_a_im, ssm_log_dt, ssm_b_re,
                          ssm_b_im, ssm_c_re, ssm_c_im, ssm_d, w_glu, b_glu, mem_norm_g, w_mem_kv, w_o, norm2_g,
                          w_ff1, w_ff2)
        last = l == depth - 1
        xp, ap = _prompt_layer(xp, mem2, p, bp, t, final_norm_g, last)
        xs, a_s = _sample_layer(xs, p, bs, s, l, cache_cmp_t, cache_slc_t, cache_win_t, cache_win_kv[l],
                                state_ssm_re[l], state_ssm_im[l], cache_mem, page_table, final_norm_g, last)
        aux_p.append(ap)
        aux_s.append(a_s)
    stack = lambda aux, key: jnp.stack([a[key] for a in aux])
    return (xp.reshape(bp, t, D_MODEL), xs.reshape(bs, s, D_MODEL),
            stack(aux_p, 'cmp'), stack(aux_s, 'cmp'), stack(aux_p, 'slc'), stack(aux_s, 'slc'),
            stack(aux_p, 'win'), stack(aux_s, 'win'), stack(aux_p, 'hr'), stack(aux_p, 'hi'),
            stack(aux_s, 'hr'), stack(aux_s, 'hi'), stack(aux_p, 'mkv'))
```

```python
import functools
import math

import jax
import jax.numpy as jnp
from jax import lax
from jax.experimental import pallas as pl
from jax.experimental.pallas import tpu as pltpu

F32 = jnp.float32
BF16 = jnp.bfloat16

D_MODEL = 1024
PAGE_SIZE = 128
N_HEADS = 8
HEAD_DIM = 64
N_KV = 2
HPG = N_HEADS // N_KV
CMP_STRIDE = 16
CMP_LEN = 2 * CMP_STRIDE
SLC_BLOCK = 64
SLC_TOPK = 16
WINDOW = 512
Q_BLOCK = 128
SSM_WIDTH = D_MODEL // 2
SSM_GROUP = 16
N_SSM_GROUPS = SSM_WIDTH // SSM_GROUP
SSM_STATE = 64
SSM_FLAT = N_SSM_GROUPS * SSM_STATE
N_MEM = 256
MEM_HEADS = 4
MEM_HEAD_DIM = 128
MEM_WIDTH = MEM_HEADS * MEM_HEAD_DIM
D_FF = 4 * D_MODEL
NSA_WIDTH = N_HEADS * HEAD_DIM
KV_WIDTH = N_KV * HEAD_DIM
MIX_WIDTH = NSA_WIDTH + SSM_WIDTH + MEM_WIDTH
OFF_KV = NSA_WIDTH
OFF_GN = OFF_KV + 6 * KV_WIDTH
OFF_U = OFF_GN + 3 * N_HEADS
OFF_MQ = OFF_U + SSM_WIDTH
OFF_GB = OFF_MQ + MEM_WIDTH
IN_WIDTH = OFF_GB + MIX_WIDTH
EPS = 1e-6
NEG_INF = -1e30
FORCED_SCORE = 1e9

LANE = 128
CMP_ROW = CMP_STRIDE * 2 * KV_WIDTH
SSM_GPB = LANE // SSM_GROUP
SSM_SLABS = SSM_WIDTH // LANE
VMEM_LIMIT = 56 * 2**20


def _cparams(n_grid, vmem=VMEM_LIMIT):
    return pltpu.CompilerParams(dimension_semantics=("arbitrary",) * n_grid, vmem_limit_bytes=vmem)


def _rms(x, g):
    return x * lax.rsqrt(jnp.mean(x * x, axis=-1, keepdims=True) + EPS) * g


def _dot(a, b):
    return jnp.dot(a, b, preferred_element_type=F32)


def _dot_nt(a, b):
    return lax.dot_general(a, b, (((1,), (1,)), ((), ())), preferred_element_type=F32)


def _masked_probs(s, mask):
    s = jnp.where(mask, s, NEG_INF)
    m = jnp.max(s, axis=-1, keepdims=True)
    e = jnp.where(mask, jnp.exp(s - m), 0.0)
    return e / jnp.maximum(jnp.sum(e, axis=-1, keepdims=True), 1e-30)


def _row_tile(n, pref):
    t = min(n, pref)
    assert n % t == 0
    return t


_SEGS = ((0, 512), (512, 768), (768, 1024), (1024, 1280), (1280, 1792), (1792, 2304), (2304, 3840), (3840, 3968))
_SIGMOID_FROM = 2304
_PACKED_WIDTH = 3968


def _pack_w_in(w):
    gn = jnp.pad(w[:, OFF_GN:OFF_U], ((0, 0), (0, LANE - 3 * N_HEADS)))
    return jnp.concatenate([w[:, :OFF_GN], w[:, OFF_U:], gn], axis=1).astype(BF16)


def _inproj_kernel(x_ref, g_ref, w_ref, *o_refs, seq_len):
    h = _rms(x_ref[...], g_ref[...]).astype(BF16)
    seg_refs = ((None,) if seq_len else ()) + o_refs
    z = []
    for (a, b), o_ref in zip(_SEGS, seg_refs):
        z.append(_dot(h, w_ref[:, a:b]))
        if a >= _SIGMOID_FROM:
            z[-1] = jax.nn.sigmoid(z[-1])
        if o_ref is not None:
            o_ref[...] = z[-1]
    if not seq_len:
        return
    qp_ref, kp_ref, vw_ref = o_refs[len(_SEGS) - 1:]
    tm = x_ref.shape[0]
    pos = (pl.program_id(0) * tm) % seq_len + lax.broadcasted_iota(jnp.int32, (tm, 1), 0)
    pos_hi = ((pos // SLC_BLOCK) * SLC_BLOCK).astype(F32)
    pos_lo = (pos % SLC_BLOCK).astype(F32)
    lane = lax.broadcasted_iota(jnp.int32, (1, HEAD_DIM), 1)
    terms = jnp.where(lane == 0, pos_hi, jnp.where(lane == 1, pos_lo, jnp.where(lane < 4, 1.0, 0.0)))
    onehot = jnp.where(pos // SLC_BLOCK == lax.broadcasted_iota(jnp.int32, (1, LANE), 1), 1.0, 0.0)
    q, kvs, kvw = z[0], z[2], z[3]
    head = lambda a, i: a[:, i * HEAD_DIM:(i + 1) * HEAD_DIM]
    for g in range(N_KV):
        kp_ref[g] = jnp.concatenate([onehot, head(kvs, g), terms], axis=1).astype(BF16)
        vw_ref[g] = jnp.concatenate([head(kvs, N_KV + g), head(kvw, N_KV + g), head(kvw, g), terms],
                                    axis=1).astype(BF16)
    for hd in range(N_HEADS):
        slope = 2.0 ** (-8.0 * (hd + 1) / N_HEADS)
        qterms = jnp.where(lane < 2, slope,
                           jnp.where(lane == 2, -slope * pos_hi, jnp.where(lane == 3, -slope * pos_lo, 0.0)))
        qp_ref[hd] = jnp.concatenate([head(q, hd) * (HEAD_DIM ** -0.5), qterms], axis=1).astype(BF16)


def _in_proj(x2, g, w_packed, seq_len=0):
    n = x2.shape[0]
    tm = _row_tile(n, 512)
    segs = _SEGS[1:] if seq_len else _SEGS
    out_specs = [pl.BlockSpec((tm, b - a), lambda i: (i, 0)) for a, b in segs]
    out_shape = [jax.ShapeDtypeStruct((n, b - a), F32) for a, b in segs]
    if seq_len:
        assert seq_len % tm == 0
        for lead, width in ((N_HEADS, LANE), (N_KV, 2 * LANE), (N_KV, 2 * LANE)):
            out_specs.append(pl.BlockSpec((lead, tm, width), lambda i: (0, i, 0)))
            out_shape.append(jax.ShapeDtypeStruct((lead, n, width), BF16))
    return pl.pallas_call(
        functools.partial(_inproj_kernel, seq_len=seq_len),
        grid=(n // tm,),
        in_specs=[pl.BlockSpec((tm, D_MODEL), lambda i: (i, 0)),
                  pl.BlockSpec((1, D_MODEL), lambda i: (0, 0)),
                  pl.BlockSpec((D_MODEL, _PACKED_WIDTH), lambda i: (0, 0))],
        out_specs=out_specs,
        out_shape=out_shape,
        compiler_params=_cparams(1),
        name="in_proj",
    )(x2, g.reshape(1, D_MODEL), w_packed)


def _norm_mm_kernel(x_ref, g_ref, w_ref, o_ref):
    o_ref[...] = _dot(_rms(x_ref[...], g_ref[...]).astype(BF16), w_ref[...])


def _norm_mm(x2, g, w_bf16):
    n, k = x2.shape
    m = w_bf16.shape[1]
    tm = _row_tile(n, 256)
    return pl.pallas_call(
        _norm_mm_kernel,
        grid=(n // tm,),
        in_specs=[pl.BlockSpec((tm, k), lambda i: (i, 0)),
                  pl.BlockSpec((1, k), lambda i: (0, 0)),
                  pl.BlockSpec((k, m), lambda i: (0, 0))],
        out_specs=pl.BlockSpec((tm, m), lambda i: (i, 0)),
        out_shape=jax.ShapeDtypeStruct((n, m), F32),
        compiler_params=_cparams(1),
        name="norm_mm",
    )(x2, g.reshape(1, k), w_bf16)


def _expand_cmp_w1(w1):
    w = w1.reshape(2, 2, CMP_STRIDE, HEAD_DIM, HEAD_DIM).astype(BF16)
    blocks = []
    for kv in range(2):
        for g in range(N_KV):
            off = (kv * N_KV + g) * HEAD_DIM
            blocks.append(jnp.pad(w[kv], ((0, 0), (0, 0), (0, 0), (off, 2 * KV_WIDTH - HEAD_DIM - off))))
    return jnp.concatenate(blocks, axis=2)


def _flat_cmp_pe(pe):
    p = pe.reshape(2, 2, CMP_STRIDE, HEAD_DIM)
    p = jnp.broadcast_to(p.transpose(1, 2, 0, 3)[:, :, :, None, :], (2, CMP_STRIDE, 2, N_KV, HEAD_DIM))
    return p.reshape(2, 1, CMP_ROW)


def _expand_cmp_w2(w2):
    eye2 = jnp.eye(2, dtype=w2.dtype)
    big = jnp.einsum('kde,kK,gG->kgdGKe', w2, eye2, eye2)
    return big.reshape(2 * KV_WIDTH, 2 * KV_WIDTH).astype(BF16)


def _cmp_lohi(x, pe_ref, w_ref):
    lo = _dot((x + pe_ref[0]).astype(BF16), w_ref[0])
    hi = _dot((x + pe_ref[1]).astype(BF16), w_ref[1])
    return jnp.concatenate([lo, hi], axis=1)


def _cmp_lh_kernel(x_ref, pe_ref, w_ref, o_ref):
    o_ref[...] = _cmp_lohi(x_ref[...], pe_ref, w_ref)


def _cmp_lh(x2, pe_flat, w1_big):
    n = x2.shape[0]
    tm = _row_tile(n, 256)
    return pl.pallas_call(
        _cmp_lh_kernel,
        grid=(n // tm,),
        in_specs=[pl.BlockSpec((tm, CMP_ROW), lambda i: (i, 0)),
                  pl.BlockSpec((2, 1, CMP_ROW), lambda i: (0, 0, 0)),
                  pl.BlockSpec((2, CMP_ROW, 2 * KV_WIDTH), lambda i: (0, 0, 0))],
        out_specs=pl.BlockSpec((tm, 4 * KV_WIDTH), lambda i: (i, 0)),
        out_shape=jax.ShapeDtypeStruct((n, 4 * KV_WIDTH), F32),
        compiler_params=_cparams(1),
        name="cmp_lh",
    )(x2, pe_flat, w1_big.reshape(2, CMP_ROW, 2 * KV_WIDTH))


def _cmp_combine(lh, w2_big):
    r = lh.shape[0]
    hi_next = pltpu.roll(lh[:, 2 * KV_WIDTH:], r - 1, 0)
    return _dot(jax.nn.gelu(lh[:, :2 * KV_WIDTH] + hi_next).astype(BF16), w2_big)


def _cmp_combine_kernel(lh_ref, w2_ref, o_ref):
    o_ref[0] = _cmp_combine(lh_ref[0], w2_ref[...])


def _cmp_tokens_prompt(lh, w2_big):
    b, r, _ = lh.shape
    return pl.pallas_call(
        _cmp_combine_kernel,
        grid=(b,),
        in_specs=[pl.BlockSpec((1, r, 4 * KV_WIDTH), lambda i: (i, 0, 0)),
                  pl.BlockSpec((2 * KV_WIDTH, 2 * KV_WIDTH), lambda i: (0, 0))],
        out_specs=pl.BlockSpec((1, r, 2 * KV_WIDTH), lambda i: (i, 0, 0)),
        out_shape=jax.ShapeDtypeStruct((b, r, 2 * KV_WIDTH), F32),
        compiler_params=_cparams(1),
        name="cmp_combine",
    )(lh, w2_big)


def _alibi_slopes():
    return jnp.exp2(-8.0 * (jnp.arange(N_HEADS, dtype=F32) + 1.0) / N_HEADS)


def _query_pack(q, pos):
    b, t, _ = q.shape
    qh = q.reshape(b, t, N_HEADS, HEAD_DIM).transpose(0, 2, 1, 3) * (HEAD_DIM ** -0.5)
    slope = _alibi_slopes()[:, None]
    pos_hi = ((pos // SLC_BLOCK) * SLC_BLOCK).astype(F32)[None, :]
    pos_lo = (pos % SLC_BLOCK).astype(F32)[None, :]
    ones = jnp.ones((1, t), F32)
    c = jnp.stack([slope * ones, slope * ones, -slope * pos_hi, -slope * pos_lo], axis=-1)
    c = jnp.pad(c, ((0, 0), (0, 0), (0, HEAD_DIM - 4)))
    c = jnp.broadcast_to(c[None], (b, N_HEADS, t, HEAD_DIM))
    return jnp.concatenate([qh, c], axis=-1).astype(BF16)


def _key_consts(kpos):
    hi = ((kpos // SLC_BLOCK) * SLC_BLOCK).astype(F32)
    lo = (kpos % SLC_BLOCK).astype(F32)
    one = jnp.ones_like(hi)
    return jnp.pad(jnp.stack([hi, lo, one, one], axis=-1), ((0, 0), (0, HEAD_DIM - 4)))


def _split3(x):
    hi = x.astype(BF16)
    r1 = x - hi.astype(F32)
    mid = r1.astype(BF16)
    lo = (r1 - mid.astype(F32)).astype(BF16)
    return hi, mid, lo


def _pool_blocks_t(ps, pool_t):
    hi, mid, lo = _split3(ps)
    return _dot_nt(pool_t, hi) + _dot_nt(pool_t, mid) + _dot_nt(pool_t, lo)


def _topk_select_cols(score, blk_f, k):
    sel = jnp.zeros(score.shape, F32)
    picks = []
    for _ in range(k):
        m = jnp.max(score, axis=0, keepdims=True)
        idx = jnp.min(jnp.where(score == m, blk_f, 1e9), axis=0, keepdims=True)
        pick = blk_f == idx
        sel = jnp.where(pick, 1.0, sel)
        score = jnp.where(pick, -3e38, score)
        picks.append(idx)
    return sel, picks


def _add_token_bias(s, bias):
    rows, n = s.shape
    q = bias.shape[0]
    return (s.reshape(rows // q, q, n) + bias[None]).reshape(rows, n)


def _biased_probs(s, bias, row_any=None):
    x = _add_token_bias(s, bias)
    e = jnp.exp(x - jnp.max(x, axis=-1, keepdims=True))
    inv = 1.0 / jnp.maximum(jnp.sum(e, axis=-1, keepdims=True), 1e-30)
    if row_any is not None:
        inv = jnp.where(row_any, inv, 0.0)
    return e * inv


def _nsa_prompt_kernel(qp_ref, gn_ref, ckp_ref, kp_ref, vw_ref, pool_ref, o_ref, *, tk):
    j = pl.program_id(2)
    start = j * Q_BLOCK
    rows = HPG * Q_BLOCK
    qp = qp_ref[...].reshape(rows, LANE)
    pos_t = start + lax.broadcasted_iota(jnp.int32, (Q_BLOCK, 1), 0)
    row = lax.broadcasted_iota(jnp.int32, (rows, 1), 0)

    ckp = ckp_ref[0, 0]
    n_c = ckp.shape[0]
    c_end = lax.broadcasted_iota(jnp.int32, (1, n_c), 1) * CMP_STRIDE + (CMP_LEN - 1)
    bias_c = jnp.where(c_end <= pos_t, 0.0, NEG_INF)
    any_c = start + (row & (Q_BLOCK - 1)) >= CMP_LEN - 1
    p_c = _biased_probs(_dot_nt(qp, ckp[:, :LANE]), bias_c, any_c)
    o_c = _dot(p_c.astype(BF16), ckp[:, LANE:LANE + HEAD_DIM])
    ps = p_c[0:Q_BLOCK]
    for h in range(1, HPG):
        ps = ps + p_c[h * Q_BLOCK:(h + 1) * Q_BLOCK]

    wlen = WINDOW + Q_BLOCK
    woff = pl.multiple_of(jnp.maximum(start - WINDOW, 0), Q_BLOCK)
    d_w = pos_t - (woff + lax.broadcasted_iota(jnp.int32, (1, wlen), 1))
    bias_w = jnp.where((d_w >= 0) & (d_w <= WINDOW), 0.0, NEG_INF)
    p_w = _biased_probs(_dot_nt(qp, vw_ref[0, 0, pl.ds(woff, wlen), LANE:2 * LANE]), bias_w)
    o_w = _dot(p_w.astype(BF16), vw_ref[0, 0, pl.ds(woff, wlen), HEAD_DIM:2 * HEAD_DIM])

    imp_t = _pool_blocks_t(ps, pool_ref[...])
    blk = lax.broadcasted_iota(jnp.int32, (LANE, 1), 0)
    cur = (start + lax.broadcasted_iota(jnp.int32, (1, Q_BLOCK), 1)) // SLC_BLOCK
    valid = blk <= cur
    forced = (blk == 0) | (blk == cur) | (blk == cur - 1)
    score = jnp.where(valid & ~forced, imp_t, -1.0)
    sel_t, _ = _topk_select_cols(score, blk.astype(F32), SLC_TOPK - 3)
    mb = jnp.where(((sel_t > 0.0) | forced) & valid, 0.0, NEG_INF).T.astype(BF16)
    qs = jnp.concatenate([jnp.concatenate([mb] * HPG, axis=0), qp], axis=1)

    def tile(i, state, diag):
        m, l, acc = state
        off = pl.multiple_of(i * tk, tk)
        s = _dot_nt(qs, kp_ref[0, 0, pl.ds(off, tk), :])
        if diag:
            kpos = off + lax.broadcasted_iota(jnp.int32, (1, tk), 1)
            s = _add_token_bias(s, jnp.where(kpos <= pos_t, 0.0, NEG_INF))
        m2 = jnp.maximum(m, jnp.max(s, axis=-1, keepdims=True))
        a = jnp.exp(m - m2)
        e = jnp.exp(s - m2)
        l = a * l + jnp.sum(e, axis=-1, keepdims=True)
        acc = a * acc + _dot(e.astype(BF16), vw_ref[0, 0, pl.ds(off, tk), 0:HEAD_DIM])
        return m2, l, acc

    n_full = start // tk
    carry = (jnp.full((rows, 1), NEG_INF, F32), jnp.zeros((rows, 1), F32), jnp.zeros((rows, HEAD_DIM), F32))
    carry = lax.fori_loop(0, n_full, lambda i, c: tile(i, c, False), carry)
    _, l_s, acc_s = tile(n_full, carry, True)
    o_s = acc_s / jnp.maximum(l_s, 1e-30)

    gn = gn_ref[0, 0]
    outs = []
    for h in range(HPG):
        sl = slice(h * Q_BLOCK, (h + 1) * Q_BLOCK)
        outs.append(gn[:, h:h + 1] * o_c[sl] + gn[:, HPG + h:HPG + h + 1] * o_s[sl]
                    + gn[:, 2 * HPG + h:2 * HPG + h + 1] * o_w[sl])
    o_ref[0] = jnp.concatenate(outs, axis=1)


def _gate_pack(gn, b, t):
    g = gn[:, :3 * N_HEADS].reshape(b, t, 3, N_KV, HPG).transpose(0, 3, 1, 2, 4).reshape(b, N_KV, t, 3 * HPG)
    return jnp.pad(g, ((0, 0), (0, 0), (0, 0), (0, 16 - 3 * HPG)))


def _nsa_prompt(qp, kp, vw, gn, ckv, b, t):
    n_b = t // SLC_BLOCK
    assert n_b <= LANE and t % Q_BLOCK == 0 and t >= WINDOW + Q_BLOCK
    tk = min(1024, t)
    qp = qp.reshape(N_HEADS, b, t, LANE)
    kp = kp.reshape(N_KV, b, t, 2 * LANE)
    vw = vw.reshape(N_KV, b, t, 2 * LANE)
    n_c = t // CMP_STRIDE
    c4 = ckv.reshape(b, n_c, N_KV, 2, HEAD_DIM).transpose(3, 0, 2, 1, 4)
    c_end = jnp.arange(n_c, dtype=jnp.int32) * CMP_STRIDE + (CMP_LEN - 1)
    ckp = jnp.concatenate([c4[0], jnp.broadcast_to(_key_consts(c_end), c4[0].shape), c4[1], jnp.zeros_like(c4[1])],
                          axis=-1).astype(BF16)
    pool = (jnp.arange(LANE)[:, None] == jnp.arange(n_c)[None, :] // (SLC_BLOCK // CMP_STRIDE)).astype(BF16)
    gn2 = _gate_pack(gn, b, t)
    return pl.pallas_call(
        functools.partial(_nsa_prompt_kernel, tk=tk),
        grid=(b, N_KV, t // Q_BLOCK),
        in_specs=[pl.BlockSpec((HPG, 1, Q_BLOCK, LANE), lambda i, g, j: (g, i, j, 0)),
                  pl.BlockSpec((1, 1, Q_BLOCK, 16), lambda i, g, j: (i, g, j, 0)),
                  pl.BlockSpec((1, 1, n_c, 2 * LANE), lambda i, g, j: (i, g, 0, 0)),
                  pl.BlockSpec((1, 1, t, 2 * LANE), lambda i, g, j: (g, i, 0, 0)),
                  pl.BlockSpec((1, 1, t, 2 * LANE), lambda i, g, j: (g, i, 0, 0)),
                  pl.BlockSpec((LANE, n_c), lambda i, g, j: (0, 0))],
        out_specs=pl.BlockSpec((1, Q_BLOCK, HPG * HEAD_DIM), lambda i, g, j: (i, j, g)),
        out_shape=jax.ShapeDtypeStruct((b, t, NSA_WIDTH), F32),
        compiler_params=_cparams(3),
        name="nsa_prompt",
    )(qp, gn2, ckp, kp, vw, pool).reshape(b * t, NSA_WIDTH)


def _mem_attend_kernel(q_ref, kv_ref, o_ref):
    q = q_ref[0]
    kv = kv_ref[...].reshape(N_MEM, 2 * MEM_WIDTH)
    scale = MEM_HEAD_DIM ** -0.5
    for h in range(MEM_HEADS):
        sl = slice(h * MEM_HEAD_DIM, (h + 1) * MEM_HEAD_DIM)
        k = kv[:, sl].astype(BF16)
        v = kv[:, MEM_WIDTH + h * MEM_HEAD_DIM:MEM_WIDTH + (h + 1) * MEM_HEAD_DIM].astype(BF16)
        s = _dot_nt(q[:, sl].astype(BF16), k) * scale
        e = jnp.exp(s - jnp.max(s, axis=-1, keepdims=True))
        p = e / jnp.sum(e, axis=-1, keepdims=True)
        o_ref[0, :, sl] = _dot(p.astype(BF16), v)


def _mem_attend(mq, mkv, layer=None):
    b, t, _ = mq.shape
    tm = _row_tile(t, 512)
    if layer is None:
        kv_spec = pl.BlockSpec((1, N_MEM, 2 * MEM_WIDTH), lambda i, j: (i, 0, 0))
    else:
        kv_spec = pl.BlockSpec((1, 1, N_MEM, 2 * MEM_WIDTH), lambda i, j: (layer, i, 0, 0))
    return pl.pallas_call(
        _mem_attend_kernel,
        grid=(b, t // tm),
        in_specs=[pl.BlockSpec((1, tm, MEM_WIDTH), lambda i, j: (i, j, 0)), kv_spec],
        out_specs=pl.BlockSpec((1, tm, MEM_WIDTH), lambda i, j: (i, j, 0)),
        out_shape=jax.ShapeDtypeStruct((b, t, MEM_WIDTH), F32),
        compiler_params=_cparams(2),
        name="mem_attend",
    )(mq, mkv)


def _ssm_disc_kernel(ar_ref, ai_ref, ldt_ref, br_ref, bi_ref, abr_ref, abi_ref, bbr_ref, bbi_ref):
    ar = ar_ref[...]
    ai = ai_ref[...]
    dt = jnp.exp(ldt_ref[...])
    mag = jnp.exp(dt * ar)
    abr = mag * jnp.cos(dt * ai)
    abi = mag * jnp.sin(dt * ai)
    den = ar * ar + ai * ai
    fr = ((abr - 1.0) * ar + abi * ai) / den
    fi = (abi * ar - (abr - 1.0) * ai) / den
    br = br_ref[...]
    bi = bi_ref[...]
    abr_ref[...] = abr
    abi_ref[...] = abi
    bbr_ref[...] = fr * br - fi * bi
    bbi_ref[...] = fr * bi + fi * br


def _ssm_discretise(a_re, a_im, log_dt, b_re, b_im):
    rep = lambda x: jnp.repeat(x, SSM_GROUP, axis=0)
    tb = lambda x: x.transpose(0, 2, 1).reshape(SSM_WIDTH, SSM_STATE)
    shp = jax.ShapeDtypeStruct((SSM_WIDTH, SSM_STATE), F32)
    abr, abi, bbr, bbi = pl.pallas_call(
        _ssm_disc_kernel, out_shape=[shp] * 4, name="ssm_disc",
    )(rep(a_re), rep(a_im), rep(jnp.broadcast_to(log_dt[:, None], a_re.shape)), tb(b_re), tb(b_im))
    eye = jnp.eye(SSM_GPB, dtype=F32)

    def in_map(bb):
        bb = bb.reshape(SSM_SLABS, SSM_GPB, SSM_GROUP, SSM_STATE)
        return jnp.einsum('jgcp,gG->jgcGp', bb, eye).reshape(SSM_SLABS, LANE, SSM_GPB * SSM_STATE).astype(BF16)

    flat = lambda x: x[::SSM_GROUP].reshape(1, SSM_FLAT)
    return flat(abr), flat(abi), in_map(bbr), in_map(bbi)


def _ssm_out_map(c):
    eye = jnp.eye(SSM_GPB, dtype=F32)
    c = c.reshape(SSM_SLABS, SSM_GPB, SSM_GROUP, SSM_STATE)
    return jnp.einsum('jgcp,gG->jgpGc', c, eye).reshape(SSM_SLABS, SSM_GPB * SSM_STATE, LANE).astype(BF16)


def _ssm_kernel(u_ref, h0r_ref, h0i_ref, abr_ref, abi_ref, bdr_ref, bdi_ref, cdr_ref, cdi_ref, d_ref,
                wg_ref, bg_ref, o_ref, hr_ref, hi_ref, sr_ref, si_ref, st_ref, *, n_scan):
    c_idx = pl.program_id(1)
    slab = SSM_GPB * SSM_STATE

    @pl.when(c_idx == 0)
    def _():
        st_ref[0:1, :] = h0r_ref[0]
        st_ref[1:2, :] = h0i_ref[0]

    u = u_ref[0]
    ub = u.astype(BF16)
    for j in range(SSM_SLABS):
        uj = ub[:, j * LANE:(j + 1) * LANE]
        sr_ref[:, j * slab:(j + 1) * slab] = _dot(uj, bdr_ref[j])
        si_ref[:, j * slab:(j + 1) * slab] = _dot(uj, bdi_ref[j])

    ar = abr_ref[...]
    ai = abi_ref[...]

    def step(t, carry):
        hr, hi = carry
        nr = ar * hr - ai * hi + sr_ref[pl.ds(t, 1), :]
        ni = ar * hi + ai * hr + si_ref[pl.ds(t, 1), :]
        sr_ref[pl.ds(t, 1), :] = nr
        si_ref[pl.ds(t, 1), :] = ni
        return nr, ni

    hr, hi = lax.fori_loop(0, n_scan, step, (st_ref[0:1, :], st_ref[1:2, :]))
    st_ref[0:1, :] = hr
    st_ref[1:2, :] = hi
    hr_ref[0] = hr
    hi_ref[0] = hi

    ys = []
    for j in range(SSM_SLABS):
        hrj = sr_ref[:, j * slab:(j + 1) * slab].astype(BF16)
        hij = si_ref[:, j * slab:(j + 1) * slab].astype(BF16)
        ys.append(_dot(hrj, cdr_ref[j]) - _dot(hij, cdi_ref[j]))
    y = jnp.concatenate(ys, axis=1) + d_ref[...] * u
    y = jax.nn.gelu(y)
    o_ref[0] = y * jax.nn.sigmoid(_dot(y.astype(BF16), wg_ref[...]) + bg_ref[...])


def _ssm(u, h0r, h0i, disc, cdr, cdi, d, w_glu, b_glu, t_valid=None):
    b, t, _ = u.shape
    tc = _row_tile(t, 256)
    n_scan = tc if t_valid is None else t_valid
    assert t_valid is None or t == tc
    abr, abi, bdr, bdi = disc
    full = lambda a: pl.BlockSpec(a.shape, lambda i, c: (0,) * a.ndim)
    state_spec = pl.BlockSpec((1, 1, SSM_FLAT), lambda i, c: (i, 0, 0))
    args = (abr, abi, bdr, bdi, cdr, cdi, d.reshape(1, SSM_WIDTH), w_glu.astype(BF16), b_glu.reshape(1, SSM_WIDTH))
    o, hr, hi = pl.pallas_call(
        functools.partial(_ssm_kernel, n_scan=n_scan),
        grid=(b, t // tc),
        in_specs=[pl.BlockSpec((1, tc, SSM_WIDTH), lambda i, c: (i, c, 0)), state_spec, state_spec]
                 + [full(a) for a in args],
        out_specs=[pl.BlockSpec((1, tc, SSM_WIDTH), lambda i, c: (i, c, 0)), state_spec, state_spec],
        out_shape=[jax.ShapeDtypeStruct((b, t, SSM_WIDTH), F32),
                   jax.ShapeDtypeStruct((b, 1, SSM_FLAT), F32),
                   jax.ShapeDtypeStruct((b, 1, SSM_FLAT), F32)],
        scratch_shapes=[pltpu.VMEM((tc, SSM_FLAT), F32), pltpu.VMEM((tc, SSM_FLAT), F32),
                        pltpu.VMEM((8, SSM_FLAT), F32)],
        compiler_params=_cparams(2),
        name="ssm",
    )(u, h0r.reshape(b, 1, SSM_FLAT), h0i.reshape(b, 1, SSM_FLAT), *args)
    shp = (b, N_SSM_GROUPS, SSM_STATE)
    return o, hr.reshape(shp), hi.reshape(shp)


def _merge_kernel(x_ref, oa_ref, ob_ref, oc_ref, gb_ref, w_ref, o_ref):
    acc = x_ref[...]
    for i, r in enumerate((oa_ref, ob_ref, oc_ref)):
        w = r.shape[1]
        off = sum((NSA_WIDTH, SSM_WIDTH, MEM_WIDTH)[:i])
        mixed = (r[...] * gb_ref[:, off:off + w]).astype(BF16)
        acc = acc + _dot(mixed, w_ref[off:off + w, :])
    o_ref[...] = acc


def _merge(x2, o_a, o_b, o_c, gb, w_o_bf16):
    n = x2.shape[0]
    tm = _row_tile(n, 512)
    row = lambda w: pl.BlockSpec((tm, w), lambda i: (i, 0))
    return pl.pallas_call(
        _merge_kernel,
        grid=(n // tm,),
        in_specs=[row(D_MODEL), row(NSA_WIDTH), row(SSM_WIDTH), row(MEM_WIDTH), row(MIX_WIDTH),
                  pl.BlockSpec((MIX_WIDTH, D_MODEL), lambda i: (0, 0))],
        out_specs=row(D_MODEL),
        out_shape=jax.ShapeDtypeStruct((n, D_MODEL), F32),
        compiler_params=_cparams(1),
        name="merge",
    )(x2, o_a, o_b, o_c, gb, w_o_bf16)


def _ffn_kernel(x_ref, g_ref, w1_ref, w2_ref, fg_ref, o_ref, h_ref, acc_ref, *, final_norm):
    k = pl.program_id(1)

    @pl.when(k == 0)
    def _():
        h_ref[...] = _rms(x_ref[...], g_ref[...]).astype(BF16)
        acc_ref[...] = jnp.zeros_like(acc_ref)

    a = jnp.square(jnp.maximum(_dot(h_ref[...], w1_ref[...]), 0.0))
    acc_ref[...] += _dot(a.astype(BF16), w2_ref[...])

    @pl.when(k == pl.num_programs(1) - 1)
    def _():
        y = x_ref[...] + acc_ref[...]
        if final_norm:
            y = _rms(y, fg_ref[...])
        o_ref[...] = y


def _ffn(x2, g, w1_bf16, w2_bf16, final_g, final_norm):
    n = x2.shape[0]
    tm = _row_tile(n, 1024)
    tf = 1024
    return pl.pallas_call(
        functools.partial(_ffn_kernel, final_norm=final_norm),
        grid=(n // tm, D_FF // tf),
        in_specs=[pl.BlockSpec((tm, D_MODEL), lambda i, k: (i, 0)),
                  pl.BlockSpec((1, D_MODEL), lambda i, k: (0, 0)),
                  pl.BlockSpec((D_MODEL, tf), lambda i, k: (0, k)),
                  pl.BlockSpec((tf, D_MODEL), lambda i, k: (k, 0)),
                  pl.BlockSpec((1, D_MODEL), lambda i, k: (0, 0))],
        out_specs=pl.BlockSpec((tm, D_MODEL), lambda i, k: (i, 0)),
        out_shape=jax.ShapeDtypeStruct((n, D_MODEL), F32),
        scratch_shapes=[pltpu.VMEM((tm, D_MODEL), BF16), pltpu.VMEM((tm, D_MODEL), F32)],
        compiler_params=_cparams(2),
        name="ffn",
    )(x2, g.reshape(1, D_MODEL), w1_bf16, w2_bf16, final_g.reshape(1, D_MODEL))


def _layer_params(l, norm1_g, w_in, cmp_pe, cmp_w1, cmp_w2, ssm_a_re, ssm_a_im, ssm_log_dt, ssm_b_re, ssm_b_im,
                  ssm_c_re, ssm_c_im, ssm_d, w_glu, b_glu, mem_norm_g, w_mem_kv, w_o, norm2_g, w_ff1, w_ff2):
    return dict(
        norm1_g=norm1_g[l], w_in=_pack_w_in(w_in[l]),
        pe=_flat_cmp_pe(cmp_pe[l]), w1=_expand_cmp_w1(cmp_w1[l]), w2=_expand_cmp_w2(cmp_w2[l]),
        disc=_ssm_discretise(ssm_a_re[l], ssm_a_im[l], ssm_log_dt[l], ssm_b_re[l], ssm_b_im[l]),
        cdr=_ssm_out_map(ssm_c_re[l]), cdi=_ssm_out_map(ssm_c_im[l]), d=ssm_d[l],
        w_glu=w_glu[l], b_glu=b_glu[l], mem_norm_g=mem_norm_g[l], w_mem_kv=w_mem_kv[l].astype(BF16),
        w_o=w_o[l].astype(BF16), norm2_g=norm2_g[l], w_ff1=w_ff1[l].astype(BF16), w_ff2=w_ff2[l].astype(BF16))


def _prompt_layer(x2, mem2, p, b, t, final_g, final_norm):
    kvc, kvs, kvw, u, mq, gb, gn, qp, kp, vw = _in_proj(x2, p['norm1_g'], p['w_in'], seq_len=t)
    lh = _cmp_lh(kvc.reshape(b * t // CMP_STRIDE, CMP_ROW), p['pe'], p['w1'])
    ckv = _cmp_tokens_prompt(lh.reshape(b, t // CMP_STRIDE, 4 * KV_WIDTH), p['w2'])
    o_a = _nsa_prompt(qp, kp, vw, gn, ckv, b, t)
    zero = jnp.zeros((b, N_SSM_GROUPS, SSM_STATE), F32)
    o_b, hr, hi = _ssm(u.reshape(b, t, SSM_WIDTH), zero, zero, p['disc'], p['cdr'], p['cdi'], p['d'],
                       p['w_glu'], p['b_glu'])
    mkv = _norm_mm(mem2, p['mem_norm_g'], p['w_mem_kv'])
    o_c = _mem_attend(mq.reshape(b, t, MEM_WIDTH), mkv.reshape(b, N_MEM, 2 * MEM_WIDTH))
    x2 = _merge(x2, o_a, o_b.reshape(b * t, SSM_WIDTH), o_c.reshape(b * t, MEM_WIDTH), gb, p['w_o'])
    x2 = _ffn(x2, p['norm2_g'], p['w_ff1'], p['w_ff2'], final_g, final_norm)
    kv_shape = (b, t, 2, N_KV, HEAD_DIM)
    wl = min(WINDOW, t)
    return x2, dict(cmp=kvc.reshape(kv_shape), slc=kvs.reshape(kv_shape),
                    win=kvw.reshape(kv_shape)[:, t - wl:], hr=hr, hi=hi,
                    mkv=mkv.reshape(b, N_MEM, 2, MEM_HEADS, MEM_HEAD_DIM))


CHUNKS_PER_PAGE = PAGE_SIZE // CMP_STRIDE


def _cmp_lh_paged_kernel(pt_ref, cache_ref, pe_ref, w_ref, o_ref, buf_ref, stage_ref, xl_ref, sem, *,
                         pages, layer, n_s, n_groups):
    step = pl.program_id(0)
    slot = step % 2

    def page_copy(grp, i, sl):
        page = pt_ref[grp // n_s, (grp % n_s) * pages + i]
        return pltpu.make_async_copy(cache_ref.at[layer, page], buf_ref.at[sl, i], sem.at[sl, i])

    @pl.when(step == 0)
    def _():
        for i in range(pages):
            page_copy(step, i, slot).start()

    @pl.when(step + 1 < n_groups)
    def _():
        for i in range(pages):
            page_copy(step + 1, i, 1 - slot).start()

    for i in range(pages):
        page_copy(step, i, slot).wait()

    n_stage = stage_ref.shape[0]

    def relayout(it, carry):
        for u in range(n_stage):
            i = it * n_stage + u
            xt = buf_ref[slot, i].reshape(2 * KV_WIDTH, PAGE_SIZE).T
            base = pl.multiple_of(i * CHUNKS_PER_PAGE, CHUNKS_PER_PAGE)
            for h in range(2 * KV_WIDTH // LANE):
                stage_ref[u, h] = xt[:, h * LANE:(h + 1) * LANE]
                for l in range(CMP_STRIDE):
                    xl_ref[l, pl.ds(base, CHUNKS_PER_PAGE), h * LANE:(h + 1) * LANE] = (
                        stage_ref[u, h, pl.ds(l, CHUNKS_PER_PAGE, stride=CMP_STRIDE), :])
        return carry

    lax.fori_loop(0, pages // n_stage, relayout, 0)
    lo = jnp.zeros((pages * CHUNKS_PER_PAGE, 2 * KV_WIDTH), F32)
    hi = jnp.zeros((pages * CHUNKS_PER_PAGE, 2 * KV_WIDTH), F32)
    for l in range(CMP_STRIDE):
        x = xl_ref[l]
        lo = lo + _dot((x + pe_ref[0, l]).astype(BF16), w_ref[0, l])
        hi = hi + _dot((x + pe_ref[1, l]).astype(BF16), w_ref[1, l])
    o_ref[0] = jnp.concatenate([lo, hi], axis=1)


def _cmp_lh_paged(cache_t, layer, page_table, pe_flat, w1_big):
    b, n_pages = page_table.shape
    pages = math.gcd(n_pages, 32)
    rows = pages * CHUNKS_PER_PAGE
    n_s = n_pages // pages
    n_groups = b * n_s
    page_shape = cache_t.shape[2:]

    grid_spec = pltpu.PrefetchScalarGridSpec(
        num_scalar_prefetch=1,
        grid=(n_groups,),
        in_specs=[pl.BlockSpec(memory_space=pl.ANY),
                  pl.BlockSpec((2, CMP_STRIDE, 1, 2 * KV_WIDTH), lambda t, pt: (0, 0, 0, 0)),
                  pl.BlockSpec((2, CMP_STRIDE, 2 * KV_WIDTH, 2 * KV_WIDTH), lambda t, pt: (0, 0, 0, 0))],
        out_specs=pl.BlockSpec((1, rows, 4 * KV_WIDTH), lambda t, pt: (t // n_s, t % n_s, 0)),
        scratch_shapes=[pltpu.VMEM((2, pages) + page_shape, F32),
                        pltpu.VMEM((math.gcd(pages, 8), 2 * KV_WIDTH // LANE, PAGE_SIZE, LANE), F32),
                        pltpu.VMEM((CMP_STRIDE, rows, 2 * KV_WIDTH), F32),
                        pltpu.SemaphoreType.DMA((2, pages))])
    return pl.pallas_call(
        functools.partial(_cmp_lh_paged_kernel, pages=pages, layer=layer, n_s=n_s, n_groups=n_groups),
        grid_spec=grid_spec,
        out_shape=jax.ShapeDtypeStruct((b, n_pages * CHUNKS_PER_PAGE, 4 * KV_WIDTH), F32),
        compiler_params=_cparams(1),
        name="cmp_lh_paged",
    )(page_table, cache_t, pe_flat.reshape(2, CMP_STRIDE, 1, 2 * KV_WIDTH), w1_big)


TOK_PAD = 8


def _nsa_sample_select_kernel(lhc_ref, lht_ref, w2_ref, q_ref, pool_ref, oc_ref, idx_ref, lh_ref, *, past, n_tok):
    n_cache = lhc_ref.shape[1]
    n_tail = lht_ref.shape[1]
    r = lh_ref.shape[0]
    lh_ref[0:n_cache, :] = lhc_ref[0]
    lh_ref[n_cache:n_cache + n_tail, :] = lht_ref[0]
    lh_ref[n_cache + n_tail:r, :] = jnp.zeros((r - n_cache - n_tail, lh_ref.shape[1]), F32)
    ckv = _cmp_combine(lh_ref[...], w2_ref[...])

    rows = HPG * TOK_PAD
    row = lax.broadcasted_iota(jnp.int32, (rows, 1), 0)
    pos_q = past + (row & (TOK_PAD - 1))
    ci = lax.broadcasted_iota(jnp.int32, (1, r), 1)
    d_c = pos_q - (ci * CMP_STRIDE + (CMP_LEN - 1))
    n_blk = pool_ref.shape[0]
    ps_all = []
    for g in range(N_KV):
        qg = q_ref[0, g]
        q = qg[:, :HEAD_DIM]
        slope = qg[:, HEAD_DIM:HEAD_DIM + 1].astype(F32)
        ck = ckv[:, g * 2 * HEAD_DIM:g * 2 * HEAD_DIM + HEAD_DIM].astype(BF16)
        cv = ckv[:, g * 2 * HEAD_DIM + HEAD_DIM:(g + 1) * 2 * HEAD_DIM].astype(BF16)
        s = _dot_nt(q, ck) - slope * d_c.astype(F32)
        p_c = _masked_probs(s, d_c >= 0)
        oc_ref[0, g] = _dot(p_c.astype(BF16), cv)
        ps = p_c[0:TOK_PAD]
        for h in range(1, HPG):
            ps = ps + p_c[h * TOK_PAD:(h + 1) * TOK_PAD]
        ps_all.append(ps)
    ps = jnp.concatenate(ps_all + [jnp.zeros((LANE - N_KV * TOK_PAD, r), F32)], axis=0)
    imp_t = _pool_blocks_t(ps, pool_ref[...])
    blk = lax.broadcasted_iota(jnp.int32, (n_blk, 1), 0)
    cur = (past + (lax.broadcasted_iota(jnp.int32, (1, LANE), 1) & (TOK_PAD - 1))) // SLC_BLOCK
    valid = blk <= cur
    forced = (blk == 0) | (blk == cur) | (blk == cur - 1)
    score = jnp.where(valid, jnp.where(forced, FORCED_SCORE, imp_t), -1.0)
    _, picks = _topk_select_cols(score, blk.astype(F32), SLC_TOPK)
    for k, pk in enumerate(picks):
        idx_ref[0, k:k + 1, :] = pk.astype(jnp.int32)


def _nsa_sample_select(lh_cache, lh_tail, w2_big, qsel, past, n_tok):
    b, n_cache, _ = lh_cache.shape
    n_tail = lh_tail.shape[1]
    r = -(-(n_cache + n_tail) // LANE) * LANE
    n_b = past // SLC_BLOCK + 1
    assert n_b >= SLC_TOPK
    n_blk = -(-n_b // 8) * 8
    pool = (jnp.arange(n_blk)[:, None] == jnp.arange(r)[None, :] // (SLC_BLOCK // CMP_STRIDE)).astype(BF16)
    rows = HPG * TOK_PAD
    o_c, idx = pl.pallas_call(
        functools.partial(_nsa_sample_select_kernel, past=past, n_tok=n_tok),
        grid=(b,),
        in_specs=[pl.BlockSpec((1, n_cache, 4 * KV_WIDTH), lambda i: (i, 0, 0)),
                  pl.BlockSpec((1, n_tail, 4 * KV_WIDTH), lambda i: (i, 0, 0)),
                  pl.BlockSpec((2 * KV_WIDTH, 2 * KV_WIDTH), lambda i: (0, 0)),
                  pl.BlockSpec((1, N_KV, rows, LANE), lambda i: (i, 0, 0, 0)),
                  pl.BlockSpec((n_blk, r), lambda i: (0, 0))],
        out_specs=[pl.BlockSpec((1, N_KV, rows, HEAD_DIM), lambda i: (i, 0, 0, 0)),
                   pl.BlockSpec((1, SLC_TOPK, LANE), lambda i: (i, 0, 0))],
        out_shape=[jax.ShapeDtypeStruct((b, N_KV, rows, HEAD_DIM), F32),
                   jax.ShapeDtypeStruct((b, SLC_TOPK, LANE), jnp.int32)],
        scratch_shapes=[pltpu.VMEM((r, 4 * KV_WIDTH), F32)],
        compiler_params=_cparams(1),
        name="nsa_sample_select",
    )(lh_cache, lh_tail, w2_big, qsel, pool)
    idx = idx[:, :, :N_KV * TOK_PAD].reshape(b, SLC_TOPK, N_KV, TOK_PAD)[:, :, :, :n_tok]
    return o_c, idx.transpose(0, 2, 3, 1).reshape(-1)


def _rows_minor(cache):
    n = cache.ndim
    return cache.transpose(tuple(range(n - 4)) + (n - 3, n - 2, n - 1, n - 4))


def _nsa_sample_attend_kernel(idx_ref, pt_ref, cache_ref, q_ref, oc_ref, gate_ref, snew_ref, win_ref, wnew_ref,
                              o_ref, kvbuf_ref, sem, *, layer, past, n_tok):
    b = pl.program_id(0)
    n_pages = past // PAGE_SIZE
    cur_blk = past // SLC_BLOCK
    per_page = PAGE_SIZE // SLC_BLOCK
    n_sel = SLC_TOPK * PAGE_SIZE

    half = b % 2

    def sel_block(g, t, k, bb=b):
        return idx_ref[((bb * N_KV + g) * n_tok + t) * SLC_TOPK + k]

    def page_copies(bb, hf, g, t, k):
        c = jnp.minimum(sel_block(g, t, k, bb), cur_blk - 1)
        page = pt_ref[bb * n_pages + c // per_page]
        slot = g * n_tok + t
        dst = pl.ds(k * PAGE_SIZE, PAGE_SIZE)
        return tuple(pltpu.make_async_copy(cache_ref.at[layer, page, kv, g], kvbuf_ref.at[hf, slot, kv, :, dst],
                                           sem.at[hf, kv, slot]) for kv in range(2))

    def start_all(bb, hf):
        for g in range(N_KV):
            for t in range(n_tok):
                for k in range(SLC_TOPK):
                    for cp in page_copies(bb, hf, g, t, k):
                        cp.start()

    @pl.when(b == 0)
    def _():
        start_all(b, half)

    @pl.when(b + 1 < pl.num_programs(0))
    def _():
        start_all(b + 1, 1 - half)

    rows = HPG * TOK_PAD
    row = lax.broadcasted_iota(jnp.int32, (rows, 1), 0)
    tok = row & (TOK_PAD - 1)
    pos_q = past + tok
    lane = lax.broadcasted_iota(jnp.int32, (1, n_sel), 1)
    lane_slot = lane // PAGE_SIZE
    lane_blk = (lane & (PAGE_SIZE - 1)) // SLC_BLOCK
    r_new = lax.broadcasted_iota(jnp.int32, (1, TOK_PAD), 1)
    d_new = pos_q - (past + r_new)
    mask_new = (d_new >= 0) & (r_new < n_tok)
    wb = win_ref.shape[-1]
    d_win = pos_q - (past - wb + lax.broadcasted_iota(jnp.int32, (1, wb), 1))
    mask_win = (d_win >= 0) & (d_win <= WINDOW)

    def joint_attend(q, slope, kt_old, vt_old, d_old, mask_old, k_new, v_new):
        s_o = jnp.where(mask_old, _dot(q, kt_old) - slope * d_old.astype(F32), NEG_INF)
        s_n = jnp.where(mask_new, _dot_nt(q, k_new) - slope * d_new.astype(F32), NEG_INF)
        m = jnp.maximum(jnp.max(s_o, axis=-1, keepdims=True), jnp.max(s_n, axis=-1, keepdims=True))
        e_o = jnp.where(mask_old, jnp.exp(s_o - m), 0.0)
        e_n = jnp.where(mask_new, jnp.exp(s_n - m), 0.0)
        l = jnp.sum(e_o, axis=-1, keepdims=True) + jnp.sum(e_n, axis=-1, keepdims=True)
        return (_dot_nt(e_o.astype(BF16), vt_old) + _dot(e_n.astype(BF16), v_new)) / jnp.maximum(l, 1e-30)

    for g in range(N_KV):
        ksl = slice(g * HEAD_DIM, (g + 1) * HEAD_DIM)
        vsl = slice(KV_WIDTH + g * HEAD_DIM, KV_WIDTH + (g + 1) * HEAD_DIM)
        qg = q_ref[0, g]
        q = qg[:, :HEAD_DIM]
        slope = qg[:, HEAD_DIM:HEAD_DIM + 1].astype(F32)
        k_snew = snew_ref[0][:, ksl].astype(BF16)
        v_snew = snew_ref[0][:, vsl].astype(BF16)
        o_s = jnp.zeros((rows, HEAD_DIM), F32)
        for t in range(n_tok):
            for k in range(SLC_TOPK):
                for cp in page_copies(b, half, g, t, k):
                    cp.wait()
            blk_of_lane = jnp.zeros((1, n_sel), jnp.int32)
            for k in range(SLC_TOPK):
                blk_of_lane = jnp.where(lane_slot == k, sel_block(g, t, k), blk_of_lane)
            d_sel = pos_q - (blk_of_lane * SLC_BLOCK + (lane & (SLC_BLOCK - 1)))
            mask_sel = (d_sel >= 0) & (blk_of_lane < cur_blk) & (blk_of_lane % per_page == lane_blk)
            o_t = joint_attend(q, slope, kvbuf_ref[half, g * n_tok + t, 0].astype(BF16),
                               kvbuf_ref[half, g * n_tok + t, 1].astype(BF16), d_sel, mask_sel, k_snew, v_snew)
            o_s = jnp.where(tok == t, o_t, o_s)
        o_w = joint_attend(q, slope, win_ref[0, 0, 0, g].astype(BF16), win_ref[0, 0, 1, g].astype(BF16), d_win,
                           mask_win, wnew_ref[0][:, ksl].astype(BF16), wnew_ref[0][:, vsl].astype(BF16))
        gate = gate_ref[0, g]
        o_ref[0, g] = gate[:, 0:1] * oc_ref[0, g] + gate[:, 1:2] * o_s + gate[:, 2:3] * o_w


def _nsa_sample_attend(idx, page_table, cache_slc_t, cache_win_t, layer, qsel, o_c, gates, slc_new, win_new, past, n_tok):
    b = qsel.shape[0]
    assert past % PAGE_SIZE == 0 and n_tok <= TOK_PAD
    rows = HPG * TOK_PAD
    wb = cache_win_t.shape[-1]
    per_b = lambda shape: pl.BlockSpec((1,) + shape, lambda i, *_: (i,) + (0,) * len(shape))
    grid_spec = pltpu.PrefetchScalarGridSpec(
        num_scalar_prefetch=2,
        grid=(b,),
        in_specs=[pl.BlockSpec(memory_space=pl.ANY),
                  per_b((N_KV, rows, LANE)), per_b((N_KV, rows, HEAD_DIM)), per_b((N_KV, rows, 16)),
                  per_b((TOK_PAD, 2 * KV_WIDTH)),
                  pl.BlockSpec((1, 1, 2, N_KV, HEAD_DIM, wb), lambda i, *_: (layer, i, 0, 0, 0, 0)),
                  per_b((TOK_PAD, 2 * KV_WIDTH))],
        out_specs=per_b((N_KV, rows, HEAD_DIM)),
        scratch_shapes=[pltpu.VMEM((2, N_KV * n_tok, 2, HEAD_DIM, SLC_TOPK * PAGE_SIZE), F32),
                        pltpu.SemaphoreType.DMA((2, 2, N_KV * n_tok))])
    return pl.pallas_call(
        functools.partial(_nsa_sample_attend_kernel, layer=layer, past=past, n_tok=n_tok),
        grid_spec=grid_spec,
        out_shape=jax.ShapeDtypeStruct((b, N_KV, rows, HEAD_DIM), F32),
        compiler_params=_cparams(1),
        name="nsa_sample_attend",
    )(idx, page_table.reshape(-1), cache_slc_t, qsel, o_c, gates, slc_new, cache_win_t, win_new)


def _pad_tokens(x, axis):
    pad = [(0, 0)] * x.ndim
    pad[axis] = (0, TOK_PAD - x.shape[axis])
    return jnp.pad(x, pad)


def _sample_layer(x2, p, b, s, layer, cache_cmp_t, cache_slc_t, cache_win_t, cache_win, h0r, h0i, cache_mem,
                  page_table, final_g, final_norm):
    past = page_table.shape[1] * PAGE_SIZE
    assert past % SLC_BLOCK == 0 and s <= CMP_STRIDE
    q, kvc, kvs, kvw, u, mq, gb, gn = _in_proj(x2, p['norm1_g'], p['w_in'])
    lh_cache = _cmp_lh_paged(cache_cmp_t, layer, page_table, p['pe'], p['w1'])
    tail = jnp.pad(kvc.reshape(b, s, 2 * KV_WIDTH), ((0, 0), (0, CHUNKS_PER_PAGE * CMP_STRIDE - s), (0, 0)))
    lh_tail = _cmp_lh(tail.reshape(b * CHUNKS_PER_PAGE, CMP_ROW), p['pe'], p['w1'])
    pos = past + jnp.arange(s, dtype=jnp.int32)
    qsel = _pad_tokens(_query_pack(q.reshape(b, s, NSA_WIDTH), pos), 2).reshape(b, N_KV, HPG * TOK_PAD, LANE)
    o_c, idx = _nsa_sample_select(lh_cache, lh_tail.reshape(b, CHUNKS_PER_PAGE, 4 * KV_WIDTH), p['w2'], qsel, past, s)
    gates = gn[:, :3 * N_HEADS].reshape(b, s, 3, N_KV, HPG).transpose(0, 3, 4, 1, 2)
    gates = jnp.pad(_pad_tokens(gates, 3), ((0, 0),) * 4 + ((0, 16 - 3),)).reshape(b, N_KV, HPG * TOK_PAD, 16)
    slc_new = kvs.reshape(b, s, 2 * KV_WIDTH)
    win_new = kvw.reshape(b, s, 2 * KV_WIDTH)
    o = _nsa_sample_attend(idx, page_table, cache_slc_t, cache_win_t, layer, qsel, o_c, gates,
                           _pad_tokens(slc_new, 1), _pad_tokens(win_new, 1), past, s)
    o_a = o.reshape(b, N_KV, HPG, TOK_PAD, HEAD_DIM)[:, :, :, :s].transpose(0, 3, 1, 2, 4).reshape(b * s, NSA_WIDTH)
    o_b, hr, hi = _ssm(_pad_tokens(u.reshape(b, s, SSM_WIDTH), 1), h0r, h0i, p['disc'], p['cdr'], p['cdi'], p['d'],
                       p['w_glu'], p['b_glu'], t_valid=s)
    o_c2 = _mem_attend(_pad_tokens(mq.reshape(b, s, MEM_WIDTH), 1), cache_mem, layer)
    x2 = _merge(x2, o_a, o_b[:, :s].reshape(b * s, SSM_WIDTH), o_c2[:, :s].reshape(b * s, MEM_WIDTH), gb, p['w_o'])
    x2 = _ffn(x2, p['norm2_g'], p['w_ff1'], p['w_ff2'], final_g, final_norm)
    kv_shape = (b, s, 2, N_KV, HEAD_DIM)
    new_win = jnp.concatenate([cache_win, kvw.reshape(kv_shape).astype(cache_win.dtype)], axis=1)[:, s:]
    return x2, dict(cmp=kvc.reshape(kv_shape), slc=kvs.reshape(kv_shape), win=new_win, hr=hr, hi=hi)


def kernel(x_prompt, x_sample, cache_cmp_kv, cache_slc_kv, cache_win_kv, state_ssm_re, state_ssm_im, cache_mem_kv,
           page_table, mem_prompt, norm1_g, w_in, cmp_pe, cmp_w1, cmp_w2, ssm_a_re, ssm_a_im, ssm_log_dt, ssm_b_re,
           ssm_b_im, ssm_c_re, ssm_c_im, ssm_d, w_glu, b_glu, mem_norm_g, w_mem_kv, w_o, norm2_g, w_ff1, w_ff2,
           final_norm_g):
    bp, t, _ = x_prompt.shape
    bs, s, _ = x_sample.shape
    depth = w_in.shape[0]
    xp = x_prompt.reshape(bp * t, D_MODEL)
    xs = x_sample.reshape(bs * s, D_MODEL)
    mem2 = mem_prompt.reshape(bp * N_MEM, D_MODEL)
    cache_cmp_t = _rows_minor(cache_cmp_kv)
    cache_slc_t = _rows_minor(cache_slc_kv)
    cache_win_t = _rows_minor(cache_win_kv)
    cache_mem = cache_mem_kv.reshape(depth, bs, N_MEM, 2 * MEM_WIDTH)
    aux_p, aux_s = [], []
    for l in range(depth):
        p = _layer_params(l, norm1_g, w_in, cmp_pe, cmp_w1, cmp_w2, ssm_a_re, ssm_a_im, ssm_log_dt, ssm_b_re,
                          ssm_b_im, ssm_c_re, ssm_c_im, ssm_d, w_glu, b_glu, mem_norm_g, w_mem_kv, w_o, norm2_g,
                          w_ff1, w_ff2)
        last = l == depth - 1
        xp, ap = _prompt_layer(xp, mem2, p, bp, t, final_norm_g, last)
        xs, a_s = _sample_layer(xs, p, bs, s, l, cache_cmp_t, cache_slc_t, cache_win_t, cache_win_kv[l],
                                state_ssm_re[l], state_ssm_im[l], cache_mem, page_table, final_norm_g, last)
        aux_p.append(ap)
        aux_s.append(a_s)
    stack = lambda aux, key: jnp.stack([a[key] for a in aux])
    return (xp.reshape(bp, t, D_MODEL), xs.reshape(bs, s, D_MODEL),
            stack(aux_p, 'cmp'), stack(aux_s, 'cmp'), stack(aux_p, 'slc'), stack(aux_s, 'slc'),
            stack(aux_p, 'win'), stack(aux_s, 'win'), stack(aux_p, 'hr'), stack(aux_p, 'hi'),
            stack(aux_s, 'hr'), stack(aux_s, 'hi'), stack(aux_p, 'mkv'))
```

```python
import functools
import math

import jax
import jax.numpy as jnp
from jax import lax
from jax.experimental import pallas as pl
from jax.experimental.pallas import tpu as pltpu

F32 = jnp.float32
BF16 = jnp.bfloat16

D_MODEL = 1024
PAGE_SIZE = 128
N_HEADS = 8
HEAD_DIM = 64
N_KV = 2
HPG = N_HEADS // N_KV
CMP_STRIDE = 16
CMP_LEN = 2 * CMP_STRIDE
SLC_BLOCK = 64
SLC_TOPK = 16
WINDOW = 512
Q_BLOCK = 128
SSM_WIDTH = D_MODEL // 2
SSM_GROUP = 16
N_SSM_GROUPS = SSM_WIDTH // SSM_GROUP
SSM_STATE = 64
SSM_FLAT = N_SSM_GROUPS * SSM_STATE
N_MEM = 256
MEM_HEADS = 4
MEM_HEAD_DIM = 128
MEM_WIDTH = MEM_HEADS * MEM_HEAD_DIM
D_FF = 4 * D_MODEL
NSA_WIDTH = N_HEADS * HEAD_DIM
KV_WIDTH = N_KV * HEAD_DIM
MIX_WIDTH = NSA_WIDTH + SSM_WIDTH + MEM_WIDTH
OFF_KV = NSA_WIDTH
OFF_GN = OFF_KV + 6 * KV_WIDTH
OFF_U = OFF_GN + 3 * N_HEADS
OFF_MQ = OFF_U + SSM_WIDTH
OFF_GB = OFF_MQ + MEM_WIDTH
IN_WIDTH = OFF_GB + MIX_WIDTH
EPS = 1e-6
NEG_INF = -1e30
FORCED_SCORE = 1e9

LANE = 128
CMP_ROW = CMP_STRIDE * 2 * KV_WIDTH
SSM_GPB = LANE // SSM_GROUP
SSM_SLABS = SSM_WIDTH // LANE
VMEM_LIMIT = 56 * 2**20


def _cparams(n_grid, vmem=VMEM_LIMIT):
    return pltpu.CompilerParams(dimension_semantics=("arbitrary",) * n_grid, vmem_limit_bytes=vmem)


def _rms(x, g):
    return x * lax.rsqrt(jnp.mean(x * x, axis=-1, keepdims=True) + EPS) * g


def _dot(a, b):
    return jnp.dot(a, b, preferred_element_type=F32)


def _dot_nt(a, b):
    return lax.dot_general(a, b, (((1,), (1,)), ((), ())), preferred_element_type=F32)


def _masked_probs(s, mask):
    s = jnp.where(mask, s, NEG_INF)
    m = jnp.max(s, axis=-1, keepdims=True)
    e = jnp.where(mask, jnp.exp(s - m), 0.0)
    return e / jnp.maximum(jnp.sum(e, axis=-1, keepdims=True), 1e-30)


def _row_tile(n, pref):
    t = min(n, pref)
    assert n % t == 0
    return t


_SEGS = ((0, 512), (512, 768), (768, 1024), (1024, 1280), (1280, 1792), (1792, 2304), (2304, 3840), (3840, 3968))
_SIGMOID_FROM = 2304
_PACKED_WIDTH = 3968


def _pack_w_in(w):
    gn = jnp.pad(w[:, OFF_GN:OFF_U], ((0, 0), (0, LANE - 3 * N_HEADS)))
    return jnp.concatenate([w[:, :OFF_GN], w[:, OFF_U:], gn], axis=1).astype(BF16)


_PROMPT_ROW_SEGS = (1, 4, 5, 6, 7)


def _inproj_kernel(x_ref, g_ref, w_ref, *o_refs, seq_len):
    h = _rms(x_ref[...], g_ref[...]).astype(BF16)
    z = []
    for a, b in _SEGS:
        z.append(_dot(h, w_ref[:, a:b]))
        if a >= _SIGMOID_FROM:
            z[-1] = jax.nn.sigmoid(z[-1])
    if not seq_len:
        for zi, o_ref in zip(z, o_refs):
            o_ref[...] = zi
        return
    for s, o_ref in zip(_PROMPT_ROW_SEGS, o_refs):
        o_ref[...] = z[s]
    kvt_refs = o_refs[len(_PROMPT_ROW_SEGS):len(_PROMPT_ROW_SEGS) + 3]
    qp_ref, kp_ref, vw_ref = o_refs[len(_PROMPT_ROW_SEGS) + 3:]
    for s, o_ref in zip((1, 2, 3), kvt_refs):
        o_ref[0] = z[s].T
    tm = x_ref.shape[0]
    pos = (pl.program_id(0) * tm) % seq_len + lax.broadcasted_iota(jnp.int32, (tm, 1), 0)
    pos_hi = ((pos // SLC_BLOCK) * SLC_BLOCK).astype(F32)
    pos_lo = (pos % SLC_BLOCK).astype(F32)
    lane = lax.broadcasted_iota(jnp.int32, (1, HEAD_DIM), 1)
    terms = jnp.where(lane == 0, pos_hi, jnp.where(lane == 1, pos_lo, jnp.where(lane < 4, 1.0, 0.0)))
    onehot = jnp.where(pos // SLC_BLOCK == lax.broadcasted_iota(jnp.int32, (1, LANE), 1), 1.0, 0.0)
    q, kvs, kvw = z[0], z[2], z[3]
    head = lambda a, i: a[:, i * HEAD_DIM:(i + 1) * HEAD_DIM]
    for g in range(N_KV):
        kp_ref[g] = jnp.concatenate([onehot, head(kvs, g), terms], axis=1).astype(BF16)
        vw_ref[g] = jnp.concatenate([head(kvs, N_KV + g), head(kvw, N_KV + g), head(kvw, g), terms],
                                    axis=1).astype(BF16)
    for hd in range(N_HEADS):
        slope = 2.0 ** (-8.0 * (hd + 1) / N_HEADS)
        qterms = jnp.where(lane < 2, slope,
                           jnp.where(lane == 2, -slope * pos_hi, jnp.where(lane == 3, -slope * pos_lo, 0.0)))
        qp_ref[hd] = jnp.concatenate([head(q, hd) * (HEAD_DIM ** -0.5), qterms], axis=1).astype(BF16)


def _in_proj(x2, g, w_packed, seq_len=0):
    n = x2.shape[0]
    tm = _row_tile(n, 512)
    segs = [_SEGS[s] for s in _PROMPT_ROW_SEGS] if seq_len else _SEGS
    out_specs = [pl.BlockSpec((tm, b - a), lambda i: (i, 0)) for a, b in segs]
    out_shape = [jax.ShapeDtypeStruct((n, b - a), F32) for a, b in segs]
    if seq_len:
        assert seq_len % tm == 0
        per_seq = seq_len // tm
        for _ in range(3):
            out_specs.append(pl.BlockSpec((1, 2 * KV_WIDTH, tm), lambda i: (i // per_seq, 0, i % per_seq)))
            out_shape.append(jax.ShapeDtypeStruct((n // seq_len, 2 * KV_WIDTH, seq_len), F32))
        for lead, width in ((N_HEADS, LANE), (N_KV, 2 * LANE), (N_KV, 2 * LANE)):
            out_specs.append(pl.BlockSpec((lead, tm, width), lambda i: (0, i, 0)))
            out_shape.append(jax.ShapeDtypeStruct((lead, n, width), BF16))
    return pl.pallas_call(
        functools.partial(_inproj_kernel, seq_len=seq_len),
        grid=(n // tm,),
        in_specs=[pl.BlockSpec((tm, D_MODEL), lambda i: (i, 0)),
                  pl.BlockSpec((1, D_MODEL), lambda i: (0, 0)),
                  pl.BlockSpec((D_MODEL, _PACKED_WIDTH), lambda i: (0, 0))],
        out_specs=out_specs,
        out_shape=out_shape,
        compiler_params=_cparams(1),
        name="in_proj",
    )(x2, g.reshape(1, D_MODEL), w_packed)


def _norm_mm_kernel(x_ref, g_ref, w_ref, o_ref):
    o_ref[...] = _dot(_rms(x_ref[...], g_ref[...]).astype(BF16), w_ref[...])


def _norm_mm(x2, g, w_bf16):
    n, k = x2.shape
    m = w_bf16.shape[1]
    tm = _row_tile(n, 256)
    return pl.pallas_call(
        _norm_mm_kernel,
        grid=(n // tm,),
        in_specs=[pl.BlockSpec((tm, k), lambda i: (i, 0)),
                  pl.BlockSpec((1, k), lambda i: (0, 0)),
                  pl.BlockSpec((k, m), lambda i: (0, 0))],
        out_specs=pl.BlockSpec((tm, m), lambda i: (i, 0)),
        out_shape=jax.ShapeDtypeStruct((n, m), F32),
        compiler_params=_cparams(1),
        name="norm_mm",
    )(x2, g.reshape(1, k), w_bf16)


def _expand_cmp_w1(w1):
    w = w1.reshape(2, 2, CMP_STRIDE, HEAD_DIM, HEAD_DIM).astype(BF16)
    blocks = []
    for kv in range(2):
        for g in range(N_KV):
            off = (kv * N_KV + g) * HEAD_DIM
            blocks.append(jnp.pad(w[kv], ((0, 0), (0, 0), (0, 0), (off, 2 * KV_WIDTH - HEAD_DIM - off))))
    return jnp.concatenate(blocks, axis=2)


def _flat_cmp_pe(pe):
    p = pe.reshape(2, 2, CMP_STRIDE, HEAD_DIM)
    p = jnp.broadcast_to(p.transpose(1, 2, 0, 3)[:, :, :, None, :], (2, CMP_STRIDE, 2, N_KV, HEAD_DIM))
    return p.reshape(2, 1, CMP_ROW)


def _expand_cmp_w2(w2):
    eye2 = jnp.eye(2, dtype=w2.dtype)
    big = jnp.einsum('kde,kK,gG->kgdGKe', w2, eye2, eye2)
    return big.reshape(2 * KV_WIDTH, 2 * KV_WIDTH).astype(BF16)


def _cmp_lohi(x, pe_ref, w_ref):
    lo = _dot((x + pe_ref[0]).astype(BF16), w_ref[0])
    hi = _dot((x + pe_ref[1]).astype(BF16), w_ref[1])
    return jnp.concatenate([lo, hi], axis=1)


def _cmp_lh_kernel(x_ref, pe_ref, w_ref, o_ref):
    o_ref[...] = _cmp_lohi(x_ref[...], pe_ref, w_ref)


def _cmp_lh(x2, pe_flat, w1_big):
    n = x2.shape[0]
    tm = _row_tile(n, 256)
    return pl.pallas_call(
        _cmp_lh_kernel,
        grid=(n // tm,),
        in_specs=[pl.BlockSpec((tm, CMP_ROW), lambda i: (i, 0)),
                  pl.BlockSpec((2, 1, CMP_ROW), lambda i: (0, 0, 0)),
                  pl.BlockSpec((2, CMP_ROW, 2 * KV_WIDTH), lambda i: (0, 0, 0))],
        out_specs=pl.BlockSpec((tm, 4 * KV_WIDTH), lambda i: (i, 0)),
        out_shape=jax.ShapeDtypeStruct((n, 4 * KV_WIDTH), F32),
        compiler_params=_cparams(1),
        name="cmp_lh",
    )(x2, pe_flat, w1_big.reshape(2, CMP_ROW, 2 * KV_WIDTH))


def _cmp_combine(lh, w2_big):
    r = lh.shape[0]
    hi_next = pltpu.roll(lh[:, 2 * KV_WIDTH:], r - 1, 0)
    return _dot(jax.nn.gelu(lh[:, :2 * KV_WIDTH] + hi_next).astype(BF16), w2_big)


def _cmp_combine_kernel(lh_ref, w2_ref, o_ref):
    o_ref[0] = _cmp_combine(lh_ref[0], w2_ref[...])


def _cmp_tokens_prompt(lh, w2_big):
    b, r, _ = lh.shape
    return pl.pallas_call(
        _cmp_combine_kernel,
        grid=(b,),
        in_specs=[pl.BlockSpec((1, r, 4 * KV_WIDTH), lambda i: (i, 0, 0)),
                  pl.BlockSpec((2 * KV_WIDTH, 2 * KV_WIDTH), lambda i: (0, 0))],
        out_specs=pl.BlockSpec((1, r, 2 * KV_WIDTH), lambda i: (i, 0, 0)),
        out_shape=jax.ShapeDtypeStruct((b, r, 2 * KV_WIDTH), F32),
        compiler_params=_cparams(1),
        name="cmp_combine",
    )(lh, w2_big)


def _alibi_slopes():
    return jnp.exp2(-8.0 * (jnp.arange(N_HEADS, dtype=F32) + 1.0) / N_HEADS)


def _query_pack(q, pos):
    b, t, _ = q.shape
    qh = q.reshape(b, t, N_HEADS, HEAD_DIM).transpose(0, 2, 1, 3) * (HEAD_DIM ** -0.5)
    slope = _alibi_slopes()[:, None]
    pos_hi = ((pos // SLC_BLOCK) * SLC_BLOCK).astype(F32)[None, :]
    pos_lo = (pos % SLC_BLOCK).astype(F32)[None, :]
    ones = jnp.ones((1, t), F32)
    c = jnp.stack([slope * ones, slope * ones, -slope * pos_hi, -slope * pos_lo], axis=-1)
    c = jnp.pad(c, ((0, 0), (0, 0), (0, HEAD_DIM - 4)))
    c = jnp.broadcast_to(c[None], (b, N_HEADS, t, HEAD_DIM))
    return jnp.concatenate([qh, c], axis=-1).astype(BF16)


def _key_consts(kpos):
    hi = ((kpos // SLC_BLOCK) * SLC_BLOCK).astype(F32)
    lo = (kpos % SLC_BLOCK).astype(F32)
    one = jnp.ones_like(hi)
    return jnp.pad(jnp.stack([hi, lo, one, one], axis=-1), ((0, 0), (0, HEAD_DIM - 4)))


def _split3(x):
    hi = x.astype(BF16)
    r1 = x - hi.astype(F32)
    mid = r1.astype(BF16)
    lo = (r1 - mid.astype(F32)).astype(BF16)
    return hi, mid, lo


def _pool_blocks_t(ps, pool_t):
    hi, mid, lo = _split3(ps)
    return _dot_nt(pool_t, hi) + _dot_nt(pool_t, mid) + _dot_nt(pool_t, lo)


def _topk_select_cols(score, blk_f, k):
    sel = jnp.zeros(score.shape, F32)
    picks = []
    for _ in range(k):
        m = jnp.max(score, axis=0, keepdims=True)
        idx = jnp.min(jnp.where(score == m, blk_f, 1e9), axis=0, keepdims=True)
        pick = blk_f == idx
        sel = jnp.where(pick, 1.0, sel)
        score = jnp.where(pick, -3e38, score)
        picks.append(idx)
    return sel, picks


def _add_token_bias(s, bias):
    rows, n = s.shape
    q = bias.shape[0]
    return (s.reshape(rows // q, q, n) + bias[None]).reshape(rows, n)


def _biased_probs(s, bias, row_any=None):
    x = _add_token_bias(s, bias)
    e = jnp.exp(x - jnp.max(x, axis=-1, keepdims=True))
    inv = 1.0 / jnp.maximum(jnp.sum(e, axis=-1, keepdims=True), 1e-30)
    if row_any is not None:
        inv = jnp.where(row_any, inv, 0.0)
    return e * inv


def _nsa_prompt_kernel(qp_ref, gn_ref, ckp_ref, kp_ref, vw_ref, pool_ref, o_ref, *, tk):
    j = pl.program_id(2)
    start = j * Q_BLOCK
    rows = HPG * Q_BLOCK
    qp = qp_ref[...].reshape(rows, LANE)
    pos_t = start + lax.broadcasted_iota(jnp.int32, (Q_BLOCK, 1), 0)
    row = lax.broadcasted_iota(jnp.int32, (rows, 1), 0)

    ckp = ckp_ref[0, 0]
    n_c = ckp.shape[0]
    c_end = lax.broadcasted_iota(jnp.int32, (1, n_c), 1) * CMP_STRIDE + (CMP_LEN - 1)
    bias_c = jnp.where(c_end <= pos_t, 0.0, NEG_INF)
    any_c = start + (row & (Q_BLOCK - 1)) >= CMP_LEN - 1
    p_c = _biased_probs(_dot_nt(qp, ckp[:, :LANE]), bias_c, any_c)
    o_c = _dot(p_c.astype(BF16), ckp[:, LANE:LANE + HEAD_DIM])
    ps = p_c[0:Q_BLOCK]
    for h in range(1, HPG):
        ps = ps + p_c[h * Q_BLOCK:(h + 1) * Q_BLOCK]

    wlen = WINDOW + Q_BLOCK
    woff = pl.multiple_of(jnp.maximum(start - WINDOW, 0), Q_BLOCK)
    d_w = pos_t - (woff + lax.broadcasted_iota(jnp.int32, (1, wlen), 1))
    bias_w = jnp.where((d_w >= 0) & (d_w <= WINDOW), 0.0, NEG_INF)
    p_w = _biased_probs(_dot_nt(qp, vw_ref[0, 0, pl.ds(woff, wlen), LANE:2 * LANE]), bias_w)
    o_w = _dot(p_w.astype(BF16), vw_ref[0, 0, pl.ds(woff, wlen), HEAD_DIM:2 * HEAD_DIM])

    imp_t = _pool_blocks_t(ps, pool_ref[...])
    blk = lax.broadcasted_iota(jnp.int32, (LANE, 1), 0)
    cur = (start + lax.broadcasted_iota(jnp.int32, (1, Q_BLOCK), 1)) // SLC_BLOCK
    valid = blk <= cur
    forced = (blk == 0) | (blk == cur) | (blk == cur - 1)
    score = jnp.where(valid & ~forced, imp_t, -1.0)
    sel_t, _ = _topk_select_cols(score, blk.astype(F32), SLC_TOPK - 3)
    mb = jnp.where(((sel_t > 0.0) | forced) & valid, 0.0, NEG_INF).T.astype(BF16)
    qs = jnp.concatenate([jnp.concatenate([mb] * HPG, axis=0), qp], axis=1)

    def tile(i, state, diag):
        m, l, acc = state
        off = pl.multiple_of(i * tk, tk)
        s = _dot_nt(qs, kp_ref[0, 0, pl.ds(off, tk), :])
        if diag:
            kpos = off + lax.broadcasted_iota(jnp.int32, (1, tk), 1)
            s = _add_token_bias(s, jnp.where(kpos <= pos_t, 0.0, NEG_INF))
        m2 = jnp.maximum(m, jnp.max(s, axis=-1, keepdims=True))
        a = jnp.exp(m - m2)
        e = jnp.exp(s - m2)
        l = a * l + jnp.sum(e, axis=-1, keepdims=True)
        acc = a * acc + _dot(e.astype(BF16), vw_ref[0, 0, pl.ds(off, tk), 0:HEAD_DIM])
        return m2, l, acc

    n_full = start // tk
    carry = (jnp.full((rows, 1), NEG_INF, F32), jnp.zeros((rows, 1), F32), jnp.zeros((rows, HEAD_DIM), F32))
    carry = lax.fori_loop(0, n_full, lambda i, c: tile(i, c, False), carry)
    _, l_s, acc_s = tile(n_full, carry, True)
    o_s = acc_s / jnp.maximum(l_s, 1e-30)

    gn = gn_ref[0, 0]
    outs = []
    for h in range(HPG):
        sl = slice(h * Q_BLOCK, (h + 1) * Q_BLOCK)
        outs.append(gn[:, h:h + 1] * o_c[sl] + gn[:, HPG + h:HPG + h + 1] * o_s[sl]
                    + gn[:, 2 * HPG + h:2 * HPG + h + 1] * o_w[sl])
    o_ref[0] = jnp.concatenate(outs, axis=1)


def _gate_pack(gn, b, t):
    g = gn[:, :3 * N_HEADS].reshape(b, t, 3, N_KV, HPG).transpose(0, 3, 1, 2, 4).reshape(b, N_KV, t, 3 * HPG)
    return jnp.pad(g, ((0, 0), (0, 0), (0, 0), (0, 16 - 3 * HPG)))


def _nsa_prompt(qp, kp, vw, gn, ckv, b, t):
    n_b = t // SLC_BLOCK
    assert n_b <= LANE and t % Q_BLOCK == 0 and t >= WINDOW + Q_BLOCK
    tk = min(1024, t)
    qp = qp.reshape(N_HEADS, b, t, LANE)
    kp = kp.reshape(N_KV, b, t, 2 * LANE)
    vw = vw.reshape(N_KV, b, t, 2 * LANE)
    n_c = t // CMP_STRIDE
    c4 = ckv.reshape(b, n_c, N_KV, 2, HEAD_DIM).transpose(3, 0, 2, 1, 4)
    c_end = jnp.arange(n_c, dtype=jnp.int32) * CMP_STRIDE + (CMP_LEN - 1)
    ckp = jnp.concatenate([c4[0], jnp.broadcast_to(_key_consts(c_end), c4[0].shape), c4[1], jnp.zeros_like(c4[1])],
                          axis=-1).astype(BF16)
    pool = (jnp.arange(LANE)[:, None] == jnp.arange(n_c)[None, :] // (SLC_BLOCK // CMP_STRIDE)).astype(BF16)
    gn2 = _gate_pack(gn, b, t)
    return pl.pallas_call(
        functools.partial(_nsa_prompt_kernel, tk=tk),
        grid=(b, N_KV, t // Q_BLOCK),
        in_specs=[pl.BlockSpec((HPG, 1, Q_BLOCK, LANE), lambda i, g, j: (g, i, j, 0)),
                  pl.BlockSpec((1, 1, Q_BLOCK, 16), lambda i, g, j: (i, g, j, 0)),
                  pl.BlockSpec((1, 1, n_c, 2 * LANE), lambda i, g, j: (i, g, 0, 0)),
                  pl.BlockSpec((1, 1, t, 2 * LANE), lambda i, g, j: (g, i, 0, 0)),
                  pl.BlockSpec((1, 1, t, 2 * LANE), lambda i, g, j: (g, i, 0, 0)),
                  pl.BlockSpec((LANE, n_c), lambda i, g, j: (0, 0))],
        out_specs=pl.BlockSpec((1, Q_BLOCK, HPG * HEAD_DIM), lambda i, g, j: (i, j, g)),
        out_shape=jax.ShapeDtypeStruct((b, t, NSA_WIDTH), F32),
        compiler_params=_cparams(3),
        name="nsa_prompt",
    )(qp, gn2, ckp, kp, vw, pool).reshape(b * t, NSA_WIDTH)


def _mem_attend_kernel(q_ref, kv_ref, o_ref):
    q = q_ref[0]
    kv = kv_ref[...].reshape(N_MEM, 2 * MEM_WIDTH)
    scale = MEM_HEAD_DIM ** -0.5
    for h in range(MEM_HEADS):
        sl = slice(h * MEM_HEAD_DIM, (h + 1) * MEM_HEAD_DIM)
        k = kv[:, sl].astype(BF16)
        v = kv[:, MEM_WIDTH + h * MEM_HEAD_DIM:MEM_WIDTH + (h + 1) * MEM_HEAD_DIM].astype(BF16)
        s = _dot_nt(q[:, sl].astype(BF16), k) * scale
        e = jnp.exp(s - jnp.max(s, axis=-1, keepdims=True))
        p = e / jnp.sum(e, axis=-1, keepdims=True)
        o_ref[0, :, sl] = _dot(p.astype(BF16), v)


def _mem_attend(mq, mkv, layer=None):
    b, t, _ = mq.shape
    tm = _row_tile(t, 512)
    if layer is None:
        kv_spec = pl.BlockSpec((1, N_MEM, 2 * MEM_WIDTH), lambda i, j: (i, 0, 0))
    else:
        kv_spec = pl.BlockSpec((1, 1, N_MEM, 2 * MEM_WIDTH), lambda i, j: (layer, i, 0, 0))
    return pl.pallas_call(
        _mem_attend_kernel,
        grid=(b, t // tm),
        in_specs=[pl.BlockSpec((1, tm, MEM_WIDTH), lambda i, j: (i, j, 0)), kv_spec],
        out_specs=pl.BlockSpec((1, tm, MEM_WIDTH), lambda i, j: (i, j, 0)),
        out_shape=jax.ShapeDtypeStruct((b, t, MEM_WIDTH), F32),
        compiler_params=_cparams(2),
        name="mem_attend",
    )(mq, mkv)


def _ssm_disc_kernel(ar_ref, ai_ref, ldt_ref, br_ref, bi_ref, abr_ref, abi_ref, bbr_ref, bbi_ref):
    ar = ar_ref[...]
    ai = ai_ref[...]
    dt = jnp.exp(ldt_ref[...])
    mag = jnp.exp(dt * ar)
    abr = mag * jnp.cos(dt * ai)
    abi = mag * jnp.sin(dt * ai)
    den = ar * ar + ai * ai
    fr = ((abr - 1.0) * ar + abi * ai) / den
    fi = (abi * ar - (abr - 1.0) * ai) / den
    br = br_ref[...]
    bi = bi_ref[...]
    abr_ref[...] = abr
    abi_ref[...] = abi
    bbr_ref[...] = fr * br - fi * bi
    bbi_ref[...] = fr * bi + fi * br


def _ssm_discretise(a_re, a_im, log_dt, b_re, b_im):
    rep = lambda x: jnp.repeat(x, SSM_GROUP, axis=0)
    tb = lambda x: x.transpose(0, 2, 1).reshape(SSM_WIDTH, SSM_STATE)
    shp = jax.ShapeDtypeStruct((SSM_WIDTH, SSM_STATE), F32)
    abr, abi, bbr, bbi = pl.pallas_call(
        _ssm_disc_kernel, out_shape=[shp] * 4, name="ssm_disc",
    )(rep(a_re), rep(a_im), rep(jnp.broadcast_to(log_dt[:, None], a_re.shape)), tb(b_re), tb(b_im))
    eye = jnp.eye(SSM_GPB, dtype=F32)

    def in_map(bb):
        bb = bb.reshape(SSM_SLABS, SSM_GPB, SSM_GROUP, SSM_STATE)
        return jnp.einsum('jgcp,gG->jgcGp', bb, eye).reshape(SSM_SLABS, LANE, SSM_GPB * SSM_STATE).astype(BF16)

    flat = lambda x: x[::SSM_GROUP].reshape(1, SSM_FLAT)
    return flat(abr), flat(abi), in_map(bbr), in_map(bbi)


def _ssm_out_map(c):
    eye = jnp.eye(SSM_GPB, dtype=F32)
    c = c.reshape(SSM_SLABS, SSM_GPB, SSM_GROUP, SSM_STATE)
    return jnp.einsum('jgcp,gG->jgpGc', c, eye).reshape(SSM_SLABS, SSM_GPB * SSM_STATE, LANE).astype(BF16)


def _ssm_kernel(u_ref, h0r_ref, h0i_ref, abr_ref, abi_ref, bdr_ref, bdi_ref, cdr_ref, cdi_ref, d_ref,
                wg_ref, bg_ref, o_ref, hr_ref, hi_ref, sr_ref, si_ref, st_ref, *, n_scan):
    c_idx = pl.program_id(1)
    slab = SSM_GPB * SSM_STATE

    @pl.when(c_idx == 0)
    def _():
        st_ref[0:1, :] = h0r_ref[0]
        st_ref[1:2, :] = h0i_ref[0]

    u = u_ref[0]
    ub = u.astype(BF16)
    for j in range(SSM_SLABS):
        uj = ub[:, j * LANE:(j + 1) * LANE]
        sr_ref[:, j * slab:(j + 1) * slab] = _dot(uj, bdr_ref[j])
        si_ref[:, j * slab:(j + 1) * slab] = _dot(uj, bdi_ref[j])

    ar = abr_ref[...]
    ai = abi_ref[...]

    def step(t, carry):
        hr, hi = carry
        nr = ar * hr - ai * hi + sr_ref[pl.ds(t, 1), :]
        ni = ar * hi + ai * hr + si_ref[pl.ds(t, 1), :]
        sr_ref[pl.ds(t, 1), :] = nr
        si_ref[pl.ds(t, 1), :] = ni
        return nr, ni

    hr, hi = lax.fori_loop(0, n_scan, step, (st_ref[0:1, :], st_ref[1:2, :]))
    st_ref[0:1, :] = hr
    st_ref[1:2, :] = hi
    hr_ref[0] = hr
    hi_ref[0] = hi

    ys = []
    for j in range(SSM_SLABS):
        hrj = sr_ref[:, j * slab:(j + 1) * slab].astype(BF16)
        hij = si_ref[:, j * slab:(j + 1) * slab].astype(BF16)
        ys.append(_dot(hrj, cdr_ref[j]) - _dot(hij, cdi_ref[j]))
    y = jnp.concatenate(ys, axis=1) + d_ref[...] * u
    y = jax.nn.gelu(y)
    o_ref[0] = y * jax.nn.sigmoid(_dot(y.astype(BF16), wg_ref[...]) + bg_ref[...])


def _ssm(u, h0r, h0i, disc, cdr, cdi, d, w_glu, b_glu, t_valid=None):
    b, t, _ = u.shape
    tc = _row_tile(t, 256)
    n_scan = tc if t_valid is None else t_valid
    assert t_valid is None or t == tc
    abr, abi, bdr, bdi = disc
    full = lambda a: pl.BlockSpec(a.shape, lambda i, c: (0,) * a.ndim)
    state_spec = pl.BlockSpec((1, 1, SSM_FLAT), lambda i, c: (i, 0, 0))
    args = (abr, abi, bdr, bdi, cdr, cdi, d.reshape(1, SSM_WIDTH), w_glu.astype(BF16), b_glu.reshape(1, SSM_WIDTH))
    o, hr, hi = pl.pallas_call(
        functools.partial(_ssm_kernel, n_scan=n_scan),
        grid=(b, t // tc),
        in_specs=[pl.BlockSpec((1, tc, SSM_WIDTH), lambda i, c: (i, c, 0)), state_spec, state_spec]
                 + [full(a) for a in args],
        out_specs=[pl.BlockSpec((1, tc, SSM_WIDTH), lambda i, c: (i, c, 0)), state_spec, state_spec],
        out_shape=[jax.ShapeDtypeStruct((b, t, SSM_WIDTH), F32),
                   jax.ShapeDtypeStruct((b, 1, SSM_FLAT), F32),
                   jax.ShapeDtypeStruct((b, 1, SSM_FLAT), F32)],
        scratch_shapes=[pltpu.VMEM((tc, SSM_FLAT), F32), pltpu.VMEM((tc, SSM_FLAT), F32),
                        pltpu.VMEM((8, SSM_FLAT), F32)],
        compiler_params=_cparams(2),
        name="ssm",
    )(u, h0r.reshape(b, 1, SSM_FLAT), h0i.reshape(b, 1, SSM_FLAT), *args)
    shp = (b, N_SSM_GROUPS, SSM_STATE)
    return o, hr.reshape(shp), hi.reshape(shp)


def _merge_kernel(x_ref, oa_ref, ob_ref, oc_ref, gb_ref, w_ref, o_ref):
    acc = x_ref[...]
    for i, r in enumerate((oa_ref, ob_ref, oc_ref)):
        w = r.shape[1]
        off = sum((NSA_WIDTH, SSM_WIDTH, MEM_WIDTH)[:i])
        mixed = (r[...] * gb_ref[:, off:off + w]).astype(BF16)
        acc = acc + _dot(mixed, w_ref[off:off + w, :])
    o_ref[...] = acc


def _merge(x2, o_a, o_b, o_c, gb, w_o_bf16):
    n = x2.shape[0]
    tm = _row_tile(n, 512)
    row = lambda w: pl.BlockSpec((tm, w), lambda i: (i, 0))
    return pl.pallas_call(
        _merge_kernel,
        grid=(n // tm,),
        in_specs=[row(D_MODEL), row(NSA_WIDTH), row(SSM_WIDTH), row(MEM_WIDTH), row(MIX_WIDTH),
                  pl.BlockSpec((MIX_WIDTH, D_MODEL), lambda i: (0, 0))],
        out_specs=row(D_MODEL),
        out_shape=jax.ShapeDtypeStruct((n, D_MODEL), F32),
        compiler_params=_cparams(1),
        name="merge",
    )(x2, o_a, o_b, o_c, gb, w_o_bf16)


def _ffn_kernel(x_ref, g_ref, w1_ref, w2_ref, fg_ref, o_ref, h_ref, acc_ref, *, final_norm):
    k = pl.program_id(1)

    @pl.when(k == 0)
    def _():
        h_ref[...] = _rms(x_ref[...], g_ref[...]).astype(BF16)
        acc_ref[...] = jnp.zeros_like(acc_ref)

    a = jnp.square(jnp.maximum(_dot(h_ref[...], w1_ref[...]), 0.0))
    acc_ref[...] += _dot(a.astype(BF16), w2_ref[...])

    @pl.when(k == pl.num_programs(1) - 1)
    def _():
        y = x_ref[...] + acc_ref[...]
        if final_norm:
            y = _rms(y, fg_ref[...])
        o_ref[...] = y


def _ffn(x2, g, w1_bf16, w2_bf16, final_g, final_norm):
    n = x2.shape[0]
    tm = _row_tile(n, 1024)
    tf = 1024
    return pl.pallas_call(
        functools.partial(_ffn_kernel, final_norm=final_norm),
        grid=(n // tm, D_FF // tf),
        in_specs=[pl.BlockSpec((tm, D_MODEL), lambda i, k: (i, 0)),
                  pl.BlockSpec((1, D_MODEL), lambda i, k: (0, 0)),
                  pl.BlockSpec((D_MODEL, tf), lambda i, k: (0, k)),
                  pl.BlockSpec((tf, D_MODEL), lambda i, k: (k, 0)),
                  pl.BlockSpec((1, D_MODEL), lambda i, k: (0, 0))],
        out_specs=pl.BlockSpec((tm, D_MODEL), lambda i, k: (i, 0)),
        out_shape=jax.ShapeDtypeStruct((n, D_MODEL), F32),
        scratch_shapes=[pltpu.VMEM((tm, D_MODEL), BF16), pltpu.VMEM((tm, D_MODEL), F32)],
        compiler_params=_cparams(2),
        name="ffn",
    )(x2, g.reshape(1, D_MODEL), w1_bf16, w2_bf16, final_g.reshape(1, D_MODEL))


def _layer_params(l, norm1_g, w_in, cmp_pe, cmp_w1, cmp_w2, ssm_a_re, ssm_a_im, ssm_log_dt, ssm_b_re, ssm_b_im,
                  ssm_c_re, ssm_c_im, ssm_d, w_glu, b_glu, mem_norm_g, w_mem_kv, w_o, norm2_g, w_ff1, w_ff2):
    return dict(
        norm1_g=norm1_g[l], w_in=_pack_w_in(w_in[l]),
        pe=_flat_cmp_pe(cmp_pe[l]), w1=_expand_cmp_w1(cmp_w1[l]), w2=_expand_cmp_w2(cmp_w2[l]),
        disc=_ssm_discretise(ssm_a_re[l], ssm_a_im[l], ssm_log_dt[l], ssm_b_re[l], ssm_b_im[l]),
        cdr=_ssm_out_map(ssm_c_re[l]), cdi=_ssm_out_map(ssm_c_im[l]), d=ssm_d[l],
        w_glu=w_glu[l], b_glu=b_glu[l], mem_norm_g=mem_norm_g[l], w_mem_kv=w_mem_kv[l].astype(BF16),
        w_o=w_o[l].astype(BF16), norm2_g=norm2_g[l], w_ff1=w_ff1[l].astype(BF16), w_ff2=w_ff2[l].astype(BF16))


def _prompt_layer(x2, mem2, p, b, t, final_g, final_norm):
    kvc, u, mq, gb, gn, kvc_t, kvs_t, kvw_t, qp, kp, vw = _in_proj(x2, p['norm1_g'], p['w_in'], seq_len=t)
    lh = _cmp_lh(kvc.reshape(b * t // CMP_STRIDE, CMP_ROW), p['pe'], p['w1'])
    ckv = _cmp_tokens_prompt(lh.reshape(b, t // CMP_STRIDE, 4 * KV_WIDTH), p['w2'])
    o_a = _nsa_prompt(qp, kp, vw, gn, ckv, b, t)
    zero = jnp.zeros((b, N_SSM_GROUPS, SSM_STATE), F32)
    o_b, hr, hi = _ssm(u.reshape(b, t, SSM_WIDTH), zero, zero, p['disc'], p['cdr'], p['cdi'], p['d'],
                       p['w_glu'], p['b_glu'])
    mkv = _norm_mm(mem2, p['mem_norm_g'], p['w_mem_kv'])
    o_c = _mem_attend(mq.reshape(b, t, MEM_WIDTH), mkv.reshape(b, N_MEM, 2 * MEM_WIDTH))
    x2 = _merge(x2, o_a, o_b.reshape(b * t, SSM_WIDTH), o_c.reshape(b * t, MEM_WIDTH), gb, p['w_o'])
    x2 = _ffn(x2, p['norm2_g'], p['w_ff1'], p['w_ff2'], final_g, final_norm)
    rows_major = lambda a: a.reshape(b, 2, N_KV, HEAD_DIM, a.shape[-1]).transpose(0, 4, 1, 2, 3)
    wl = min(WINDOW, t)
    return x2, dict(cmp=rows_major(kvc_t), slc=rows_major(kvs_t), win=rows_major(kvw_t[:, :, t - wl:]), hr=hr, hi=hi,
                    mkv=mkv.reshape(b, N_MEM, 2, MEM_HEADS, MEM_HEAD_DIM))


CHUNKS_PER_PAGE = PAGE_SIZE // CMP_STRIDE


def _cmp_lh_paged_kernel(pt_ref, cache_ref, pe_ref, w_ref, o_ref, buf_ref, stage_ref, xl_ref, sem, *,
                         pages, layer, n_s, n_groups):
    step = pl.program_id(0)
    slot = step % 2

    def page_copy(grp, i, sl):
        page = pt_ref[grp // n_s, (grp % n_s) * pages + i]
        return pltpu.make_async_copy(cache_ref.at[layer, page], buf_ref.at[sl, i], sem.at[sl, i])

    @pl.when(step == 0)
    def _():
        for i in range(pages):
            page_copy(step, i, slot).start()

    @pl.when(step + 1 < n_groups)
    def _():
        for i in range(pages):
            page_copy(step + 1, i, 1 - slot).start()

    for i in range(pages):
        page_copy(step, i, slot).wait()

    n_stage = stage_ref.shape[0]

    def relayout(it, carry):
        for u in range(n_stage):
            i = it * n_stage + u
            xt = buf_ref[slot, i].reshape(2 * KV_WIDTH, PAGE_SIZE).T
            base = pl.multiple_of(i * CHUNKS_PER_PAGE, CHUNKS_PER_PAGE)
            for h in range(2 * KV_WIDTH // LANE):
                stage_ref[u, h] = xt[:, h * LANE:(h + 1) * LANE]
                for l in range(CMP_STRIDE):
                    xl_ref[l, pl.ds(base, CHUNKS_PER_PAGE), h * LANE:(h + 1) * LANE] = (
                        stage_ref[u, h, pl.ds(l, CHUNKS_PER_PAGE, stride=CMP_STRIDE), :])
        return carry

    lax.fori_loop(0, pages // n_stage, relayout, 0)
    lo = jnp.zeros((pages * CHUNKS_PER_PAGE, 2 * KV_WIDTH), F32)
    hi = jnp.zeros((pages * CHUNKS_PER_PAGE, 2 * KV_WIDTH), F32)
    for l in range(CMP_STRIDE):
        x = xl_ref[l]
        lo = lo + _dot((x + pe_ref[0, l]).astype(BF16), w_ref[0, l])
        hi = hi + _dot((x + pe_ref[1, l]).astype(BF16), w_ref[1, l])
    o_ref[0] = jnp.concatenate([lo, hi], axis=1)


def _cmp_lh_paged(cache_t, layer, page_table, pe_flat, w1_big):
    b, n_pages = page_table.shape
    pages = math.gcd(n_pages, 32)
    rows = pages * CHUNKS_PER_PAGE
    n_s = n_pages // pages
    n_groups = b * n_s
    page_shape = cache_t.shape[2:]

    grid_spec = pltpu.PrefetchScalarGridSpec(
        num_scalar_prefetch=1,
        grid=(n_groups,),
        in_specs=[pl.BlockSpec(memory_space=pl.ANY),
                  pl.BlockSpec((2, CMP_STRIDE, 1, 2 * KV_WIDTH), lambda t, pt: (0, 0, 0, 0)),
                  pl.BlockSpec((2, CMP_STRIDE, 2 * KV_WIDTH, 2 * KV_WIDTH), lambda t, pt: (0, 0, 0, 0))],
        out_specs=pl.BlockSpec((1, rows, 4 * KV_WIDTH), lambda t, pt: (t // n_s, t % n_s, 0)),
        scratch_shapes=[pltpu.VMEM((2, pages) + page_shape, F32),
                        pltpu.VMEM((math.gcd(pages, 8), 2 * KV_WIDTH // LANE, PAGE_SIZE, LANE), F32),
                        pltpu.VMEM((CMP_STRIDE, rows, 2 * KV_WIDTH), F32),
                        pltpu.SemaphoreType.DMA((2, pages))])
    return pl.pallas_call(
        functools.partial(_cmp_lh_paged_kernel, pages=pages, layer=layer, n_s=n_s, n_groups=n_groups),
        grid_spec=grid_spec,
        out_shape=jax.ShapeDtypeStruct((b, n_pages * CHUNKS_PER_PAGE, 4 * KV_WIDTH), F32),
        compiler_params=_cparams(1),
        name="cmp_lh_paged",
    )(page_table, cache_t, pe_flat.reshape(2, CMP_STRIDE, 1, 2 * KV_WIDTH), w1_big)


TOK_PAD = 8


def _nsa_sample_select_kernel(lhc_ref, lht_ref, w2_ref, q_ref, pool_ref, oc_ref, idx_ref, lh_ref, *, past, n_tok):
    n_cache = lhc_ref.shape[1]
    n_tail = lht_ref.shape[1]
    r = lh_ref.shape[0]
    lh_ref[0:n_cache, :] = lhc_ref[0]
    lh_ref[n_cache:n_cache + n_tail, :] = lht_ref[0]
    lh_ref[n_cache + n_tail:r, :] = jnp.zeros((r - n_cache - n_tail, lh_ref.shape[1]), F32)
    ckv = _cmp_combine(lh_ref[...], w2_ref[...])

    rows = HPG * TOK_PAD
    row = lax.broadcasted_iota(jnp.int32, (rows, 1), 0)
    pos_q = past + (row & (TOK_PAD - 1))
    ci = lax.broadcasted_iota(jnp.int32, (1, r), 1)
    d_c = pos_q - (ci * CMP_STRIDE + (CMP_LEN - 1))
    n_blk = pool_ref.shape[0]
    ps_all = []
    for g in range(N_KV):
        qg = q_ref[0, g]
        q = qg[:, :HEAD_DIM]
        slope = qg[:, HEAD_DIM:HEAD_DIM + 1].astype(F32)
        ck = ckv[:, g * 2 * HEAD_DIM:g * 2 * HEAD_DIM + HEAD_DIM].astype(BF16)
        cv = ckv[:, g * 2 * HEAD_DIM + HEAD_DIM:(g + 1) * 2 * HEAD_DIM].astype(BF16)
        s = _dot_nt(q, ck) - slope * d_c.astype(F32)
        p_c = _masked_probs(s, d_c >= 0)
        oc_ref[0, g] = _dot(p_c.astype(BF16), cv)
        ps = p_c[0:TOK_PAD]
        for h in range(1, HPG):
            ps = ps + p_c[h * TOK_PAD:(h + 1) * TOK_PAD]
        ps_all.append(ps)
    ps = jnp.concatenate(ps_all + [jnp.zeros((LANE - N_KV * TOK_PAD, r), F32)], axis=0)
    imp_t = _pool_blocks_t(ps, pool_ref[...])
    blk = lax.broadcasted_iota(jnp.int32, (n_blk, 1), 0)
    cur = (past + (lax.broadcasted_iota(jnp.int32, (1, LANE), 1) & (TOK_PAD - 1))) // SLC_BLOCK
    valid = blk <= cur
    forced = (blk == 0) | (blk == cur) | (blk == cur - 1)
    score = jnp.where(valid, jnp.where(forced, FORCED_SCORE, imp_t), -1.0)
    _, picks = _topk_select_cols(score, blk.astype(F32), SLC_TOPK)
    for k, pk in enumerate(picks):
        idx_ref[0, k:k + 1, :] = pk.astype(jnp.int32)


def _nsa_sample_select(lh_cache, lh_tail, w2_big, qsel, past, n_tok):
    b, n_cache, _ = lh_cache.shape
    n_tail = lh_tail.shape[1]
    r = -(-(n_cache + n_tail) // LANE) * LANE
    n_b = past // SLC_BLOCK + 1
    assert n_b >= SLC_TOPK
    n_blk = -(-n_b // 8) * 8
    pool = (jnp.arange(n_blk)[:, None] == jnp.arange(r)[None, :] // (SLC_BLOCK // CMP_STRIDE)).astype(BF16)
    rows = HPG * TOK_PAD
    o_c, idx = pl.pallas_call(
        functools.partial(_nsa_sample_select_kernel, past=past, n_tok=n_tok),
        grid=(b,),
        in_specs=[pl.BlockSpec((1, n_cache, 4 * KV_WIDTH), lambda i: (i, 0, 0)),
                  pl.BlockSpec((1, n_tail, 4 * KV_WIDTH), lambda i: (i, 0, 0)),
                  pl.BlockSpec((2 * KV_WIDTH, 2 * KV_WIDTH), lambda i: (0, 0)),
                  pl.BlockSpec((1, N_KV, rows, LANE), lambda i: (i, 0, 0, 0)),
                  pl.BlockSpec((n_blk, r), lambda i: (0, 0))],
        out_specs=[pl.BlockSpec((1, N_KV, rows, HEAD_DIM), lambda i: (i, 0, 0, 0)),
                   pl.BlockSpec((1, SLC_TOPK, LANE), lambda i: (i, 0, 0))],
        out_shape=[jax.ShapeDtypeStruct((b, N_KV, rows, HEAD_DIM), F32),
                   jax.ShapeDtypeStruct((b, SLC_TOPK, LANE), jnp.int32)],
        scratch_shapes=[pltpu.VMEM((r, 4 * KV_WIDTH), F32)],
        compiler_params=_cparams(1),
        name="nsa_sample_select",
    )(lh_cache, lh_tail, w2_big, qsel, pool)
    idx = idx[:, :, :N_KV * TOK_PAD].reshape(b, SLC_TOPK, N_KV, TOK_PAD)[:, :, :, :n_tok]
    return o_c, idx.transpose(0, 2, 3, 1).reshape(-1)


def _rows_minor(cache):
    n = cache.ndim
    return cache.transpose(tuple(range(n - 4)) + (n - 3, n - 2, n - 1, n - 4))


def _nsa_sample_attend_kernel(idx_ref, pt_ref, cache_ref, q_ref, oc_ref, gate_ref, snew_ref, win_ref, wnew_ref,
                              o_ref, kvbuf_ref, sem, *, layer, past, n_tok):
    b = pl.program_id(0)
    n_pages = past // PAGE_SIZE
    cur_blk = past // SLC_BLOCK
    per_page = PAGE_SIZE // SLC_BLOCK
    n_sel = SLC_TOPK * PAGE_SIZE

    half = b % 2

    def sel_block(g, t, k, bb=b):
        return idx_ref[((bb * N_KV + g) * n_tok + t) * SLC_TOPK + k]

    def page_copies(bb, hf, g, t, k):
        c = jnp.minimum(sel_block(g, t, k, bb), cur_blk - 1)
        page = pt_ref[bb * n_pages + c // per_page]
        slot = g * n_tok + t
        dst = pl.ds(k * PAGE_SIZE, PAGE_SIZE)
        return tuple(pltpu.make_async_copy(cache_ref.at[layer, page, kv, g], kvbuf_ref.at[hf, slot, kv, :, dst],
                                           sem.at[hf, kv, slot]) for kv in range(2))

    def start_all(bb, hf):
        for g in range(N_KV):
            for t in range(n_tok):
                for k in range(SLC_TOPK):
                    for cp in page_copies(bb, hf, g, t, k):
                        cp.start()

    @pl.when(b == 0)
    def _():
        start_all(b, half)

    @pl.when(b + 1 < pl.num_programs(0))
    def _():
        start_all(b + 1, 1 - half)

    rows = HPG * TOK_PAD
    row = lax.broadcasted_iota(jnp.int32, (rows, 1), 0)
    tok = row & (TOK_PAD - 1)
    pos_q = past + tok
    lane = lax.broadcasted_iota(jnp.int32, (1, n_sel), 1)
    lane_slot = lane // PAGE_SIZE
    lane_blk = (lane & (PAGE_SIZE - 1)) // SLC_BLOCK
    r_new = lax.broadcasted_iota(jnp.int32, (1, TOK_PAD), 1)
    d_new = pos_q - (past + r_new)
    mask_new = (d_new >= 0) & (r_new < n_tok)
    wb = win_ref.shape[-1]
    d_win = pos_q - (past - wb + lax.broadcasted_iota(jnp.int32, (1, wb), 1))
    mask_win = (d_win >= 0) & (d_win <= WINDOW)

    def joint_attend(q, slope, kt_old, vt_old, d_old, mask_old, k_new, v_new):
        s_o = jnp.where(mask_old, _dot(q, kt_old) - slope * d_old.astype(F32), NEG_INF)
        s_n = jnp.where(mask_new, _dot_nt(q, k_new) - slope * d_new.astype(F32), NEG_INF)
        m = jnp.maximum(jnp.max(s_o, axis=-1, keepdims=True), jnp.max(s_n, axis=-1, keepdims=True))
        e_o = jnp.where(mask_old, jnp.exp(s_o - m), 0.0)
        e_n = jnp.where(mask_new, jnp.exp(s_n - m), 0.0)
        l = jnp.sum(e_o, axis=-1, keepdims=True) + jnp.sum(e_n, axis=-1, keepdims=True)
        return (_dot_nt(e_o.astype(BF16), vt_old) + _dot(e_n.astype(BF16), v_new)) / jnp.maximum(l, 1e-30)

    for g in range(N_KV):
        ksl = slice(g * HEAD_DIM, (g + 1) * HEAD_DIM)
        vsl = slice(KV_WIDTH + g * HEAD_DIM, KV_WIDTH + (g + 1) * HEAD_DIM)
        qg = q_ref[0, g]
        q = qg[:, :HEAD_DIM]
        slope = qg[:, HEAD_DIM:HEAD_DIM + 1].astype(F32)
        k_snew = snew_ref[0][:, ksl].astype(BF16)
        v_snew = snew_ref[0][:, vsl].astype(BF16)
        o_s = jnp.zeros((rows, HEAD_DIM), F32)
        for t in range(n_tok):
            for k in range(SLC_TOPK):
                for cp in page_copies(b, half, g, t, k):
                    cp.wait()
            blk_of_lane = jnp.zeros((1, n_sel), jnp.int32)
            for k in range(SLC_TOPK):
                blk_of_lane = jnp.where(lane_slot == k, sel_block(g, t, k), blk_of_lane)
            d_sel = pos_q - (blk_of_lane * SLC_BLOCK + (lane & (SLC_BLOCK - 1)))
            mask_sel = (d_sel >= 0) & (blk_of_lane < cur_blk) & (blk_of_lane % per_page == lane_blk)
            o_t = joint_attend(q, slope, kvbuf_ref[half, g * n_tok + t, 0].astype(BF16),
                               kvbuf_ref[half, g * n_tok + t, 1].astype(BF16), d_sel, mask_sel, k_snew, v_snew)
            o_s = jnp.where(tok == t, o_t, o_s)
        o_w = joint_attend(q, slope, win_ref[0, 0, 0, g].astype(BF16), win_ref[0, 0, 1, g].astype(BF16), d_win,
                           mask_win, wnew_ref[0][:, ksl].astype(BF16), wnew_ref[0][:, vsl].astype(BF16))
        gate = gate_ref[0, g]
        o_ref[0, g] = gate[:, 0:1] * oc_ref[0, g] + gate[:, 1:2] * o_s + gate[:, 2:3] * o_w


def _nsa_sample_attend(idx, page_table, cache_slc_t, cache_win_t, layer, qsel, o_c, gates, slc_new, win_new, past, n_tok):
    b = qsel.shape[0]
    assert past % PAGE_SIZE == 0 and n_tok <= TOK_PAD
    rows = HPG * TOK_PAD
    wb = cache_win_t.shape[-1]
    per_b = lambda shape: pl.BlockSpec((1,) + shape, lambda i, *_: (i,) + (0,) * len(shape))
    grid_spec = pltpu.PrefetchScalarGridSpec(
        num_scalar_prefetch=2,
        grid=(b,),
        in_specs=[pl.BlockSpec(memory_space=pl.ANY),
                  per_b((N_KV, rows, LANE)), per_b((N_KV, rows, HEAD_DIM)), per_b((N_KV, rows, 16)),
                  per_b((TOK_PAD, 2 * KV_WIDTH)),
                  pl.BlockSpec((1, 1, 2, N_KV, HEAD_DIM, wb), lambda i, *_: (layer, i, 0, 0, 0, 0)),
                  per_b((TOK_PAD, 2 * KV_WIDTH))],
        out_specs=per_b((N_KV, rows, HEAD_DIM)),
        scratch_shapes=[pltpu.VMEM((2, N_KV * n_tok, 2, HEAD_DIM, SLC_TOPK * PAGE_SIZE), F32),
                        pltpu.SemaphoreType.DMA((2, 2, N_KV * n_tok))])
    return pl.pallas_call(
        functools.partial(_nsa_sample_attend_kernel, layer=layer, past=past, n_tok=n_tok),
        grid_spec=grid_spec,
        out_shape=jax.ShapeDtypeStruct((b, N_KV, rows, HEAD_DIM), F32),
        compiler_params=_cparams(1),
        name="nsa_sample_attend",
    )(idx, page_table.reshape(-1), cache_slc_t, qsel, o_c, gates, slc_new, cache_win_t, win_new)


def _pad_tokens(x, axis):
    pad = [(0, 0)] * x.ndim
    pad[axis] = (0, TOK_PAD - x.shape[axis])
    return jnp.pad(x, pad)


def _sample_layer(x2, p, b, s, layer, cache_cmp_t, cache_slc_t, cache_win_t, cache_win, h0r, h0i, cache_mem,
                  page_table, final_g, final_norm):
    past = page_table.shape[1] * PAGE_SIZE
    assert past % SLC_BLOCK == 0 and s <= CMP_STRIDE
    q, kvc, kvs, kvw, u, mq, gb, gn = _in_proj(x2, p['norm1_g'], p['w_in'])
    lh_cache = _cmp_lh_paged(cache_cmp_t, layer, page_table, p['pe'], p['w1'])
    tail = jnp.pad(kvc.reshape(b, s, 2 * KV_WIDTH), ((0, 0), (0, CHUNKS_PER_PAGE * CMP_STRIDE - s), (0, 0)))
    lh_tail = _cmp_lh(tail.reshape(b * CHUNKS_PER_PAGE, CMP_ROW), p['pe'], p['w1'])
    pos = past + jnp.arange(s, dtype=jnp.int32)
    qsel = _pad_tokens(_query_pack(q.reshape(b, s, NSA_WIDTH), pos), 2).reshape(b, N_KV, HPG * TOK_PAD, LANE)
    o_c, idx = _nsa_sample_select(lh_cache, lh_tail.reshape(b, CHUNKS_PER_PAGE, 4 * KV_WIDTH), p['w2'], qsel, past, s)
    gates = gn[:, :3 * N_HEADS].reshape(b, s, 3, N_KV, HPG).transpose(0, 3, 4, 1, 2)
    gates = jnp.pad(_pad_tokens(gates, 3), ((0, 0),) * 4 + ((0, 16 - 3),)).reshape(b, N_KV, HPG * TOK_PAD, 16)
    slc_new = kvs.reshape(b, s, 2 * KV_WIDTH)
    win_new = kvw.reshape(b, s, 2 * KV_WIDTH)
    o = _nsa_sample_attend(idx, page_table, cache_slc_t, cache_win_t, layer, qsel, o_c, gates,
                           _pad_tokens(slc_new, 1), _pad_tokens(win_new, 1), past, s)
    o_a = o.reshape(b, N_KV, HPG, TOK_PAD, HEAD_DIM)[:, :, :, :s].transpose(0, 3, 1, 2, 4).reshape(b * s, NSA_WIDTH)
    o_b, hr, hi = _ssm(_pad_tokens(u.reshape(b, s, SSM_WIDTH), 1), h0r, h0i, p['disc'], p['cdr'], p['cdi'], p['d'],
                       p['w_glu'], p['b_glu'], t_valid=s)
    o_c2 = _mem_attend(_pad_tokens(mq.reshape(b, s, MEM_WIDTH), 1), cache_mem, layer)
    x2 = _merge(x2, o_a, o_b[:, :s].reshape(b * s, SSM_WIDTH), o_c2[:, :s].reshape(b * s, MEM_WIDTH), gb, p['w_o'])
    x2 = _ffn(x2, p['norm2_g'], p['w_ff1'], p['w_ff2'], final_g, final_norm)
    kv_shape = (b, s, 2, N_KV, HEAD_DIM)
    new_win = jnp.concatenate([cache_win, kvw.reshape(kv_shape).astype(cache_win.dtype)], axis=1)[:, s:]
    return x2, dict(cmp=kvc.reshape(kv_shape), slc=kvs.reshape(kv_shape), win=new_win, hr=hr, hi=hi)


def kernel(x_prompt, x_sample, cache_cmp_kv, cache_slc_kv, cache_win_kv, state_ssm_re, state_ssm_im, cache_mem_kv,
           page_table, mem_prompt, norm1_g, w_in, cmp_pe, cmp_w1, cmp_w2, ssm_a_re, ssm_a_im, ssm_log_dt, ssm_b_re,
           ssm_b_im, ssm_c_re, ssm_c_im, ssm_d, w_glu, b_glu, mem_norm_g, w_mem_kv, w_o, norm2_g, w_ff1, w_ff2,
           final_norm_g):
    bp, t, _ = x_prompt.shape
    bs, s, _ = x_sample.shape
    depth = w_in.shape[0]
    xp = x_prompt.reshape(bp * t, D_MODEL)
    xs = x_sample.reshape(bs * s, D_MODEL)
    mem2 = mem_prompt.reshape(bp * N_MEM, D_MODEL)
    cache_cmp_t = _rows_minor(cache_cmp_kv)
    cache_slc_t = _rows_minor(cache_slc_kv)
    cache_win_t = _rows_minor(cache_win_kv)
    cache_mem = cache_mem_kv.reshape(depth, bs, N_MEM, 2 * MEM_WIDTH)
    aux_p, aux_s = [], []
    for l in range(depth):
        p = _layer_params(l, norm1_g, w_in, cmp_pe, cmp_w1, cmp_w2, ssm_a_re, ssm_a_im, ssm_log_dt, ssm_b_re,
                          ssm_b_im, ssm_c_re, ssm_c_im, ssm_d, w_glu, b_glu, mem_norm_g, w_mem_kv, w_o, norm2_g,
                          w_ff1, w_ff2)
        last = l == depth - 1
        xp, ap = _prompt_layer(xp, mem2, p, bp, t, final_norm_g, last)
        xs, a_s = _sample_layer(xs, p, bs, s, l, cache_cmp_t, cache_slc_t, cache_win_t, cache_win_kv[l],
                                state_ssm_re[l], state_ssm_im[l], cache_mem, page_table, final_norm_g, last)
        aux_p.append(ap)
        aux_s.append(a_s)
    stack = lambda aux, key: jnp.stack([a[key] for a in aux])
    return (xp.reshape(bp, t, D_MODEL), xs.reshape(bs, s, D_MODEL),
            stack(aux_p, 'cmp'), stack(aux_s, 'cmp'), stack(aux_p, 'slc'), stack(aux_s, 'slc'),
            stack(aux_p, 'win'), stack(aux_s, 'win'), stack(aux_p, 'hr'), stack(aux_p, 'hi'),
            stack(aux_s, 'hr'), stack(aux_s, 'hi'), stack(aux_p, 'mkv'))
```

```python
import functools
import math

import jax
import jax.numpy as jnp
from jax import lax
from jax.experimental import pallas as pl
from jax.experimental.pallas import tpu as pltpu

F32 = jnp.float32
BF16 = jnp.bfloat16

D_MODEL = 1024
PAGE_SIZE = 128
N_HEADS = 8
HEAD_DIM = 64
N_KV = 2
HPG = N_HEADS // N_KV
CMP_STRIDE = 16
CMP_LEN = 2 * CMP_STRIDE
SLC_BLOCK = 64
SLC_TOPK = 16
WINDOW = 512
Q_BLOCK = 128
SSM_WIDTH = D_MODEL // 2
SSM_GROUP = 16
N_SSM_GROUPS = SSM_WIDTH // SSM_GROUP
SSM_STATE = 64
SSM_FLAT = N_SSM_GROUPS * SSM_STATE
N_MEM = 256
MEM_HEADS = 4
MEM_HEAD_DIM = 128
MEM_WIDTH = MEM_HEADS * MEM_HEAD_DIM
D_FF = 4 * D_MODEL
NSA_WIDTH = N_HEADS * HEAD_DIM
KV_WIDTH = N_KV * HEAD_DIM
MIX_WIDTH = NSA_WIDTH + SSM_WIDTH + MEM_WIDTH
OFF_KV = NSA_WIDTH
OFF_GN = OFF_KV + 6 * KV_WIDTH
OFF_U = OFF_GN + 3 * N_HEADS
OFF_MQ = OFF_U + SSM_WIDTH
OFF_GB = OFF_MQ + MEM_WIDTH
IN_WIDTH = OFF_GB + MIX_WIDTH
EPS = 1e-6
NEG_INF = -1e30
FORCED_SCORE = 1e9

LANE = 128
CMP_ROW = CMP_STRIDE * 2 * KV_WIDTH
SSM_GPB = LANE // SSM_GROUP
SSM_SLABS = SSM_WIDTH // LANE
VMEM_LIMIT = 56 * 2**20

ROW_TILE = 512
SMALL_ROW_TILE = 256
FFN_ROW_TILE = 1024
FFN_COL_TILE = 1024
SLC_KEY_TILE = 1024
SSM_CHUNK = 256
PAGES_PER_STEP = 32
RELAYOUT_SLOTS = 8
GATE_LANES = 16


def _cparams(n_grid, vmem=VMEM_LIMIT):
    return pltpu.CompilerParams(dimension_semantics=("arbitrary",) * n_grid, vmem_limit_bytes=vmem)


def _rms(x, g):
    return x * lax.rsqrt(jnp.mean(x * x, axis=-1, keepdims=True) + EPS) * g


def _dot(a, b):
    return jnp.dot(a, b, preferred_element_type=F32)


def _dot_nt(a, b):
    return lax.dot_general(a, b, (((1,), (1,)), ((), ())), preferred_element_type=F32)


def _masked_probs(s, mask):
    s = jnp.where(mask, s, NEG_INF)
    m = jnp.max(s, axis=-1, keepdims=True)
    e = jnp.where(mask, jnp.exp(s - m), 0.0)
    return e / jnp.maximum(jnp.sum(e, axis=-1, keepdims=True), 1e-30)


def _row_tile(n, pref):
    t = min(n, pref)
    assert n % t == 0
    return t


_SEGS = ((0, 512), (512, 768), (768, 1024), (1024, 1280), (1280, 1792), (1792, 2304), (2304, 3840), (3840, 3968))
_SIGMOID_FROM = 2304
_PACKED_WIDTH = 3968


def _pack_w_in(w):
    gn = jnp.pad(w[:, OFF_GN:OFF_U], ((0, 0), (0, LANE - 3 * N_HEADS)))
    return jnp.concatenate([w[:, :OFF_GN], w[:, OFF_U:], gn], axis=1).astype(BF16)


_PROMPT_ROW_SEGS = (1, 4, 5, 6, 7)


def _inproj_kernel(x_ref, g_ref, w_ref, *o_refs, seq_len):
    h = _rms(x_ref[...], g_ref[...]).astype(BF16)
    z = []
    for a, b in _SEGS:
        z.append(_dot(h, w_ref[:, a:b]))
        if a >= _SIGMOID_FROM:
            z[-1] = jax.nn.sigmoid(z[-1])
    if not seq_len:
        for zi, o_ref in zip(z, o_refs):
            o_ref[...] = zi
        return
    for s, o_ref in zip(_PROMPT_ROW_SEGS, o_refs):
        o_ref[...] = z[s]
    kvt_refs = o_refs[len(_PROMPT_ROW_SEGS):len(_PROMPT_ROW_SEGS) + 3]
    qp_ref, kp_ref, vw_ref = o_refs[len(_PROMPT_ROW_SEGS) + 3:]
    for s, o_ref in zip((1, 2, 3), kvt_refs):
        o_ref[0] = z[s].T
    tm = x_ref.shape[0]
    pos = (pl.program_id(0) * tm) % seq_len + lax.broadcasted_iota(jnp.int32, (tm, 1), 0)
    pos_hi = ((pos // SLC_BLOCK) * SLC_BLOCK).astype(F32)
    pos_lo = (pos % SLC_BLOCK).astype(F32)
    lane = lax.broadcasted_iota(jnp.int32, (1, HEAD_DIM), 1)
    terms = jnp.where(lane == 0, pos_hi, jnp.where(lane == 1, pos_lo, jnp.where(lane < 4, 1.0, 0.0)))
    onehot = jnp.where(pos // SLC_BLOCK == lax.broadcasted_iota(jnp.int32, (1, LANE), 1), 1.0, 0.0)
    q, kvs, kvw = z[0], z[2], z[3]
    head = lambda a, i: a[:, i * HEAD_DIM:(i + 1) * HEAD_DIM]
    for g in range(N_KV):
        kp_ref[g] = jnp.concatenate([onehot, head(kvs, g), terms], axis=1).astype(BF16)
        vw_ref[g] = jnp.concatenate([head(kvs, N_KV + g), head(kvw, N_KV + g), head(kvw, g), terms],
                                    axis=1).astype(BF16)
    for hd in range(N_HEADS):
        slope = 2.0 ** (-8.0 * (hd + 1) / N_HEADS)
        qterms = jnp.where(lane < 2, slope,
                           jnp.where(lane == 2, -slope * pos_hi, jnp.where(lane == 3, -slope * pos_lo, 0.0)))
        qp_ref[hd] = jnp.concatenate([head(q, hd) * (HEAD_DIM ** -0.5), qterms], axis=1).astype(BF16)


def _in_proj(x2, g, w_packed, seq_len=0):
    n = x2.shape[0]
    tm = _row_tile(n, ROW_TILE)
    segs = [_SEGS[s] for s in _PROMPT_ROW_SEGS] if seq_len else _SEGS
    out_specs = [pl.BlockSpec((tm, b - a), lambda i: (i, 0)) for a, b in segs]
    out_shape = [jax.ShapeDtypeStruct((n, b - a), F32) for a, b in segs]
    if seq_len:
        assert seq_len % tm == 0
        per_seq = seq_len // tm
        for _ in range(3):
            out_specs.append(pl.BlockSpec((1, 2 * KV_WIDTH, tm), lambda i: (i // per_seq, 0, i % per_seq)))
            out_shape.append(jax.ShapeDtypeStruct((n // seq_len, 2 * KV_WIDTH, seq_len), F32))
        for lead, width in ((N_HEADS, LANE), (N_KV, 2 * LANE), (N_KV, 2 * LANE)):
            out_specs.append(pl.BlockSpec((lead, tm, width), lambda i: (0, i, 0)))
            out_shape.append(jax.ShapeDtypeStruct((lead, n, width), BF16))
    return pl.pallas_call(
        functools.partial(_inproj_kernel, seq_len=seq_len),
        grid=(n // tm,),
        in_specs=[pl.BlockSpec((tm, D_MODEL), lambda i: (i, 0)),
                  pl.BlockSpec((1, D_MODEL), lambda i: (0, 0)),
                  pl.BlockSpec((D_MODEL, _PACKED_WIDTH), lambda i: (0, 0))],
        out_specs=out_specs,
        out_shape=out_shape,
        compiler_params=_cparams(1),
        name="in_proj",
    )(x2, g.reshape(1, D_MODEL), w_packed)


def _norm_mm_kernel(x_ref, g_ref, w_ref, o_ref):
    o_ref[...] = _dot(_rms(x_ref[...], g_ref[...]).astype(BF16), w_ref[...])


def _norm_mm(x2, g, w_bf16):
    n, k = x2.shape
    m = w_bf16.shape[1]
    tm = _row_tile(n, SMALL_ROW_TILE)
    return pl.pallas_call(
        _norm_mm_kernel,
        grid=(n // tm,),
        in_specs=[pl.BlockSpec((tm, k), lambda i: (i, 0)),
                  pl.BlockSpec((1, k), lambda i: (0, 0)),
                  pl.BlockSpec((k, m), lambda i: (0, 0))],
        out_specs=pl.BlockSpec((tm, m), lambda i: (i, 0)),
        out_shape=jax.ShapeDtypeStruct((n, m), F32),
        compiler_params=_cparams(1),
        name="norm_mm",
    )(x2, g.reshape(1, k), w_bf16)


def _expand_cmp_w1(w1):
    w = w1.reshape(2, 2, CMP_STRIDE, HEAD_DIM, HEAD_DIM).astype(BF16)
    blocks = []
    for kv in range(2):
        for g in range(N_KV):
            off = (kv * N_KV + g) * HEAD_DIM
            blocks.append(jnp.pad(w[kv], ((0, 0), (0, 0), (0, 0), (off, 2 * KV_WIDTH - HEAD_DIM - off))))
    return jnp.concatenate(blocks, axis=2)


def _flat_cmp_pe(pe):
    p = pe.reshape(2, 2, CMP_STRIDE, HEAD_DIM)
    p = jnp.broadcast_to(p.transpose(1, 2, 0, 3)[:, :, :, None, :], (2, CMP_STRIDE, 2, N_KV, HEAD_DIM))
    return p.reshape(2, 1, CMP_ROW)


def _expand_cmp_w2(w2):
    eye2 = jnp.eye(2, dtype=w2.dtype)
    big = jnp.einsum('kde,kK,gG->kgdGKe', w2, eye2, eye2)
    return big.reshape(2 * KV_WIDTH, 2 * KV_WIDTH).astype(BF16)


def _cmp_lohi(x, pe_ref, w_ref):
    lo = _dot((x + pe_ref[0]).astype(BF16), w_ref[0])
    hi = _dot((x + pe_ref[1]).astype(BF16), w_ref[1])
    return jnp.concatenate([lo, hi], axis=1)


def _cmp_lh_kernel(x_ref, pe_ref, w_ref, o_ref):
    o_ref[...] = _cmp_lohi(x_ref[...], pe_ref, w_ref)


def _cmp_lh(x2, pe_flat, w1_big):
    n = x2.shape[0]
    tm = _row_tile(n, SMALL_ROW_TILE)
    return pl.pallas_call(
        _cmp_lh_kernel,
        grid=(n // tm,),
        in_specs=[pl.BlockSpec((tm, CMP_ROW), lambda i: (i, 0)),
                  pl.BlockSpec((2, 1, CMP_ROW), lambda i: (0, 0, 0)),
                  pl.BlockSpec((2, CMP_ROW, 2 * KV_WIDTH), lambda i: (0, 0, 0))],
        out_specs=pl.BlockSpec((tm, 4 * KV_WIDTH), lambda i: (i, 0)),
        out_shape=jax.ShapeDtypeStruct((n, 4 * KV_WIDTH), F32),
        compiler_params=_cparams(1),
        name="cmp_lh",
    )(x2, pe_flat, w1_big.reshape(2, CMP_ROW, 2 * KV_WIDTH))


def _cmp_combine(lh, w2_big):
    r = lh.shape[0]
    hi_next = pltpu.roll(lh[:, 2 * KV_WIDTH:], r - 1, 0)
    return _dot(jax.nn.gelu(lh[:, :2 * KV_WIDTH] + hi_next).astype(BF16), w2_big)


def _cmp_combine_kernel(lh_ref, w2_ref, o_ref):
    o_ref[0] = _cmp_combine(lh_ref[0], w2_ref[...])


def _cmp_tokens_prompt(lh, w2_big):
    b, r, _ = lh.shape
    return pl.pallas_call(
        _cmp_combine_kernel,
        grid=(b,),
        in_specs=[pl.BlockSpec((1, r, 4 * KV_WIDTH), lambda i: (i, 0, 0)),
                  pl.BlockSpec((2 * KV_WIDTH, 2 * KV_WIDTH), lambda i: (0, 0))],
        out_specs=pl.BlockSpec((1, r, 2 * KV_WIDTH), lambda i: (i, 0, 0)),
        out_shape=jax.ShapeDtypeStruct((b, r, 2 * KV_WIDTH), F32),
        compiler_params=_cparams(1),
        name="cmp_combine",
    )(lh, w2_big)


def _alibi_slopes():
    return jnp.exp2(-8.0 * (jnp.arange(N_HEADS, dtype=F32) + 1.0) / N_HEADS)


def _query_pack(q, pos):
    b, t, _ = q.shape
    qh = q.reshape(b, t, N_HEADS, HEAD_DIM).transpose(0, 2, 1, 3) * (HEAD_DIM ** -0.5)
    slope = _alibi_slopes()[:, None]
    pos_hi = ((pos // SLC_BLOCK) * SLC_BLOCK).astype(F32)[None, :]
    pos_lo = (pos % SLC_BLOCK).astype(F32)[None, :]
    ones = jnp.ones((1, t), F32)
    c = jnp.stack([slope * ones, slope * ones, -slope * pos_hi, -slope * pos_lo], axis=-1)
    c = jnp.pad(c, ((0, 0), (0, 0), (0, HEAD_DIM - 4)))
    c = jnp.broadcast_to(c[None], (b, N_HEADS, t, HEAD_DIM))
    return jnp.concatenate([qh, c], axis=-1).astype(BF16)


def _key_consts(kpos):
    hi = ((kpos // SLC_BLOCK) * SLC_BLOCK).astype(F32)
    lo = (kpos % SLC_BLOCK).astype(F32)
    one = jnp.ones_like(hi)
    return jnp.pad(jnp.stack([hi, lo, one, one], axis=-1), ((0, 0), (0, HEAD_DIM - 4)))


def _split3(x):
    hi = x.astype(BF16)
    r1 = x - hi.astype(F32)
    mid = r1.astype(BF16)
    lo = (r1 - mid.astype(F32)).astype(BF16)
    return hi, mid, lo


def _pool_blocks_t(ps, pool_t):
    hi, mid, lo = _split3(ps)
    return _dot_nt(pool_t, hi) + _dot_nt(pool_t, mid) + _dot_nt(pool_t, lo)


def _topk_select_cols(score, blk_f, k):
    sel = jnp.zeros(score.shape, F32)
    picks = []
    for _ in range(k):
        m = jnp.max(score, axis=0, keepdims=True)
        idx = jnp.min(jnp.where(score == m, blk_f, 1e9), axis=0, keepdims=True)
        pick = blk_f == idx
        sel = jnp.where(pick, 1.0, sel)
        score = jnp.where(pick, -3e38, score)
        picks.append(idx)
    return sel, picks


def _add_token_bias(s, bias):
    rows, n = s.shape
    q = bias.shape[0]
    return (s.reshape(rows // q, q, n) + bias[None]).reshape(rows, n)


def _biased_probs(s, bias, row_any=None):
    x = _add_token_bias(s, bias)
    e = jnp.exp(x - jnp.max(x, axis=-1, keepdims=True))
    inv = 1.0 / jnp.maximum(jnp.sum(e, axis=-1, keepdims=True), 1e-30)
    if row_any is not None:
        inv = jnp.where(row_any, inv, 0.0)
    return e * inv


def _nsa_prompt_kernel(qp_ref, gn_ref, ckp_ref, kp_ref, vw_ref, pool_ref, o_ref, *, tk):
    j = pl.program_id(2)
    start = j * Q_BLOCK
    rows = HPG * Q_BLOCK
    qp = qp_ref[...].reshape(rows, LANE)
    pos_t = start + lax.broadcasted_iota(jnp.int32, (Q_BLOCK, 1), 0)
    row = lax.broadcasted_iota(jnp.int32, (rows, 1), 0)

    ckp = ckp_ref[0, 0]
    n_c = ckp.shape[0]
    c_end = lax.broadcasted_iota(jnp.int32, (1, n_c), 1) * CMP_STRIDE + (CMP_LEN - 1)
    bias_c = jnp.where(c_end <= pos_t, 0.0, NEG_INF)
    any_c = start + (row & (Q_BLOCK - 1)) >= CMP_LEN - 1
    p_c = _biased_probs(_dot_nt(qp, ckp[:, :LANE]), bias_c, any_c)
    o_c = _dot(p_c.astype(BF16), ckp[:, LANE:LANE + HEAD_DIM])
    ps = p_c[0:Q_BLOCK]
    for h in range(1, HPG):
        ps = ps + p_c[h * Q_BLOCK:(h + 1) * Q_BLOCK]

    wlen = WINDOW + Q_BLOCK
    woff = pl.multiple_of(jnp.maximum(start - WINDOW, 0), Q_BLOCK)
    d_w = pos_t - (woff + lax.broadcasted_iota(jnp.int32, (1, wlen), 1))
    bias_w = jnp.where((d_w >= 0) & (d_w <= WINDOW), 0.0, NEG_INF)
    p_w = _biased_probs(_dot_nt(qp, vw_ref[0, 0, pl.ds(woff, wlen), LANE:2 * LANE]), bias_w)
    o_w = _dot(p_w.astype(BF16), vw_ref[0, 0, pl.ds(woff, wlen), HEAD_DIM:2 * HEAD_DIM])

    imp_t = _pool_blocks_t(ps, pool_ref[...])
    blk = lax.broadcasted_iota(jnp.int32, (LANE, 1), 0)
    cur = (start + lax.broadcasted_iota(jnp.int32, (1, Q_BLOCK), 1)) // SLC_BLOCK
    valid = blk <= cur
    forced = (blk == 0) | (blk == cur) | (blk == cur - 1)
    score = jnp.where(valid & ~forced, imp_t, -1.0)
    sel_t, _ = _topk_select_cols(score, blk.astype(F32), SLC_TOPK - 3)
    mb = jnp.where(((sel_t > 0.0) | forced) & valid, 0.0, NEG_INF).T.astype(BF16)
    qs = jnp.concatenate([jnp.concatenate([mb] * HPG, axis=0), qp], axis=1)

    def tile(i, state, diag):
        m, l, acc = state
        off = pl.multiple_of(i * tk, tk)
        s = _dot_nt(qs, kp_ref[0, 0, pl.ds(off, tk), :])
        if diag:
            kpos = off + lax.broadcasted_iota(jnp.int32, (1, tk), 1)
            s = _add_token_bias(s, jnp.where(kpos <= pos_t, 0.0, NEG_INF))
        m2 = jnp.maximum(m, jnp.max(s, axis=-1, keepdims=True))
        a = jnp.exp(m - m2)
        e = jnp.exp(s - m2)
        l = a * l + jnp.sum(e, axis=-1, keepdims=True)
        acc = a * acc + _dot(e.astype(BF16), vw_ref[0, 0, pl.ds(off, tk), 0:HEAD_DIM])
        return m2, l, acc

    n_full = start // tk
    carry = (jnp.full((rows, 1), NEG_INF, F32), jnp.zeros((rows, 1), F32), jnp.zeros((rows, HEAD_DIM), F32))
    carry = lax.fori_loop(0, n_full, lambda i, c: tile(i, c, False), carry)
    _, l_s, acc_s = tile(n_full, carry, True)
    o_s = acc_s / jnp.maximum(l_s, 1e-30)

    gn = gn_ref[0, 0]
    outs = []
    for h in range(HPG):
        sl = slice(h * Q_BLOCK, (h + 1) * Q_BLOCK)
        outs.append(gn[:, h:h + 1] * o_c[sl] + gn[:, HPG + h:HPG + h + 1] * o_s[sl]
                    + gn[:, 2 * HPG + h:2 * HPG + h + 1] * o_w[sl])
    o_ref[0] = jnp.concatenate(outs, axis=1)


def _gate_pack(gn, b, t):
    g = gn[:, :3 * N_HEADS].reshape(b, t, 3, N_KV, HPG).transpose(0, 3, 1, 2, 4).reshape(b, N_KV, t, 3 * HPG)
    return jnp.pad(g, ((0, 0), (0, 0), (0, 0), (0, GATE_LANES - 3 * HPG)))


def _nsa_prompt(qp, kp, vw, gn, ckv, b, t):
    n_b = t // SLC_BLOCK
    assert n_b <= LANE and t % Q_BLOCK == 0 and t >= WINDOW + Q_BLOCK
    tk = min(SLC_KEY_TILE, t)
    qp = qp.reshape(N_HEADS, b, t, LANE)
    kp = kp.reshape(N_KV, b, t, 2 * LANE)
    vw = vw.reshape(N_KV, b, t, 2 * LANE)
    n_c = t // CMP_STRIDE
    c4 = ckv.reshape(b, n_c, N_KV, 2, HEAD_DIM).transpose(3, 0, 2, 1, 4)
    c_end = jnp.arange(n_c, dtype=jnp.int32) * CMP_STRIDE + (CMP_LEN - 1)
    ckp = jnp.concatenate([c4[0], jnp.broadcast_to(_key_consts(c_end), c4[0].shape), c4[1], jnp.zeros_like(c4[1])],
                          axis=-1).astype(BF16)
    pool = (jnp.arange(LANE)[:, None] == jnp.arange(n_c)[None, :] // (SLC_BLOCK // CMP_STRIDE)).astype(BF16)
    gn2 = _gate_pack(gn, b, t)
    return pl.pallas_call(
        functools.partial(_nsa_prompt_kernel, tk=tk),
        grid=(b, N_KV, t // Q_BLOCK),
        in_specs=[pl.BlockSpec((HPG, 1, Q_BLOCK, LANE), lambda i, g, j: (g, i, j, 0)),
                  pl.BlockSpec((1, 1, Q_BLOCK, GATE_LANES), lambda i, g, j: (i, g, j, 0)),
                  pl.BlockSpec((1, 1, n_c, 2 * LANE), lambda i, g, j: (i, g, 0, 0)),
                  pl.BlockSpec((1, 1, t, 2 * LANE), lambda i, g, j: (g, i, 0, 0)),
                  pl.BlockSpec((1, 1, t, 2 * LANE), lambda i, g, j: (g, i, 0, 0)),
                  pl.BlockSpec((LANE, n_c), lambda i, g, j: (0, 0))],
        out_specs=pl.BlockSpec((1, Q_BLOCK, HPG * HEAD_DIM), lambda i, g, j: (i, j, g)),
        out_shape=jax.ShapeDtypeStruct((b, t, NSA_WIDTH), F32),
        compiler_params=_cparams(3),
        name="nsa_prompt",
    )(qp, gn2, ckp, kp, vw, pool).reshape(b * t, NSA_WIDTH)


def _mem_attend_kernel(q_ref, kv_ref, o_ref):
    q = q_ref[0]
    kv = kv_ref[...].reshape(N_MEM, 2 * MEM_WIDTH)
    scale = MEM_HEAD_DIM ** -0.5
    for h in range(MEM_HEADS):
        sl = slice(h * MEM_HEAD_DIM, (h + 1) * MEM_HEAD_DIM)
        k = kv[:, sl].astype(BF16)
        v = kv[:, MEM_WIDTH + h * MEM_HEAD_DIM:MEM_WIDTH + (h + 1) * MEM_HEAD_DIM].astype(BF16)
        s = _dot_nt(q[:, sl].astype(BF16), k) * scale
        e = jnp.exp(s - jnp.max(s, axis=-1, keepdims=True))
        p = e / jnp.sum(e, axis=-1, keepdims=True)
        o_ref[0, :, sl] = _dot(p.astype(BF16), v)


def _mem_attend(mq, mkv, layer=None):
    b, t, _ = mq.shape
    tm = _row_tile(t, ROW_TILE)
    if layer is None:
        kv_spec = pl.BlockSpec((1, N_MEM, 2 * MEM_WIDTH), lambda i, j: (i, 0, 0))
    else:
        kv_spec = pl.BlockSpec((1, 1, N_MEM, 2 * MEM_WIDTH), lambda i, j: (layer, i, 0, 0))
    return pl.pallas_call(
        _mem_attend_kernel,
        grid=(b, t // tm),
        in_specs=[pl.BlockSpec((1, tm, MEM_WIDTH), lambda i, j: (i, j, 0)), kv_spec],
        out_specs=pl.BlockSpec((1, tm, MEM_WIDTH), lambda i, j: (i, j, 0)),
        out_shape=jax.ShapeDtypeStruct((b, t, MEM_WIDTH), F32),
        compiler_params=_cparams(2),
        name="mem_attend",
    )(mq, mkv)


def _ssm_disc_kernel(ar_ref, ai_ref, ldt_ref, br_ref, bi_ref, abr_ref, abi_ref, bbr_ref, bbi_ref):
    ar = ar_ref[...]
    ai = ai_ref[...]
    dt = jnp.exp(ldt_ref[...])
    mag = jnp.exp(dt * ar)
    abr = mag * jnp.cos(dt * ai)
    abi = mag * jnp.sin(dt * ai)
    den = ar * ar + ai * ai
    fr = ((abr - 1.0) * ar + abi * ai) / den
    fi = (abi * ar - (abr - 1.0) * ai) / den
    br = br_ref[...]
    bi = bi_ref[...]
    abr_ref[...] = abr
    abi_ref[...] = abi
    bbr_ref[...] = fr * br - fi * bi
    bbi_ref[...] = fr * bi + fi * br


def _ssm_discretise(a_re, a_im, log_dt, b_re, b_im):
    rep = lambda x: jnp.repeat(x, SSM_GROUP, axis=0)
    tb = lambda x: x.transpose(0, 2, 1).reshape(SSM_WIDTH, SSM_STATE)
    shp = jax.ShapeDtypeStruct((SSM_WIDTH, SSM_STATE), F32)
    abr, abi, bbr, bbi = pl.pallas_call(
        _ssm_disc_kernel, out_shape=[shp] * 4, name="ssm_disc",
    )(rep(a_re), rep(a_im), rep(jnp.broadcast_to(log_dt[:, None], a_re.shape)), tb(b_re), tb(b_im))
    eye = jnp.eye(SSM_GPB, dtype=F32)

    def in_map(bb):
        bb = bb.reshape(SSM_SLABS, SSM_GPB, SSM_GROUP, SSM_STATE)
        return jnp.einsum('jgcp,gG->jgcGp', bb, eye).reshape(SSM_SLABS, LANE, SSM_GPB * SSM_STATE).astype(BF16)

    flat = lambda x: x[::SSM_GROUP].reshape(1, SSM_FLAT)
    return flat(abr), flat(abi), in_map(bbr), in_map(bbi)


def _ssm_out_map(c):
    eye = jnp.eye(SSM_GPB, dtype=F32)
    c = c.reshape(SSM_SLABS, SSM_GPB, SSM_GROUP, SSM_STATE)
    return jnp.einsum('jgcp,gG->jgpGc', c, eye).reshape(SSM_SLABS, SSM_GPB * SSM_STATE, LANE).astype(BF16)


def _ssm_kernel(u_ref, h0r_ref, h0i_ref, abr_ref, abi_ref, bdr_ref, bdi_ref, cdr_ref, cdi_ref, d_ref,
                wg_ref, bg_ref, o_ref, hr_ref, hi_ref, sr_ref, si_ref, st_ref, *, n_scan):
    c_idx = pl.program_id(1)
    slab = SSM_GPB * SSM_STATE

    @pl.when(c_idx == 0)
    def _():
        st_ref[0:1, :] = h0r_ref[0]
        st_ref[1:2, :] = h0i_ref[0]

    u = u_ref[0]
    ub = u.astype(BF16)
    for j in range(SSM_SLABS):
        uj = ub[:, j * LANE:(j + 1) * LANE]
        sr_ref[:, j * slab:(j + 1) * slab] = _dot(uj, bdr_ref[j])
        si_ref[:, j * slab:(j + 1) * slab] = _dot(uj, bdi_ref[j])

    ar = abr_ref[...]
    ai = abi_ref[...]

    def step(t, carry):
        hr, hi = carry
        nr = ar * hr - ai * hi + sr_ref[pl.ds(t, 1), :]
        ni = ar * hi + ai * hr + si_ref[pl.ds(t, 1), :]
        sr_ref[pl.ds(t, 1), :] = nr
        si_ref[pl.ds(t, 1), :] = ni
        return nr, ni

    hr, hi = lax.fori_loop(0, n_scan, step, (st_ref[0:1, :], st_ref[1:2, :]))
    st_ref[0:1, :] = hr
    st_ref[1:2, :] = hi
    hr_ref[0] = hr
    hi_ref[0] = hi

    ys = []
    for j in range(SSM_SLABS):
        hrj = sr_ref[:, j * slab:(j + 1) * slab].astype(BF16)
        hij = si_ref[:, j * slab:(j + 1) * slab].astype(BF16)
        ys.append(_dot(hrj, cdr_ref[j]) - _dot(hij, cdi_ref[j]))
    y = jnp.concatenate(ys, axis=1) + d_ref[...] * u
    y = jax.nn.gelu(y)
    o_ref[0] = y * jax.nn.sigmoid(_dot(y.astype(BF16), wg_ref[...]) + bg_ref[...])


def _ssm(u, h0r, h0i, disc, cdr, cdi, d, w_glu, b_glu, t_valid=None):
    b, t, _ = u.shape
    tc = _row_tile(t, SSM_CHUNK)
    n_scan = tc if t_valid is None else t_valid
    assert t_valid is None or t == tc
    abr, abi, bdr, bdi = disc
    full = lambda a: pl.BlockSpec(a.shape, lambda i, c: (0,) * a.ndim)
    state_spec = pl.BlockSpec((1, 1, SSM_FLAT), lambda i, c: (i, 0, 0))
    args = (abr, abi, bdr, bdi, cdr, cdi, d.reshape(1, SSM_WIDTH), w_glu.astype(BF16), b_glu.reshape(1, SSM_WIDTH))
    o, hr, hi = pl.pallas_call(
        functools.partial(_ssm_kernel, n_scan=n_scan),
        grid=(b, t // tc),
        in_specs=[pl.BlockSpec((1, tc, SSM_WIDTH), lambda i, c: (i, c, 0)), state_spec, state_spec]
                 + [full(a) for a in args],
        out_specs=[pl.BlockSpec((1, tc, SSM_WIDTH), lambda i, c: (i, c, 0)), state_spec, state_spec],
        out_shape=[jax.ShapeDtypeStruct((b, t, SSM_WIDTH), F32),
                   jax.ShapeDtypeStruct((b, 1, SSM_FLAT), F32),
                   jax.ShapeDtypeStruct((b, 1, SSM_FLAT), F32)],
        scratch_shapes=[pltpu.VMEM((tc, SSM_FLAT), F32), pltpu.VMEM((tc, SSM_FLAT), F32),
                        pltpu.VMEM((8, SSM_FLAT), F32)],
        compiler_params=_cparams(2),
        name="ssm",
    )(u, h0r.reshape(b, 1, SSM_FLAT), h0i.reshape(b, 1, SSM_FLAT), *args)
    shp = (b, N_SSM_GROUPS, SSM_STATE)
    return o, hr.reshape(shp), hi.reshape(shp)


def _merge_kernel(x_ref, oa_ref, ob_ref, oc_ref, gb_ref, w_ref, o_ref):
    acc = x_ref[...]
    for i, r in enumerate((oa_ref, ob_ref, oc_ref)):
        w = r.shape[1]
        off = sum((NSA_WIDTH, SSM_WIDTH, MEM_WIDTH)[:i])
        mixed = (r[...] * gb_ref[:, off:off + w]).astype(BF16)
        acc = acc + _dot(mixed, w_ref[off:off + w, :])
    o_ref[...] = acc


def _merge(x2, o_a, o_b, o_c, gb, w_o_bf16):
    n = x2.shape[0]
    tm = _row_tile(n, ROW_TILE)
    row = lambda w: pl.BlockSpec((tm, w), lambda i: (i, 0))
    return pl.pallas_call(
        _merge_kernel,
        grid=(n // tm,),
        in_specs=[row(D_MODEL), row(NSA_WIDTH), row(SSM_WIDTH), row(MEM_WIDTH), row(MIX_WIDTH),
                  pl.BlockSpec((MIX_WIDTH, D_MODEL), lambda i: (0, 0))],
        out_specs=row(D_MODEL),
        out_shape=jax.ShapeDtypeStruct((n, D_MODEL), F32),
        compiler_params=_cparams(1),
        name="merge",
    )(x2, o_a, o_b, o_c, gb, w_o_bf16)


def _ffn_kernel(x_ref, g_ref, w1_ref, w2_ref, fg_ref, o_ref, h_ref, acc_ref, *, final_norm):
    k = pl.program_id(1)

    @pl.when(k == 0)
    def _():
        h_ref[...] = _rms(x_ref[...], g_ref[...]).astype(BF16)
        acc_ref[...] = jnp.zeros_like(acc_ref)

    a = jnp.square(jnp.maximum(_dot(h_ref[...], w1_ref[...]), 0.0))
    acc_ref[...] += _dot(a.astype(BF16), w2_ref[...])

    @pl.when(k == pl.num_programs(1) - 1)
    def _():
        y = x_ref[...] + acc_ref[...]
        if final_norm:
            y = _rms(y, fg_ref[...])
        o_ref[...] = y


def _ffn(x2, g, w1_bf16, w2_bf16, final_g, final_norm):
    n = x2.shape[0]
    tm = _row_tile(n, FFN_ROW_TILE)
    tf = FFN_COL_TILE
    return pl.pallas_call(
        functools.partial(_ffn_kernel, final_norm=final_norm),
        grid=(n // tm, D_FF // tf),
        in_specs=[pl.BlockSpec((tm, D_MODEL), lambda i, k: (i, 0)),
                  pl.BlockSpec((1, D_MODEL), lambda i, k: (0, 0)),
                  pl.BlockSpec((D_MODEL, tf), lambda i, k: (0, k)),
                  pl.BlockSpec((tf, D_MODEL), lambda i, k: (k, 0)),
                  pl.BlockSpec((1, D_MODEL), lambda i, k: (0, 0))],
        out_specs=pl.BlockSpec((tm, D_MODEL), lambda i, k: (i, 0)),
        out_shape=jax.ShapeDtypeStruct((n, D_MODEL), F32),
        scratch_shapes=[pltpu.VMEM((tm, D_MODEL), BF16), pltpu.VMEM((tm, D_MODEL), F32)],
        compiler_params=_cparams(2),
        name="ffn",
    )(x2, g.reshape(1, D_MODEL), w1_bf16, w2_bf16, final_g.reshape(1, D_MODEL))


def _layer_params(l, norm1_g, w_in, cmp_pe, cmp_w1, cmp_w2, ssm_a_re, ssm_a_im, ssm_log_dt, ssm_b_re, ssm_b_im,
                  ssm_c_re, ssm_c_im, ssm_d, w_glu, b_glu, mem_norm_g, w_mem_kv, w_o, norm2_g, w_ff1, w_ff2):
    return dict(
        norm1_g=norm1_g[l], w_in=_pack_w_in(w_in[l]),
        pe=_flat_cmp_pe(cmp_pe[l]), w1=_expand_cmp_w1(cmp_w1[l]), w2=_expand_cmp_w2(cmp_w2[l]),
        disc=_ssm_discretise(ssm_a_re[l], ssm_a_im[l], ssm_log_dt[l], ssm_b_re[l], ssm_b_im[l]),
        cdr=_ssm_out_map(ssm_c_re[l]), cdi=_ssm_out_map(ssm_c_im[l]), d=ssm_d[l],
        w_glu=w_glu[l], b_glu=b_glu[l], mem_norm_g=mem_norm_g[l], w_mem_kv=w_mem_kv[l].astype(BF16),
        w_o=w_o[l].astype(BF16), norm2_g=norm2_g[l], w_ff1=w_ff1[l].astype(BF16), w_ff2=w_ff2[l].astype(BF16))


def _prompt_layer(x2, mem2, p, b, t, final_g, final_norm):
    kvc, u, mq, gb, gn, kvc_t, kvs_t, kvw_t, qp, kp, vw = _in_proj(x2, p['norm1_g'], p['w_in'], seq_len=t)
    lh = _cmp_lh(kvc.reshape(b * t // CMP_STRIDE, CMP_ROW), p['pe'], p['w1'])
    ckv = _cmp_tokens_prompt(lh.reshape(b, t // CMP_STRIDE, 4 * KV_WIDTH), p['w2'])
    o_a = _nsa_prompt(qp, kp, vw, gn, ckv, b, t)
    zero = jnp.zeros((b, N_SSM_GROUPS, SSM_STATE), F32)
    o_b, hr, hi = _ssm(u.reshape(b, t, SSM_WIDTH), zero, zero, p['disc'], p['cdr'], p['cdi'], p['d'],
                       p['w_glu'], p['b_glu'])
    mkv = _norm_mm(mem2, p['mem_norm_g'], p['w_mem_kv'])
    o_c = _mem_attend(mq.reshape(b, t, MEM_WIDTH), mkv.reshape(b, N_MEM, 2 * MEM_WIDTH))
    x2 = _merge(x2, o_a, o_b.reshape(b * t, SSM_WIDTH), o_c.reshape(b * t, MEM_WIDTH), gb, p['w_o'])
    x2 = _ffn(x2, p['norm2_g'], p['w_ff1'], p['w_ff2'], final_g, final_norm)
    rows_major = lambda a: a.reshape(b, 2, N_KV, HEAD_DIM, a.shape[-1]).transpose(0, 4, 1, 2, 3)
    wl = min(WINDOW, t)
    return x2, dict(cmp=rows_major(kvc_t), slc=rows_major(kvs_t), win=rows_major(kvw_t[:, :, t - wl:]), hr=hr, hi=hi,
                    mkv=mkv.reshape(b, N_MEM, 2, MEM_HEADS, MEM_HEAD_DIM))


CHUNKS_PER_PAGE = PAGE_SIZE // CMP_STRIDE


def _cmp_lh_paged_kernel(pt_ref, cache_ref, pe_ref, w_ref, o_ref, buf_ref, stage_ref, xl_ref, sem, *,
                         pages, layer, n_s, n_groups):
    step = pl.program_id(0)
    slot = step % 2

    def page_copy(grp, i, sl):
        page = pt_ref[grp // n_s, (grp % n_s) * pages + i]
        return pltpu.make_async_copy(cache_ref.at[layer, page], buf_ref.at[sl, i], sem.at[sl, i])

    @pl.when(step == 0)
    def _():
        for i in range(pages):
            page_copy(step, i, slot).start()

    @pl.when(step + 1 < n_groups)
    def _():
        for i in range(pages):
            page_copy(step + 1, i, 1 - slot).start()

    for i in range(pages):
        page_copy(step, i, slot).wait()

    n_stage = stage_ref.shape[0]

    def relayout(it, carry):
        for u in range(n_stage):
            i = it * n_stage + u
            xt = buf_ref[slot, i].reshape(2 * KV_WIDTH, PAGE_SIZE).T
            base = pl.multiple_of(i * CHUNKS_PER_PAGE, CHUNKS_PER_PAGE)
            for h in range(2 * KV_WIDTH // LANE):
                stage_ref[u, h] = xt[:, h * LANE:(h + 1) * LANE]
                for l in range(CMP_STRIDE):
                    xl_ref[l, pl.ds(base, CHUNKS_PER_PAGE), h * LANE:(h + 1) * LANE] = (
                        stage_ref[u, h, pl.ds(l, CHUNKS_PER_PAGE, stride=CMP_STRIDE), :])
        return carry

    lax.fori_loop(0, pages // n_stage, relayout, 0)
    lo = jnp.zeros((pages * CHUNKS_PER_PAGE, 2 * KV_WIDTH), F32)
    hi = jnp.zeros((pages * CHUNKS_PER_PAGE, 2 * KV_WIDTH), F32)
    for l in range(CMP_STRIDE):
        x = xl_ref[l]
        lo = lo + _dot((x + pe_ref[0, l]).astype(BF16), w_ref[0, l])
        hi = hi + _dot((x + pe_ref[1, l]).astype(BF16), w_ref[1, l])
    o_ref[0] = jnp.concatenate([lo, hi], axis=1)


def _cmp_lh_paged(cache_t, layer, page_table, pe_flat, w1_big):
    b, n_pages = page_table.shape
    pages = math.gcd(n_pages, PAGES_PER_STEP)
    rows = pages * CHUNKS_PER_PAGE
    n_s = n_pages // pages
    n_groups = b * n_s
    page_shape = cache_t.shape[2:]

    grid_spec = pltpu.PrefetchScalarGridSpec(
        num_scalar_prefetch=1,
        grid=(n_groups,),
        in_specs=[pl.BlockSpec(memory_space=pl.ANY),
                  pl.BlockSpec((2, CMP_STRIDE, 1, 2 * KV_WIDTH), lambda t, pt: (0, 0, 0, 0)),
                  pl.BlockSpec((2, CMP_STRIDE, 2 * KV_WIDTH, 2 * KV_WIDTH), lambda t, pt: (0, 0, 0, 0))],
        out_specs=pl.BlockSpec((1, rows, 4 * KV_WIDTH), lambda t, pt: (t // n_s, t % n_s, 0)),
        scratch_shapes=[pltpu.VMEM((2, pages) + page_shape, F32),
                        pltpu.VMEM((math.gcd(pages, RELAYOUT_SLOTS), 2 * KV_WIDTH // LANE, PAGE_SIZE, LANE), F32),
                        pltpu.VMEM((CMP_STRIDE, rows, 2 * KV_WIDTH), F32),
                        pltpu.SemaphoreType.DMA((2, pages))])
    return pl.pallas_call(
        functools.partial(_cmp_lh_paged_kernel, pages=pages, layer=layer, n_s=n_s, n_groups=n_groups),
        grid_spec=grid_spec,
        out_shape=jax.ShapeDtypeStruct((b, n_pages * CHUNKS_PER_PAGE, 4 * KV_WIDTH), F32),
        compiler_params=_cparams(1),
        name="cmp_lh_paged",
    )(page_table, cache_t, pe_flat.reshape(2, CMP_STRIDE, 1, 2 * KV_WIDTH), w1_big)


TOK_PAD = 8


def _nsa_sample_select_kernel(lhc_ref, lht_ref, w2_ref, q_ref, pool_ref, oc_ref, idx_ref, lh_ref, *, past, n_tok):
    n_cache = lhc_ref.shape[1]
    n_tail = lht_ref.shape[1]
    r = lh_ref.shape[0]
    lh_ref[0:n_cache, :] = lhc_ref[0]
    lh_ref[n_cache:n_cache + n_tail, :] = lht_ref[0]
    lh_ref[n_cache + n_tail:r, :] = jnp.zeros((r - n_cache - n_tail, lh_ref.shape[1]), F32)
    ckv = _cmp_combine(lh_ref[...], w2_ref[...])

    rows = HPG * TOK_PAD
    row = lax.broadcasted_iota(jnp.int32, (rows, 1), 0)
    pos_q = past + (row & (TOK_PAD - 1))
    ci = lax.broadcasted_iota(jnp.int32, (1, r), 1)
    d_c = pos_q - (ci * CMP_STRIDE + (CMP_LEN - 1))
    n_blk = pool_ref.shape[0]
    ps_all = []
    for g in range(N_KV):
        qg = q_ref[0, g]
        q = qg[:, :HEAD_DIM]
        slope = qg[:, HEAD_DIM:HEAD_DIM + 1].astype(F32)
        ck = ckv[:, g * 2 * HEAD_DIM:g * 2 * HEAD_DIM + HEAD_DIM].astype(BF16)
        cv = ckv[:, g * 2 * HEAD_DIM + HEAD_DIM:(g + 1) * 2 * HEAD_DIM].astype(BF16)
        s = _dot_nt(q, ck) - slope * d_c.astype(F32)
        p_c = _masked_probs(s, d_c >= 0)
        oc_ref[0, g] = _dot(p_c.astype(BF16), cv)
        ps = p_c[0:TOK_PAD]
        for h in range(1, HPG):
            ps = ps + p_c[h * TOK_PAD:(h + 1) * TOK_PAD]
        ps_all.append(ps)
    ps = jnp.concatenate(ps_all + [jnp.zeros((LANE - N_KV * TOK_PAD, r), F32)], axis=0)
    imp_t = _pool_blocks_t(ps, pool_ref[...])
    blk = lax.broadcasted_iota(jnp.int32, (n_blk, 1), 0)
    cur = (past + (lax.broadcasted_iota(jnp.int32, (1, LANE), 1) & (TOK_PAD - 1))) // SLC_BLOCK
    valid = blk <= cur
    forced = (blk == 0) | (blk == cur) | (blk == cur - 1)
    score = jnp.where(valid, jnp.where(forced, FORCED_SCORE, imp_t), -1.0)
    _, picks = _topk_select_cols(score, blk.astype(F32), SLC_TOPK)
    for k, pk in enumerate(picks):
        idx_ref[0, k:k + 1, :] = pk.astype(jnp.int32)


def _nsa_sample_select(lh_cache, lh_tail, w2_big, qsel, past, n_tok):
    b, n_cache, _ = lh_cache.shape
    n_tail = lh_tail.shape[1]
    r = -(-(n_cache + n_tail) // LANE) * LANE
    n_b = past // SLC_BLOCK + 1
    assert n_b >= SLC_TOPK
    n_blk = -(-n_b // 8) * 8
    pool = (jnp.arange(n_blk)[:, None] == jnp.arange(r)[None, :] // (SLC_BLOCK // CMP_STRIDE)).astype(BF16)
    rows = HPG * TOK_PAD
    o_c, idx = pl.pallas_call(
        functools.partial(_nsa_sample_select_kernel, past=past, n_tok=n_tok),
        grid=(b,),
        in_specs=[pl.BlockSpec((1, n_cache, 4 * KV_WIDTH), lambda i: (i, 0, 0)),
                  pl.BlockSpec((1, n_tail, 4 * KV_WIDTH), lambda i: (i, 0, 0)),
                  pl.BlockSpec((2 * KV_WIDTH, 2 * KV_WIDTH), lambda i: (0, 0)),
                  pl.BlockSpec((1, N_KV, rows, LANE), lambda i: (i, 0, 0, 0)),
                  pl.BlockSpec((n_blk, r), lambda i: (0, 0))],
        out_specs=[pl.BlockSpec((1, N_KV, rows, HEAD_DIM), lambda i: (i, 0, 0, 0)),
                   pl.BlockSpec((1, SLC_TOPK, LANE), lambda i: (i, 0, 0))],
        out_shape=[jax.ShapeDtypeStruct((b, N_KV, rows, HEAD_DIM), F32),
                   jax.ShapeDtypeStruct((b, SLC_TOPK, LANE), jnp.int32)],
        scratch_shapes=[pltpu.VMEM((r, 4 * KV_WIDTH), F32)],
        compiler_params=_cparams(1),
        name="nsa_sample_select",
    )(lh_cache, lh_tail, w2_big, qsel, pool)
    idx = idx[:, :, :N_KV * TOK_PAD].reshape(b, SLC_TOPK, N_KV, TOK_PAD)[:, :, :, :n_tok]
    return o_c, idx.transpose(0, 2, 3, 1).reshape(-1)


def _rows_minor(cache):
    n = cache.ndim
    return cache.transpose(tuple(range(n - 4)) + (n - 3, n - 2, n - 1, n - 4))


def _nsa_sample_attend_kernel(idx_ref, pt_ref, cache_ref, q_ref, oc_ref, gate_ref, snew_ref, win_ref, wnew_ref,
                              o_ref, kvbuf_ref, sem, *, layer, past, n_tok):
    b = pl.program_id(0)
    n_pages = past // PAGE_SIZE
    cur_blk = past // SLC_BLOCK
    per_page = PAGE_SIZE // SLC_BLOCK
    n_sel = SLC_TOPK * PAGE_SIZE

    half = b % 2

    def sel_block(g, t, k, bb=b):
        return idx_ref[((bb * N_KV + g) * n_tok + t) * SLC_TOPK + k]

    def page_copies(bb, hf, g, t, k):
        c = jnp.minimum(sel_block(g, t, k, bb), cur_blk - 1)
        page = pt_ref[bb * n_pages + c // per_page]
        slot = g * n_tok + t
        dst = pl.ds(k * PAGE_SIZE, PAGE_SIZE)
        return tuple(pltpu.make_async_copy(cache_ref.at[layer, page, kv, g], kvbuf_ref.at[hf, slot, kv, :, dst],
                                           sem.at[hf, kv, slot]) for kv in range(2))

    def start_all(bb, hf):
        for g in range(N_KV):
            for t in range(n_tok):
                for k in range(SLC_TOPK):
                    for cp in page_copies(bb, hf, g, t, k):
                        cp.start()

    @pl.when(b == 0)
    def _():
        start_all(b, half)

    @pl.when(b + 1 < pl.num_programs(0))
    def _():
        start_all(b + 1, 1 - half)

    rows = HPG * TOK_PAD
    row = lax.broadcasted_iota(jnp.int32, (rows, 1), 0)
    tok = row & (TOK_PAD - 1)
    pos_q = past + tok
    lane = lax.broadcasted_iota(jnp.int32, (1, n_sel), 1)
    lane_slot = lane // PAGE_SIZE
    lane_blk = (lane & (PAGE_SIZE - 1)) // SLC_BLOCK
    r_new = lax.broadcasted_iota(jnp.int32, (1, TOK_PAD), 1)
    d_new = pos_q - (past + r_new)
    mask_new = (d_new >= 0) & (r_new < n_tok)
    wb = win_ref.shape[-1]
    d_win = pos_q - (past - wb + lax.broadcasted_iota(jnp.int32, (1, wb), 1))
    mask_win = (d_win >= 0) & (d_win <= WINDOW)

    def joint_attend(q, slope, kt_old, vt_old, d_old, mask_old, k_new, v_new):
        s_o = jnp.where(mask_old, _dot(q, kt_old) - slope * d_old.astype(F32), NEG_INF)
        s_n = jnp.where(mask_new, _dot_nt(q, k_new) - slope * d_new.astype(F32), NEG_INF)
        m = jnp.maximum(jnp.max(s_o, axis=-1, keepdims=True), jnp.max(s_n, axis=-1, keepdims=True))
        e_o = jnp.where(mask_old, jnp.exp(s_o - m), 0.0)
        e_n = jnp.where(mask_new, jnp.exp(s_n - m), 0.0)
        l = jnp.sum(e_o, axis=-1, keepdims=True) + jnp.sum(e_n, axis=-1, keepdims=True)
        return (_dot_nt(e_o.astype(BF16), vt_old) + _dot(e_n.astype(BF16), v_new)) / jnp.maximum(l, 1e-30)

    for g in range(N_KV):
        for t in range(n_tok):
            for k in range(SLC_TOPK):
                for cp in page_copies(b, half, g, t, k):
                    cp.wait()

    for g in range(N_KV):
        ksl = slice(g * HEAD_DIM, (g + 1) * HEAD_DIM)
        vsl = slice(KV_WIDTH + g * HEAD_DIM, KV_WIDTH + (g + 1) * HEAD_DIM)
        qg = q_ref[0, g]
        q = qg[:, :HEAD_DIM]
        slope = qg[:, HEAD_DIM:HEAD_DIM + 1].astype(F32)
        k_snew = snew_ref[0][:, ksl].astype(BF16)
        v_snew = snew_ref[0][:, vsl].astype(BF16)
        o_s = jnp.zeros((rows, HEAD_DIM), F32)
        for t in range(n_tok):
            blk_of_lane = jnp.zeros((1, n_sel), jnp.int32)
            for k in range(SLC_TOPK):
                blk_of_lane = jnp.where(lane_slot == k, sel_block(g, t, k), blk_of_lane)
            d_sel = pos_q - (blk_of_lane * SLC_BLOCK + (lane & (SLC_BLOCK - 1)))
            mask_sel = (d_sel >= 0) & (blk_of_lane < cur_blk) & (blk_of_lane % per_page == lane_blk)
            o_t = joint_attend(q, slope, kvbuf_ref[half, g * n_tok + t, 0].astype(BF16),
                               kvbuf_ref[half, g * n_tok + t, 1].astype(BF16), d_sel, mask_sel, k_snew, v_snew)
            o_s = jnp.where(tok == t, o_t, o_s)
        o_w = joint_attend(q, slope, win_ref[0, 0, 0, g].astype(BF16), win_ref[0, 0, 1, g].astype(BF16), d_win,
                           mask_win, wnew_ref[0][:, ksl].astype(BF16), wnew_ref[0][:, vsl].astype(BF16))
        gate = gate_ref[0, g]
        o_ref[0, g] = gate[:, 0:1] * oc_ref[0, g] + gate[:, 1:2] * o_s + gate[:, 2:3] * o_w


def _nsa_sample_attend(idx, page_table, cache_slc_t, cache_win_t, layer, qsel, o_c, gates, slc_new, win_new, past, n_tok):
    b = qsel.shape[0]
    assert past % PAGE_SIZE == 0 and n_tok <= TOK_PAD
    rows = HPG * TOK_PAD
    wb = cache_win_t.shape[-1]
    per_b = lambda shape: pl.BlockSpec((1,) + shape, lambda i, *_: (i,) + (0,) * len(shape))
    grid_spec = pltpu.PrefetchScalarGridSpec(
        num_scalar_prefetch=2,
        grid=(b,),
        in_specs=[pl.BlockSpec(memory_space=pl.ANY),
                  per_b((N_KV, rows, LANE)), per_b((N_KV, rows, HEAD_DIM)), per_b((N_KV, rows, GATE_LANES)),
                  per_b((TOK_PAD, 2 * KV_WIDTH)),
                  pl.BlockSpec((1, 1, 2, N_KV, HEAD_DIM, wb), lambda i, *_: (layer, i, 0, 0, 0, 0)),
                  per_b((TOK_PAD, 2 * KV_WIDTH))],
        out_specs=per_b((N_KV, rows, HEAD_DIM)),
        scratch_shapes=[pltpu.VMEM((2, N_KV * n_tok, 2, HEAD_DIM, SLC_TOPK * PAGE_SIZE), F32),
                        pltpu.SemaphoreType.DMA((2, 2, N_KV * n_tok))])
    return pl.pallas_call(
        functools.partial(_nsa_sample_attend_kernel, layer=layer, past=past, n_tok=n_tok),
        grid_spec=grid_spec,
        out_shape=jax.ShapeDtypeStruct((b, N_KV, rows, HEAD_DIM), F32),
        compiler_params=_cparams(1),
        name="nsa_sample_attend",
    )(idx, page_table.reshape(-1), cache_slc_t, qsel, o_c, gates, slc_new, cache_win_t, win_new)


def _pad_tokens(x, axis):
    pad = [(0, 0)] * x.ndim
    pad[axis] = (0, TOK_PAD - x.shape[axis])
    return jnp.pad(x, pad)


def _sample_layer(x2, p, b, s, layer, cache_cmp_t, cache_slc_t, cache_win_t, cache_win, h0r, h0i, cache_mem,
                  page_table, final_g, final_norm):
    past = page_table.shape[1] * PAGE_SIZE
    assert past % SLC_BLOCK == 0 and s <= CMP_STRIDE
    q, kvc, kvs, kvw, u, mq, gb, gn = _in_proj(x2, p['norm1_g'], p['w_in'])
    lh_cache = _cmp_lh_paged(cache_cmp_t, layer, page_table, p['pe'], p['w1'])
    tail = jnp.pad(kvc.reshape(b, s, 2 * KV_WIDTH), ((0, 0), (0, CHUNKS_PER_PAGE * CMP_STRIDE - s), (0, 0)))
    lh_tail = _cmp_lh(tail.reshape(b * CHUNKS_PER_PAGE, CMP_ROW), p['pe'], p['w1'])
    pos = past + jnp.arange(s, dtype=jnp.int32)
    qsel = _pad_tokens(_query_pack(q.reshape(b, s, NSA_WIDTH), pos), 2).reshape(b, N_KV, HPG * TOK_PAD, LANE)
    o_c, idx = _nsa_sample_select(lh_cache, lh_tail.reshape(b, CHUNKS_PER_PAGE, 4 * KV_WIDTH), p['w2'], qsel, past, s)
    gates = gn[:, :3 * N_HEADS].reshape(b, s, 3, N_KV, HPG).transpose(0, 3, 4, 1, 2)
    gates = jnp.pad(_pad_tokens(gates, 3), ((0, 0),) * 4 + ((0, GATE_LANES - 3),))
    gates = gates.reshape(b, N_KV, HPG * TOK_PAD, GATE_LANES)
    slc_new = kvs.reshape(b, s, 2 * KV_WIDTH)
    win_new = kvw.reshape(b, s, 2 * KV_WIDTH)
    o = _nsa_sample_attend(idx, page_table, cache_slc_t, cache_win_t, layer, qsel, o_c, gates,
                           _pad_tokens(slc_new, 1), _pad_tokens(win_new, 1), past, s)
    o_a = o.reshape(b, N_KV, HPG, TOK_PAD, HEAD_DIM)[:, :, :, :s].transpose(0, 3, 1, 2, 4).reshape(b * s, NSA_WIDTH)
    o_b, hr, hi = _ssm(_pad_tokens(u.reshape(b, s, SSM_WIDTH), 1), h0r, h0i, p['disc'], p['cdr'], p['cdi'], p['d'],
                       p['w_glu'], p['b_glu'], t_valid=s)
    o_c2 = _mem_attend(_pad_tokens(mq.reshape(b, s, MEM_WIDTH), 1), cache_mem, layer)
    x2 = _merge(x2, o_a, o_b[:, :s].reshape(b * s, SSM_WIDTH), o_c2[:, :s].reshape(b * s, MEM_WIDTH), gb, p['w_o'])
    x2 = _ffn(x2, p['norm2_g'], p['w_ff1'], p['w_ff2'], final_g, final_norm)
    kv_shape = (b, s, 2, N_KV, HEAD_DIM)
    new_win = jnp.concatenate([cache_win, kvw.reshape(kv_shape).astype(cache_win.dtype)], axis=1)[:, s:]
    return x2, dict(cmp=kvc.reshape(kv_shape), slc=kvs.reshape(kv_shape), win=new_win, hr=hr, hi=hi)


def kernel(x_prompt, x_sample, cache_cmp_kv, cache_slc_kv, cache_win_kv, state_ssm_re, state_ssm_im, cache_mem_kv,
           page_table, mem_prompt, norm1_g, w_in, cmp_pe, cmp_w1, cmp_w2, ssm_a_re, ssm_a_im, ssm_log_dt, ssm_b_re,
           ssm_b_im, ssm_c_re, ssm_c_im, ssm_d, w_glu, b_glu, mem_norm_g, w_mem_kv, w_o, norm2_g, w_ff1, w_ff2,
           final_norm_g):
    bp, t, _ = x_prompt.shape
    bs, s, _ = x_sample.shape
    depth = w_in.shape[0]
    xp = x_prompt.reshape(bp * t, D_MODEL)
    xs = x_sample.reshape(bs * s, D_MODEL)
    mem2 = mem_prompt.reshape(bp * N_MEM, D_MODEL)
    cache_cmp_t = _rows_minor(cache_cmp_kv)
    cache_slc_t = _rows_minor(cache_slc_kv)
    cache_win_t = _rows_minor(cache_win_kv)
    cache_mem = cache_mem_kv.reshape(depth, bs, N_MEM, 2 * MEM_WIDTH)
    aux_p, aux_s = [], []
    for l in range(depth):
        p = _layer_params(l, norm1_g, w_in, cmp_pe, cmp_w1, cmp_w2, ssm_a_re, ssm_a_im, ssm_log_dt, ssm_b_re,
                          ssm_b_im, ssm_c_re, ssm_c_im, ssm_d, w_glu, b_glu, mem_norm_g, w_mem_kv, w_o, norm2_g,
                          w_ff1, w_ff2)
        last = l == depth - 1
        xp, ap = _prompt_layer(xp, mem2, p, bp, t, final_norm_g, last)
        xs, a_s = _sample_layer(xs, p, bs, s, l, cache_cmp_t, cache_slc_t, cache_win_t, cache_win_kv[l],
                                state_ssm_re[l], state_ssm_im[l], cache_mem, page_table, final_norm_g, last)
        aux_p.append(ap)
        aux_s.append(a_s)
    stack = lambda aux, key: jnp.stack([a[key] for a in aux])
    return (xp.reshape(bp, t, D_MODEL), xs.reshape(bs, s, D_MODEL),
            stack(aux_p, 'cmp'), stack(aux_s, 'cmp'), stack(aux_p, 'slc'), stack(aux_s, 'slc'),
            stack(aux_p, 'win'), stack(aux_s, 'win'), stack(aux_p, 'hr'), stack(aux_p, 'hi'),
            stack(aux_s, 'hr'), stack(aux_s, 'hi'), stack(aux_p, 'mkv'))
```

```python
import functools
import math

import jax
import jax.numpy as jnp
from jax import lax
from jax.experimental import pallas as pl
from jax.experimental.pallas import tpu as pltpu

F32 = jnp.float32
BF16 = jnp.bfloat16

D_MODEL = 1024
PAGE_SIZE = 128
N_HEADS = 8
HEAD_DIM = 64
N_KV = 2
HPG = N_HEADS // N_KV
CMP_STRIDE = 16
CMP_LEN = 2 * CMP_STRIDE
SLC_BLOCK = 64
SLC_TOPK = 16
WINDOW = 512
Q_BLOCK = 128
SSM_WIDTH = D_MODEL // 2
SSM_GROUP = 16
N_SSM_GROUPS = SSM_WIDTH // SSM_GROUP
SSM_STATE = 64
SSM_FLAT = N_SSM_GROUPS * SSM_STATE
N_MEM = 256
MEM_HEADS = 4
MEM_HEAD_DIM = 128
MEM_WIDTH = MEM_HEADS * MEM_HEAD_DIM
D_FF = 4 * D_MODEL
NSA_WIDTH = N_HEADS * HEAD_DIM
KV_WIDTH = N_KV * HEAD_DIM
MIX_WIDTH = NSA_WIDTH + SSM_WIDTH + MEM_WIDTH
OFF_KV = NSA_WIDTH
OFF_GN = OFF_KV + 6 * KV_WIDTH
OFF_U = OFF_GN + 3 * N_HEADS
OFF_MQ = OFF_U + SSM_WIDTH
OFF_GB = OFF_MQ + MEM_WIDTH
IN_WIDTH = OFF_GB + MIX_WIDTH
EPS = 1e-6
NEG_INF = -1e30
FORCED_SCORE = 1e9

LANE = 128
CMP_ROW = CMP_STRIDE * 2 * KV_WIDTH
SSM_GPB = LANE // SSM_GROUP
SSM_SLABS = SSM_WIDTH // LANE
VMEM_LIMIT = 56 * 2**20

ROW_TILE = 512
SMALL_ROW_TILE = 256
FFN_ROW_TILE = 1024
FFN_COL_TILE = 1024
SLC_KEY_TILE = 1024
SSM_CHUNK = 256
PAGES_PER_STEP = 32
RELAYOUT_SLOTS = 8
GATE_LANES = 16


def _cparams(n_grid, vmem=VMEM_LIMIT):
    return pltpu.CompilerParams(dimension_semantics=("arbitrary",) * n_grid, vmem_limit_bytes=vmem)


def _rms(x, g):
    return x * lax.rsqrt(jnp.mean(x * x, axis=-1, keepdims=True) + EPS) * g


def _dot(a, b):
    return jnp.dot(a, b, preferred_element_type=F32)


def _dot_nt(a, b):
    return lax.dot_general(a, b, (((1,), (1,)), ((), ())), preferred_element_type=F32)


def _masked_probs(s, mask):
    s = jnp.where(mask, s, NEG_INF)
    m = jnp.max(s, axis=-1, keepdims=True)
    e = jnp.where(mask, jnp.exp(s - m), 0.0)
    return e / jnp.maximum(jnp.sum(e, axis=-1, keepdims=True), 1e-30)


def _row_tile(n, pref):
    t = min(n, pref)
    assert n % t == 0
    return t


_SEGS = ((0, 512), (512, 768), (768, 1024), (1024, 1280), (1280, 1792), (1792, 2304), (2304, 3840), (3840, 3968))
_SIGMOID_FROM = 2304
_PACKED_WIDTH = 3968


def _pack_w_in(w):
    gn = jnp.pad(w[:, OFF_GN:OFF_U], ((0, 0), (0, LANE - 3 * N_HEADS)))
    return jnp.concatenate([w[:, :OFF_GN], w[:, OFF_U:], gn], axis=1).astype(BF16)


_PROMPT_ROW_SEGS = (1, 4, 5, 6, 7)


def _inproj_kernel(x_ref, g_ref, w_ref, *o_refs, seq_len):
    h = _rms(x_ref[...], g_ref[...]).astype(BF16)
    z = []
    for a, b in _SEGS:
        z.append(_dot(h, w_ref[:, a:b]))
        if a >= _SIGMOID_FROM:
            z[-1] = jax.nn.sigmoid(z[-1])
    if not seq_len:
        for zi, o_ref in zip(z, o_refs):
            o_ref[...] = zi
        return
    for s, o_ref in zip(_PROMPT_ROW_SEGS, o_refs):
        o_ref[...] = z[s]
    kvt_refs = o_refs[len(_PROMPT_ROW_SEGS):len(_PROMPT_ROW_SEGS) + 3]
    qp_ref, kp_ref, vw_ref = o_refs[len(_PROMPT_ROW_SEGS) + 3:]
    for s, o_ref in zip((1, 2, 3), kvt_refs):
        o_ref[0] = z[s].T
    tm = x_ref.shape[0]
    pos = (pl.program_id(0) * tm) % seq_len + lax.broadcasted_iota(jnp.int32, (tm, 1), 0)
    pos_hi = ((pos // SLC_BLOCK) * SLC_BLOCK).astype(F32)
    pos_lo = (pos % SLC_BLOCK).astype(F32)
    lane = lax.broadcasted_iota(jnp.int32, (1, HEAD_DIM), 1)
    terms = jnp.where(lane == 0, pos_hi, jnp.where(lane == 1, pos_lo, jnp.where(lane < 4, 1.0, 0.0)))
    onehot = jnp.where(pos // SLC_BLOCK == lax.broadcasted_iota(jnp.int32, (1, LANE), 1), 1.0, 0.0)
    q, kvs, kvw = z[0], z[2], z[3]
    head = lambda a, i: a[:, i * HEAD_DIM:(i + 1) * HEAD_DIM]
    for g in range(N_KV):
        kp_ref[g] = jnp.concatenate([onehot, head(kvs, g), terms], axis=1).astype(BF16)
        vw_ref[g] = jnp.concatenate([head(kvs, N_KV + g), head(kvw, N_KV + g), head(kvw, g), terms],
                                    axis=1).astype(BF16)
    for hd in range(N_HEADS):
        slope = 2.0 ** (-8.0 * (hd + 1) / N_HEADS)
        qterms = jnp.where(lane < 2, slope,
                           jnp.where(lane == 2, -slope * pos_hi, jnp.where(lane == 3, -slope * pos_lo, 0.0)))
        qp_ref[hd] = jnp.concatenate([head(q, hd) * (HEAD_DIM ** -0.5), qterms], axis=1).astype(BF16)


def _in_proj(x2, g, w_packed, seq_len=0):
    n = x2.shape[0]
    tm = _row_tile(n, ROW_TILE)
    segs = [_SEGS[s] for s in _PROMPT_ROW_SEGS] if seq_len else _SEGS
    out_specs = [pl.BlockSpec((tm, b - a), lambda i: (i, 0)) for a, b in segs]
    out_shape = [jax.ShapeDtypeStruct((n, b - a), F32) for a, b in segs]
    if seq_len:
        assert seq_len % tm == 0
        per_seq = seq_len // tm
        for _ in range(3):
            out_specs.append(pl.BlockSpec((1, 2 * KV_WIDTH, tm), lambda i: (i // per_seq, 0, i % per_seq)))
            out_shape.append(jax.ShapeDtypeStruct((n // seq_len, 2 * KV_WIDTH, seq_len), F32))
        for lead, width in ((N_HEADS, LANE), (N_KV, 2 * LANE), (N_KV, 2 * LANE)):
            out_specs.append(pl.BlockSpec((lead, tm, width), lambda i: (0, i, 0)))
            out_shape.append(jax.ShapeDtypeStruct((lead, n, width), BF16))
    return pl.pallas_call(
        functools.partial(_inproj_kernel, seq_len=seq_len),
        grid=(n // tm,),
        in_specs=[pl.BlockSpec((tm, D_MODEL), lambda i: (i, 0)),
                  pl.BlockSpec((1, D_MODEL), lambda i: (0, 0)),
                  pl.BlockSpec((D_MODEL, _PACKED_WIDTH), lambda i: (0, 0))],
        out_specs=out_specs,
        out_shape=out_shape,
        compiler_params=_cparams(1),
        name="in_proj",
    )(x2, g.reshape(1, D_MODEL), w_packed)


def _norm_mm_kernel(x_ref, g_ref, w_ref, o_ref):
    o_ref[...] = _dot(_rms(x_ref[...], g_ref[...]).astype(BF16), w_ref[...])


def _norm_mm(x2, g, w_bf16):
    n, k = x2.shape
    m = w_bf16.shape[1]
    tm = _row_tile(n, SMALL_ROW_TILE)
    return pl.pallas_call(
        _norm_mm_kernel,
        grid=(n // tm,),
        in_specs=[pl.BlockSpec((tm, k), lambda i: (i, 0)),
                  pl.BlockSpec((1, k), lambda i: (0, 0)),
                  pl.BlockSpec((k, m), lambda i: (0, 0))],
        out_specs=pl.BlockSpec((tm, m), lambda i: (i, 0)),
        out_shape=jax.ShapeDtypeStruct((n, m), F32),
        compiler_params=_cparams(1),
        name="norm_mm",
    )(x2, g.reshape(1, k), w_bf16)


def _expand_cmp_w1(w1):
    w = w1.reshape(2, 2, CMP_STRIDE, HEAD_DIM, HEAD_DIM).astype(BF16)
    blocks = []
    for kv in range(2):
        for g in range(N_KV):
            off = (kv * N_KV + g) * HEAD_DIM
            blocks.append(jnp.pad(w[kv], ((0, 0), (0, 0), (0, 0), (off, 2 * KV_WIDTH - HEAD_DIM - off))))
    return jnp.concatenate(blocks, axis=2)


def _flat_cmp_pe(pe):
    p = pe.reshape(2, 2, CMP_STRIDE, HEAD_DIM)
    p = jnp.broadcast_to(p.transpose(1, 2, 0, 3)[:, :, :, None, :], (2, CMP_STRIDE, 2, N_KV, HEAD_DIM))
    return p.reshape(2, 1, CMP_ROW)


def _expand_cmp_w2(w2):
    eye2 = jnp.eye(2, dtype=w2.dtype)
    big = jnp.einsum('kde,kK,gG->kgdGKe', w2, eye2, eye2)
    return big.reshape(2 * KV_WIDTH, 2 * KV_WIDTH).astype(BF16)


def _cmp_lohi(x, pe_ref, w_ref):
    lo = _dot((x + pe_ref[0]).astype(BF16), w_ref[0])
    hi = _dot((x + pe_ref[1]).astype(BF16), w_ref[1])
    return jnp.concatenate([lo, hi], axis=1)


def _cmp_lh_kernel(x_ref, pe_ref, w_ref, o_ref):
    o_ref[...] = _cmp_lohi(x_ref[...], pe_ref, w_ref)


def _cmp_lh(x2, pe_flat, w1_big):
    n = x2.shape[0]
    tm = _row_tile(n, SMALL_ROW_TILE)
    return pl.pallas_call(
        _cmp_lh_kernel,
        grid=(n // tm,),
        in_specs=[pl.BlockSpec((tm, CMP_ROW), lambda i: (i, 0)),
                  pl.BlockSpec((2, 1, CMP_ROW), lambda i: (0, 0, 0)),
                  pl.BlockSpec((2, CMP_ROW, 2 * KV_WIDTH), lambda i: (0, 0, 0))],
        out_specs=pl.BlockSpec((tm, 4 * KV_WIDTH), lambda i: (i, 0)),
        out_shape=jax.ShapeDtypeStruct((n, 4 * KV_WIDTH), F32),
        compiler_params=_cparams(1),
        name="cmp_lh",
    )(x2, pe_flat, w1_big.reshape(2, CMP_ROW, 2 * KV_WIDTH))


def _cmp_combine(lh, w2_big):
    r = lh.shape[0]
    hi_next = pltpu.roll(lh[:, 2 * KV_WIDTH:], r - 1, 0)
    return _dot(jax.nn.gelu(lh[:, :2 * KV_WIDTH] + hi_next).astype(BF16), w2_big)


def _cmp_combine_kernel(lh_ref, w2_ref, o_ref):
    o_ref[0] = _cmp_combine(lh_ref[0], w2_ref[...])


def _cmp_tokens_prompt(lh, w2_big):
    b, r, _ = lh.shape
    return pl.pallas_call(
        _cmp_combine_kernel,
        grid=(b,),
        in_specs=[pl.BlockSpec((1, r, 4 * KV_WIDTH), lambda i: (i, 0, 0)),
                  pl.BlockSpec((2 * KV_WIDTH, 2 * KV_WIDTH), lambda i: (0, 0))],
        out_specs=pl.BlockSpec((1, r, 2 * KV_WIDTH), lambda i: (i, 0, 0)),
        out_shape=jax.ShapeDtypeStruct((b, r, 2 * KV_WIDTH), F32),
        compiler_params=_cparams(1),
        name="cmp_combine",
    )(lh, w2_big)


def _alibi_slopes():
    return jnp.exp2(-8.0 * (jnp.arange(N_HEADS, dtype=F32) + 1.0) / N_HEADS)


def _query_pack(q, pos):
    b, t, _ = q.shape
    qh = q.reshape(b, t, N_HEADS, HEAD_DIM).transpose(0, 2, 1, 3) * (HEAD_DIM ** -0.5)
    slope = _alibi_slopes()[:, None]
    pos_hi = ((pos // SLC_BLOCK) * SLC_BLOCK).astype(F32)[None, :]
    pos_lo = (pos % SLC_BLOCK).astype(F32)[None, :]
    ones = jnp.ones((1, t), F32)
    c = jnp.stack([slope * ones, slope * ones, -slope * pos_hi, -slope * pos_lo], axis=-1)
    c = jnp.pad(c, ((0, 0), (0, 0), (0, HEAD_DIM - 4)))
    c = jnp.broadcast_to(c[None], (b, N_HEADS, t, HEAD_DIM))
    return jnp.concatenate([qh, c], axis=-1).astype(BF16)


def _key_consts(kpos):
    hi = ((kpos // SLC_BLOCK) * SLC_BLOCK).astype(F32)
    lo = (kpos % SLC_BLOCK).astype(F32)
    one = jnp.ones_like(hi)
    return jnp.pad(jnp.stack([hi, lo, one, one], axis=-1), ((0, 0), (0, HEAD_DIM - 4)))


def _split3(x):
    hi = x.astype(BF16)
    r1 = x - hi.astype(F32)
    mid = r1.astype(BF16)
    lo = (r1 - mid.astype(F32)).astype(BF16)
    return hi, mid, lo


def _pool_blocks_t(ps, pool_t):
    hi, mid, lo = _split3(ps)
    return _dot_nt(pool_t, hi) + _dot_nt(pool_t, mid) + _dot_nt(pool_t, lo)


def _topk_select_cols(score, blk_f, k):
    sel = jnp.zeros(score.shape, F32)
    picks = []
    for _ in range(k):
        m = jnp.max(score, axis=0, keepdims=True)
        idx = jnp.min(jnp.where(score == m, blk_f, 1e9), axis=0, keepdims=True)
        pick = blk_f == idx
        sel = jnp.where(pick, 1.0, sel)
        score = jnp.where(pick, -3e38, score)
        picks.append(idx)
    return sel, picks


def _add_token_bias(s, bias):
    rows, n = s.shape
    q = bias.shape[0]
    return (s.reshape(rows // q, q, n) + bias[None]).reshape(rows, n)


def _biased_probs(s, bias, row_any=None):
    x = _add_token_bias(s, bias)
    e = jnp.exp(x - jnp.max(x, axis=-1, keepdims=True))
    inv = 1.0 / jnp.maximum(jnp.sum(e, axis=-1, keepdims=True), 1e-30)
    if row_any is not None:
        inv = jnp.where(row_any, inv, 0.0)
    return e * inv


def _nsa_prompt_kernel(qp_ref, gn_ref, ckp_ref, kp_ref, vw_ref, pool_ref, o_ref, *, tk):
    j = pl.program_id(2)
    start = j * Q_BLOCK
    rows = HPG * Q_BLOCK
    qp = qp_ref[...].reshape(rows, LANE)
    pos_t = start + lax.broadcasted_iota(jnp.int32, (Q_BLOCK, 1), 0)
    row = lax.broadcasted_iota(jnp.int32, (rows, 1), 0)

    ckp = ckp_ref[0, 0]
    n_c = ckp.shape[0]
    c_end = lax.broadcasted_iota(jnp.int32, (1, n_c), 1) * CMP_STRIDE + (CMP_LEN - 1)
    bias_c = jnp.where(c_end <= pos_t, 0.0, NEG_INF)
    any_c = start + (row & (Q_BLOCK - 1)) >= CMP_LEN - 1
    p_c = _biased_probs(_dot_nt(qp, ckp[:, :LANE]), bias_c, any_c)
    o_c = _dot(p_c.astype(BF16), ckp[:, LANE:LANE + HEAD_DIM])
    ps = p_c[0:Q_BLOCK]
    for h in range(1, HPG):
        ps = ps + p_c[h * Q_BLOCK:(h + 1) * Q_BLOCK]

    wlen = WINDOW + Q_BLOCK
    woff = pl.multiple_of(jnp.maximum(start - WINDOW, 0), Q_BLOCK)
    d_w = pos_t - (woff + lax.broadcasted_iota(jnp.int32, (1, wlen), 1))
    bias_w = jnp.where((d_w >= 0) & (d_w <= WINDOW), 0.0, NEG_INF)
    p_w = _biased_probs(_dot_nt(qp, vw_ref[0, 0, pl.ds(woff, wlen), LANE:2 * LANE]), bias_w)
    o_w = _dot(p_w.astype(BF16), vw_ref[0, 0, pl.ds(woff, wlen), HEAD_DIM:2 * HEAD_DIM])

    imp_t = _pool_blocks_t(ps, pool_ref[...])
    blk = lax.broadcasted_iota(jnp.int32, (LANE, 1), 0)
    cur = (start + lax.broadcasted_iota(jnp.int32, (1, Q_BLOCK), 1)) // SLC_BLOCK
    valid = blk <= cur
    forced = (blk == 0) | (blk == cur) | (blk == cur - 1)
    score = jnp.where(valid & ~forced, imp_t, -1.0)
    sel_t, _ = _topk_select_cols(score, blk.astype(F32), SLC_TOPK - 3)
    mb = jnp.where(((sel_t > 0.0) | forced) & valid, 0.0, NEG_INF).T.astype(BF16)
    qs = jnp.concatenate([jnp.concatenate([mb] * HPG, axis=0), qp], axis=1)

    def tile(i, state, diag):
        m, l, acc = state
        off = pl.multiple_of(i * tk, tk)
        s = _dot_nt(qs, kp_ref[0, 0, pl.ds(off, tk), :])
        if diag:
            kpos = off + lax.broadcasted_iota(jnp.int32, (1, tk), 1)
            s = _add_token_bias(s, jnp.where(kpos <= pos_t, 0.0, NEG_INF))
        m2 = jnp.maximum(m, jnp.max(s, axis=-1, keepdims=True))
        a = jnp.exp(m - m2)
        e = jnp.exp(s - m2)
        l = a * l + jnp.sum(e, axis=-1, keepdims=True)
        acc = a * acc + _dot(e.astype(BF16), vw_ref[0, 0, pl.ds(off, tk), 0:HEAD_DIM])
        return m2, l, acc

    n_full = start // tk
    carry = (jnp.full((rows, 1), NEG_INF, F32), jnp.zeros((rows, 1), F32), jnp.zeros((rows, HEAD_DIM), F32))
    carry = lax.fori_loop(0, n_full, lambda i, c: tile(i, c, False), carry)
    _, l_s, acc_s = tile(n_full, carry, True)
    o_s = acc_s / jnp.maximum(l_s, 1e-30)

    gn = gn_ref[0, 0]
    outs = []
    for h in range(HPG):
        sl = slice(h * Q_BLOCK, (h + 1) * Q_BLOCK)
        outs.append(gn[:, h:h + 1] * o_c[sl] + gn[:, HPG + h:HPG + h + 1] * o_s[sl]
                    + gn[:, 2 * HPG + h:2 * HPG + h + 1] * o_w[sl])
    o_ref[0] = jnp.concatenate(outs, axis=1)


def _gate_pack(gn, b, t):
    g = gn[:, :3 * N_HEADS].reshape(b, t, 3, N_KV, HPG).transpose(0, 3, 1, 2, 4).reshape(b, N_KV, t, 3 * HPG)
    return jnp.pad(g, ((0, 0), (0, 0), (0, 0), (0, GATE_LANES - 3 * HPG)))


def _nsa_prompt(qp, kp, vw, gn, ckv, b, t):
    n_b = t // SLC_BLOCK
    assert n_b <= LANE and t % Q_BLOCK == 0 and t >= WINDOW + Q_BLOCK
    tk = min(SLC_KEY_TILE, t)
    qp = qp.reshape(N_HEADS, b, t, LANE)
    kp = kp.reshape(N_KV, b, t, 2 * LANE)
    vw = vw.reshape(N_KV, b, t, 2 * LANE)
    n_c = t // CMP_STRIDE
    c4 = ckv.reshape(b, n_c, N_KV, 2, HEAD_DIM).transpose(3, 0, 2, 1, 4)
    c_end = jnp.arange(n_c, dtype=jnp.int32) * CMP_STRIDE + (CMP_LEN - 1)
    ckp = jnp.concatenate([c4[0], jnp.broadcast_to(_key_consts(c_end), c4[0].shape), c4[1], jnp.zeros_like(c4[1])],
                          axis=-1).astype(BF16)
    pool = (jnp.arange(LANE)[:, None] == jnp.arange(n_c)[None, :] // (SLC_BLOCK // CMP_STRIDE)).astype(BF16)
    gn2 = _gate_pack(gn, b, t)
    return pl.pallas_call(
        functools.partial(_nsa_prompt_kernel, tk=tk),
        grid=(b, N_KV, t // Q_BLOCK),
        in_specs=[pl.BlockSpec((HPG, 1, Q_BLOCK, LANE), lambda i, g, j: (g, i, j, 0)),
                  pl.BlockSpec((1, 1, Q_BLOCK, GATE_LANES), lambda i, g, j: (i, g, j, 0)),
                  pl.BlockSpec((1, 1, n_c, 2 * LANE), lambda i, g, j: (i, g, 0, 0)),
                  pl.BlockSpec((1, 1, t, 2 * LANE), lambda i, g, j: (g, i, 0, 0)),
                  pl.BlockSpec((1, 1, t, 2 * LANE), lambda i, g, j: (g, i, 0, 0)),
                  pl.BlockSpec((LANE, n_c), lambda i, g, j: (0, 0))],
        out_specs=pl.BlockSpec((1, Q_BLOCK, HPG * HEAD_DIM), lambda i, g, j: (i, j, g)),
        out_shape=jax.ShapeDtypeStruct((b, t, NSA_WIDTH), F32),
        compiler_params=_cparams(3),
        name="nsa_prompt",
    )(qp, gn2, ckp, kp, vw, pool).reshape(b * t, NSA_WIDTH)


def _mem_attend_kernel(q_ref, kv_ref, o_ref):
    q = q_ref[0]
    scale = MEM_HEAD_DIM ** -0.5
    for h in range(MEM_HEADS):
        sl = slice(h * MEM_HEAD_DIM, (h + 1) * MEM_HEAD_DIM)
        if kv_ref.ndim == 6:
            k = kv_ref[0, 0, :, 0, h, :].astype(BF16)
            v = kv_ref[0, 0, :, 1, h, :].astype(BF16)
        else:
            k = kv_ref[0, :, sl].astype(BF16)
            v = kv_ref[0, :, MEM_WIDTH + h * MEM_HEAD_DIM:MEM_WIDTH + (h + 1) * MEM_HEAD_DIM].astype(BF16)
        s = _dot_nt(q[:, sl].astype(BF16), k) * scale
        e = jnp.exp(s - jnp.max(s, axis=-1, keepdims=True))
        p = e / jnp.sum(e, axis=-1, keepdims=True)
        o_ref[0, :, sl] = _dot(p.astype(BF16), v)


def _mem_attend(mq, mkv, layer=None):
    b, t, _ = mq.shape
    tm = _row_tile(t, ROW_TILE)
    if layer is None:
        kv_spec = pl.BlockSpec((1, N_MEM, 2 * MEM_WIDTH), lambda i, j: (i, 0, 0))
    else:
        kv_spec = pl.BlockSpec((1, 1, N_MEM, 2, MEM_HEADS, MEM_HEAD_DIM), lambda i, j: (layer, i, 0, 0, 0, 0))
    return pl.pallas_call(
        _mem_attend_kernel,
        grid=(b, t // tm),
        in_specs=[pl.BlockSpec((1, tm, MEM_WIDTH), lambda i, j: (i, j, 0)), kv_spec],
        out_specs=pl.BlockSpec((1, tm, MEM_WIDTH), lambda i, j: (i, j, 0)),
        out_shape=jax.ShapeDtypeStruct((b, t, MEM_WIDTH), F32),
        compiler_params=_cparams(2),
        name="mem_attend",
    )(mq, mkv)


def _ssm_disc_kernel(ar_ref, ai_ref, ldt_ref, br_ref, bi_ref, abr_ref, abi_ref, bbr_ref, bbi_ref):
    ar = ar_ref[...]
    ai = ai_ref[...]
    dt = jnp.exp(ldt_ref[...])
    mag = jnp.exp(dt * ar)
    abr = mag * jnp.cos(dt * ai)
    abi = mag * jnp.sin(dt * ai)
    den = ar * ar + ai * ai
    fr = ((abr - 1.0) * ar + abi * ai) / den
    fi = (abi * ar - (abr - 1.0) * ai) / den
    br = br_ref[...]
    bi = bi_ref[...]
    abr_ref[...] = abr
    abi_ref[...] = abi
    bbr_ref[...] = fr * br - fi * bi
    bbi_ref[...] = fr * bi + fi * br


def _ssm_discretise(a_re, a_im, log_dt, b_re, b_im):
    rep = lambda x: jnp.repeat(x, SSM_GROUP, axis=0)
    tb = lambda x: x.transpose(0, 2, 1).reshape(SSM_WIDTH, SSM_STATE)
    shp = jax.ShapeDtypeStruct((SSM_WIDTH, SSM_STATE), F32)
    abr, abi, bbr, bbi = pl.pallas_call(
        _ssm_disc_kernel, out_shape=[shp] * 4, name="ssm_disc",
    )(rep(a_re), rep(a_im), rep(jnp.broadcast_to(log_dt[:, None], a_re.shape)), tb(b_re), tb(b_im))
    eye = jnp.eye(SSM_GPB, dtype=F32)

    def in_map(bb):
        bb = bb.reshape(SSM_SLABS, SSM_GPB, SSM_GROUP, SSM_STATE)
        return jnp.einsum('jgcp,gG->jgcGp', bb, eye).reshape(SSM_SLABS, LANE, SSM_GPB * SSM_STATE).astype(BF16)

    flat = lambda x: x[::SSM_GROUP].reshape(1, SSM_FLAT)
    return flat(abr), flat(abi), in_map(bbr), in_map(bbi)


def _ssm_out_map(c):
    eye = jnp.eye(SSM_GPB, dtype=F32)
    c = c.reshape(SSM_SLABS, SSM_GPB, SSM_GROUP, SSM_STATE)
    return jnp.einsum('jgcp,gG->jgpGc', c, eye).reshape(SSM_SLABS, SSM_GPB * SSM_STATE, LANE).astype(BF16)


def _ssm_kernel(u_ref, h0r_ref, h0i_ref, abr_ref, abi_ref, bdr_ref, bdi_ref, cdr_ref, cdi_ref, d_ref,
                wg_ref, bg_ref, o_ref, hr_ref, hi_ref, sr_ref, si_ref, st_ref, *, n_scan):
    c_idx = pl.program_id(1)
    slab = SSM_GPB * SSM_STATE

    @pl.when(c_idx == 0)
    def _():
        st_ref[0:1, :] = h0r_ref[0]
        st_ref[1:2, :] = h0i_ref[0]

    u = u_ref[0]
    ub = u.astype(BF16)
    for j in range(SSM_SLABS):
        uj = ub[:, j * LANE:(j + 1) * LANE]
        sr_ref[:, j * slab:(j + 1) * slab] = _dot(uj, bdr_ref[j])
        si_ref[:, j * slab:(j + 1) * slab] = _dot(uj, bdi_ref[j])

    ar = abr_ref[...]
    ai = abi_ref[...]

    def step(t, carry):
        hr, hi = carry
        nr = ar * hr - ai * hi + sr_ref[pl.ds(t, 1), :]
        ni = ar * hi + ai * hr + si_ref[pl.ds(t, 1), :]
        sr_ref[pl.ds(t, 1), :] = nr
        si_ref[pl.ds(t, 1), :] = ni
        return nr, ni

    hr, hi = lax.fori_loop(0, n_scan, step, (st_ref[0:1, :], st_ref[1:2, :]))
    st_ref[0:1, :] = hr
    st_ref[1:2, :] = hi
    hr_ref[0] = hr
    hi_ref[0] = hi

    ys = []
    for j in range(SSM_SLABS):
        hrj = sr_ref[:, j * slab:(j + 1) * slab].astype(BF16)
        hij = si_ref[:, j * slab:(j + 1) * slab].astype(BF16)
        ys.append(_dot(hrj, cdr_ref[j]) - _dot(hij, cdi_ref[j]))
    y = jnp.concatenate(ys, axis=1) + d_ref[...] * u
    y = jax.nn.gelu(y)
    o_ref[0] = y * jax.nn.sigmoid(_dot(y.astype(BF16), wg_ref[...]) + bg_ref[...])


def _ssm(u, h0r, h0i, disc, cdr, cdi, d, w_glu, b_glu, t_valid=None):
    b, t, _ = u.shape
    tc = _row_tile(t, SSM_CHUNK)
    n_scan = tc if t_valid is None else t_valid
    assert t_valid is None or t == tc
    abr, abi, bdr, bdi = disc
    full = lambda a: pl.BlockSpec(a.shape, lambda i, c: (0,) * a.ndim)
    state_spec = pl.BlockSpec((1, 1, SSM_FLAT), lambda i, c: (i, 0, 0))
    args = (abr, abi, bdr, bdi, cdr, cdi, d.reshape(1, SSM_WIDTH), w_glu.astype(BF16), b_glu.reshape(1, SSM_WIDTH))
    o, hr, hi = pl.pallas_call(
        functools.partial(_ssm_kernel, n_scan=n_scan),
        grid=(b, t // tc),
        in_specs=[pl.BlockSpec((1, tc, SSM_WIDTH), lambda i, c: (i, c, 0)), state_spec, state_spec]
                 + [full(a) for a in args],
        out_specs=[pl.BlockSpec((1, tc, SSM_WIDTH), lambda i, c: (i, c, 0)), state_spec, state_spec],
        out_shape=[jax.ShapeDtypeStruct((b, t, SSM_WIDTH), F32),
                   jax.ShapeDtypeStruct((b, 1, SSM_FLAT), F32),
                   jax.ShapeDtypeStruct((b, 1, SSM_FLAT), F32)],
        scratch_shapes=[pltpu.VMEM((tc, SSM_FLAT), F32), pltpu.VMEM((tc, SSM_FLAT), F32),
                        pltpu.VMEM((8, SSM_FLAT), F32)],
        compiler_params=_cparams(2),
        name="ssm",
    )(u, h0r.reshape(b, 1, SSM_FLAT), h0i.reshape(b, 1, SSM_FLAT), *args)
    shp = (b, N_SSM_GROUPS, SSM_STATE)
    return o, hr.reshape(shp), hi.reshape(shp)


def _merge_kernel(x_ref, oa_ref, ob_ref, oc_ref, gb_ref, w_ref, o_ref):
    acc = x_ref[...]
    for i, r in enumerate((oa_ref, ob_ref, oc_ref)):
        w = r.shape[1]
        off = sum((NSA_WIDTH, SSM_WIDTH, MEM_WIDTH)[:i])
        mixed = (r[...] * gb_ref[:, off:off + w]).astype(BF16)
        acc = acc + _dot(mixed, w_ref[off:off + w, :])
    o_ref[...] = acc


def _merge(x2, o_a, o_b, o_c, gb, w_o_bf16):
    n = x2.shape[0]
    tm = _row_tile(n, ROW_TILE)
    row = lambda w: pl.BlockSpec((tm, w), lambda i: (i, 0))
    return pl.pallas_call(
        _merge_kernel,
        grid=(n // tm,),
        in_specs=[row(D_MODEL), row(NSA_WIDTH), row(SSM_WIDTH), row(MEM_WIDTH), row(MIX_WIDTH),
                  pl.BlockSpec((MIX_WIDTH, D_MODEL), lambda i: (0, 0))],
        out_specs=row(D_MODEL),
        out_shape=jax.ShapeDtypeStruct((n, D_MODEL), F32),
        compiler_params=_cparams(1),
        name="merge",
    )(x2, o_a, o_b, o_c, gb, w_o_bf16)


def _ffn_kernel(x_ref, g_ref, w1_ref, w2_ref, fg_ref, o_ref, h_ref, acc_ref, *, final_norm):
    k = pl.program_id(1)

    @pl.when(k == 0)
    def _():
        h_ref[...] = _rms(x_ref[...], g_ref[...]).astype(BF16)
        acc_ref[...] = jnp.zeros_like(acc_ref)

    a = jnp.square(jnp.maximum(_dot(h_ref[...], w1_ref[...]), 0.0))
    acc_ref[...] += _dot(a.astype(BF16), w2_ref[...])

    @pl.when(k == pl.num_programs(1) - 1)
    def _():
        y = x_ref[...] + acc_ref[...]
        if final_norm:
            y = _rms(y, fg_ref[...])
        o_ref[...] = y


def _ffn(x2, g, w1_bf16, w2_bf16, final_g, final_norm):
    n = x2.shape[0]
    tm = _row_tile(n, FFN_ROW_TILE)
    tf = FFN_COL_TILE
    return pl.pallas_call(
        functools.partial(_ffn_kernel, final_norm=final_norm),
        grid=(n // tm, D_FF // tf),
        in_specs=[pl.BlockSpec((tm, D_MODEL), lambda i, k: (i, 0)),
                  pl.BlockSpec((1, D_MODEL), lambda i, k: (0, 0)),
                  pl.BlockSpec((D_MODEL, tf), lambda i, k: (0, k)),
                  pl.BlockSpec((tf, D_MODEL), lambda i, k: (k, 0)),
                  pl.BlockSpec((1, D_MODEL), lambda i, k: (0, 0))],
        out_specs=pl.BlockSpec((tm, D_MODEL), lambda i, k: (i, 0)),
        out_shape=jax.ShapeDtypeStruct((n, D_MODEL), F32),
        scratch_shapes=[pltpu.VMEM((tm, D_MODEL), BF16), pltpu.VMEM((tm, D_MODEL), F32)],
        compiler_params=_cparams(2),
        name="ffn",
    )(x2, g.reshape(1, D_MODEL), w1_bf16, w2_bf16, final_g.reshape(1, D_MODEL))


def _layer_params(l, norm1_g, w_in, cmp_pe, cmp_w1, cmp_w2, ssm_a_re, ssm_a_im, ssm_log_dt, ssm_b_re, ssm_b_im,
                  ssm_c_re, ssm_c_im, ssm_d, w_glu, b_glu, mem_norm_g, w_mem_kv, w_o, norm2_g, w_ff1, w_ff2):
    return dict(
        norm1_g=norm1_g[l], w_in=_pack_w_in(w_in[l]),
        pe=_flat_cmp_pe(cmp_pe[l]), w1=_expand_cmp_w1(cmp_w1[l]), w2=_expand_cmp_w2(cmp_w2[l]),
        disc=_ssm_discretise(ssm_a_re[l], ssm_a_im[l], ssm_log_dt[l], ssm_b_re[l], ssm_b_im[l]),
        cdr=_ssm_out_map(ssm_c_re[l]), cdi=_ssm_out_map(ssm_c_im[l]), d=ssm_d[l],
        w_glu=w_glu[l], b_glu=b_glu[l], mem_norm_g=mem_norm_g[l], w_mem_kv=w_mem_kv[l].astype(BF16),
        w_o=w_o[l].astype(BF16), norm2_g=norm2_g[l], w_ff1=w_ff1[l].astype(BF16), w_ff2=w_ff2[l].astype(BF16))


def _prompt_layer(x2, mem2, p, b, t, final_g, final_norm):
    kvc, u, mq, gb, gn, kvc_t, kvs_t, kvw_t, qp, kp, vw = _in_proj(x2, p['norm1_g'], p['w_in'], seq_len=t)
    lh = _cmp_lh(kvc.reshape(b * t // CMP_STRIDE, CMP_ROW), p['pe'], p['w1'])
    ckv = _cmp_tokens_prompt(lh.reshape(b, t // CMP_STRIDE, 4 * KV_WIDTH), p['w2'])
    o_a = _nsa_prompt(qp, kp, vw, gn, ckv, b, t)
    zero = jnp.zeros((b, N_SSM_GROUPS, SSM_STATE), F32)
    o_b, hr, hi = _ssm(u.reshape(b, t, SSM_WIDTH), zero, zero, p['disc'], p['cdr'], p['cdi'], p['d'],
                       p['w_glu'], p['b_glu'])
    mkv = _norm_mm(mem2, p['mem_norm_g'], p['w_mem_kv'])
    o_c = _mem_attend(mq.reshape(b, t, MEM_WIDTH), mkv.reshape(b, N_MEM, 2 * MEM_WIDTH))
    x2 = _merge(x2, o_a, o_b.reshape(b * t, SSM_WIDTH), o_c.reshape(b * t, MEM_WIDTH), gb, p['w_o'])
    x2 = _ffn(x2, p['norm2_g'], p['w_ff1'], p['w_ff2'], final_g, final_norm)
    rows_major = lambda a: a.reshape(b, 2, N_KV, HEAD_DIM, a.shape[-1]).transpose(0, 4, 1, 2, 3)
    wl = min(WINDOW, t)
    return x2, dict(cmp=rows_major(kvc_t), slc=rows_major(kvs_t), win=rows_major(kvw_t[:, :, t - wl:]), hr=hr, hi=hi,
                    mkv=mkv.reshape(b, N_MEM, 2, MEM_HEADS, MEM_HEAD_DIM))


CHUNKS_PER_PAGE = PAGE_SIZE // CMP_STRIDE


def _cmp_lh_paged_kernel(pt_ref, cache_ref, pe_ref, w_ref, o_ref, buf_ref, stage_ref, xl_ref, sem, *,
                         pages, layer, n_s, n_groups):
    step = pl.program_id(0)
    slot = step % 2

    def page_copy(grp, i, sl):
        page = pt_ref[grp // n_s, (grp % n_s) * pages + i]
        return pltpu.make_async_copy(cache_ref.at[layer, page], buf_ref.at[sl, i], sem.at[sl, i])

    @pl.when(step == 0)
    def _():
        for i in range(pages):
            page_copy(step, i, slot).start()

    @pl.when(step + 1 < n_groups)
    def _():
        for i in range(pages):
            page_copy(step + 1, i, 1 - slot).start()

    for i in range(pages):
        page_copy(step, i, slot).wait()

    n_stage = stage_ref.shape[0]

    def relayout(it, carry):
        for u in range(n_stage):
            i = it * n_stage + u
            xt = buf_ref[slot, i].reshape(2 * KV_WIDTH, PAGE_SIZE).T
            base = pl.multiple_of(i * CHUNKS_PER_PAGE, CHUNKS_PER_PAGE)
            for h in range(2 * KV_WIDTH // LANE):
                stage_ref[u, h] = xt[:, h * LANE:(h + 1) * LANE]
                for l in range(CMP_STRIDE):
                    xl_ref[l, pl.ds(base, CHUNKS_PER_PAGE), h * LANE:(h + 1) * LANE] = (
                        stage_ref[u, h, pl.ds(l, CHUNKS_PER_PAGE, stride=CMP_STRIDE), :])
        return carry

    lax.fori_loop(0, pages // n_stage, relayout, 0)
    lo = jnp.zeros((pages * CHUNKS_PER_PAGE, 2 * KV_WIDTH), F32)
    hi = jnp.zeros((pages * CHUNKS_PER_PAGE, 2 * KV_WIDTH), F32)
    for l in range(CMP_STRIDE):
        x = xl_ref[l]
        lo = lo + _dot((x + pe_ref[0, l]).astype(BF16), w_ref[0, l])
        hi = hi + _dot((x + pe_ref[1, l]).astype(BF16), w_ref[1, l])
    o_ref[0] = jnp.concatenate([lo, hi], axis=1)


def _cmp_lh_paged(cache_t, layer, page_table, pe_flat, w1_big):
    b, n_pages = page_table.shape
    pages = math.gcd(n_pages, PAGES_PER_STEP)
    rows = pages * CHUNKS_PER_PAGE
    n_s = n_pages // pages
    n_groups = b * n_s
    page_shape = cache_t.shape[2:]

    grid_spec = pltpu.PrefetchScalarGridSpec(
        num_scalar_prefetch=1,
        grid=(n_groups,),
        in_specs=[pl.BlockSpec(memory_space=pl.ANY),
                  pl.BlockSpec((2, CMP_STRIDE, 1, 2 * KV_WIDTH), lambda t, pt: (0, 0, 0, 0)),
                  pl.BlockSpec((2, CMP_STRIDE, 2 * KV_WIDTH, 2 * KV_WIDTH), lambda t, pt: (0, 0, 0, 0))],
        out_specs=pl.BlockSpec((1, rows, 4 * KV_WIDTH), lambda t, pt: (t // n_s, t % n_s, 0)),
        scratch_shapes=[pltpu.VMEM((2, pages) + page_shape, F32),
                        pltpu.VMEM((math.gcd(pages, RELAYOUT_SLOTS), 2 * KV_WIDTH // LANE, PAGE_SIZE, LANE), F32),
                        pltpu.VMEM((CMP_STRIDE, rows, 2 * KV_WIDTH), F32),
                        pltpu.SemaphoreType.DMA((2, pages))])
    return pl.pallas_call(
        functools.partial(_cmp_lh_paged_kernel, pages=pages, layer=layer, n_s=n_s, n_groups=n_groups),
        grid_spec=grid_spec,
        out_shape=jax.ShapeDtypeStruct((b, n_pages * CHUNKS_PER_PAGE, 4 * KV_WIDTH), F32),
        compiler_params=_cparams(1),
        name="cmp_lh_paged",
    )(page_table, cache_t, pe_flat.reshape(2, CMP_STRIDE, 1, 2 * KV_WIDTH), w1_big)


TOK_PAD = 8


def _nsa_sample_select_kernel(lhc_ref, lht_ref, w2_ref, q_ref, pool_ref, oc_ref, idx_ref, lh_ref, *, past, n_tok):
    n_cache = lhc_ref.shape[1]
    n_tail = lht_ref.shape[1]
    r = lh_ref.shape[0]
    lh_ref[0:n_cache, :] = lhc_ref[0]
    lh_ref[n_cache:n_cache + n_tail, :] = lht_ref[0]
    lh_ref[n_cache + n_tail:r, :] = jnp.zeros((r - n_cache - n_tail, lh_ref.shape[1]), F32)
    ckv = _cmp_combine(lh_ref[...], w2_ref[...])

    rows = HPG * TOK_PAD
    row = lax.broadcasted_iota(jnp.int32, (rows, 1), 0)
    pos_q = past + (row & (TOK_PAD - 1))
    ci = lax.broadcasted_iota(jnp.int32, (1, r), 1)
    d_c = pos_q - (ci * CMP_STRIDE + (CMP_LEN - 1))
    n_blk = pool_ref.shape[0]
    ps_all = []
    for g in range(N_KV):
        qg = q_ref[0, g]
        q = qg[:, :HEAD_DIM]
        slope = qg[:, HEAD_DIM:HEAD_DIM + 1].astype(F32)
        ck = ckv[:, g * 2 * HEAD_DIM:g * 2 * HEAD_DIM + HEAD_DIM].astype(BF16)
        cv = ckv[:, g * 2 * HEAD_DIM + HEAD_DIM:(g + 1) * 2 * HEAD_DIM].astype(BF16)
        s = _dot_nt(q, ck) - slope * d_c.astype(F32)
        p_c = _masked_probs(s, d_c >= 0)
        oc_ref[0, g] = _dot(p_c.astype(BF16), cv)
        ps = p_c[0:TOK_PAD]
        for h in range(1, HPG):
            ps = ps + p_c[h * TOK_PAD:(h + 1) * TOK_PAD]
        ps_all.append(ps)
    ps = jnp.concatenate(ps_all + [jnp.zeros((LANE - N_KV * TOK_PAD, r), F32)], axis=0)
    imp_t = _pool_blocks_t(ps, pool_ref[...])
    blk = lax.broadcasted_iota(jnp.int32, (n_blk, 1), 0)
    cur = (past + (lax.broadcasted_iota(jnp.int32, (1, LANE), 1) & (TOK_PAD - 1))) // SLC_BLOCK
    valid = blk <= cur
    forced = (blk == 0) | (blk == cur) | (blk == cur - 1)
    score = jnp.where(valid, jnp.where(forced, FORCED_SCORE, imp_t), -1.0)
    _, picks = _topk_select_cols(score, blk.astype(F32), SLC_TOPK)
    for k, pk in enumerate(picks):
        idx_ref[0, k:k + 1, :] = pk.astype(jnp.int32)


def _nsa_sample_select(lh_cache, lh_tail, w2_big, qsel, past, n_tok):
    b, n_cache, _ = lh_cache.shape
    n_tail = lh_tail.shape[1]
    r = -(-(n_cache + n_tail) // LANE) * LANE
    n_b = past // SLC_BLOCK + 1
    assert n_b >= SLC_TOPK
    n_blk = -(-n_b // 8) * 8
    pool = (jnp.arange(n_blk)[:, None] == jnp.arange(r)[None, :] // (SLC_BLOCK // CMP_STRIDE)).astype(BF16)
    rows = HPG * TOK_PAD
    o_c, idx = pl.pallas_call(
        functools.partial(_nsa_sample_select_kernel, past=past, n_tok=n_tok),
        grid=(b,),
        in_specs=[pl.BlockSpec((1, n_cache, 4 * KV_WIDTH), lambda i: (i, 0, 0)),
                  pl.BlockSpec((1, n_tail, 4 * KV_WIDTH), lambda i: (i, 0, 0)),
                  pl.BlockSpec((2 * KV_WIDTH, 2 * KV_WIDTH), lambda i: (0, 0)),
                  pl.BlockSpec((1, N_KV, rows, LANE), lambda i: (i, 0, 0, 0)),
                  pl.BlockSpec((n_blk, r), lambda i: (0, 0))],
        out_specs=[pl.BlockSpec((1, N_KV, rows, HEAD_DIM), lambda i: (i, 0, 0, 0)),
                   pl.BlockSpec((1, SLC_TOPK, LANE), lambda i: (i, 0, 0))],
        out_shape=[jax.ShapeDtypeStruct((b, N_KV, rows, HEAD_DIM), F32),
                   jax.ShapeDtypeStruct((b, SLC_TOPK, LANE), jnp.int32)],
        scratch_shapes=[pltpu.VMEM((r, 4 * KV_WIDTH), F32)],
        compiler_params=_cparams(1),
        name="nsa_sample_select",
    )(lh_cache, lh_tail, w2_big, qsel, pool)
    idx = idx[:, :, :N_KV * TOK_PAD].reshape(b, SLC_TOPK, N_KV, TOK_PAD)[:, :, :, :n_tok]
    return o_c, idx.transpose(0, 2, 3, 1).reshape(-1)


def _rows_minor(cache):
    n = cache.ndim
    return cache.transpose(tuple(range(n - 4)) + (n - 3, n - 2, n - 1, n - 4))


def _nsa_sample_attend_kernel(idx_ref, pt_ref, cache_ref, q_ref, oc_ref, gate_ref, snew_ref, win_ref, wnew_ref,
                              o_ref, kvbuf_ref, sem, *, layer, past, n_tok):
    b = pl.program_id(0)
    n_pages = past // PAGE_SIZE
    cur_blk = past // SLC_BLOCK
    per_page = PAGE_SIZE // SLC_BLOCK
    n_sel = SLC_TOPK * PAGE_SIZE

    half = b % 2

    def sel_block(g, t, k, bb=b):
        return idx_ref[((bb * N_KV + g) * n_tok + t) * SLC_TOPK + k]

    def page_copies(bb, hf, g, t, k):
        c = jnp.minimum(sel_block(g, t, k, bb), cur_blk - 1)
        page = pt_ref[bb * n_pages + c // per_page]
        slot = g * n_tok + t
        dst = pl.ds(k * PAGE_SIZE, PAGE_SIZE)
        return tuple(pltpu.make_async_copy(cache_ref.at[layer, page, kv, g], kvbuf_ref.at[hf, slot, kv, :, dst],
                                           sem.at[hf, kv, slot]) for kv in range(2))

    def start_all(bb, hf):
        for g in range(N_KV):
            for t in range(n_tok):
                for k in range(SLC_TOPK):
                    for cp in page_copies(bb, hf, g, t, k):
                        cp.start()

    @pl.when(b == 0)
    def _():
        start_all(b, half)

    @pl.when(b + 1 < pl.num_programs(0))
    def _():
        start_all(b + 1, 1 - half)

    rows = HPG * TOK_PAD
    row = lax.broadcasted_iota(jnp.int32, (rows, 1), 0)
    tok = row & (TOK_PAD - 1)
    pos_q = past + tok
    lane = lax.broadcasted_iota(jnp.int32, (1, n_sel), 1)
    lane_slot = lane // PAGE_SIZE
    lane_blk = (lane & (PAGE_SIZE - 1)) // SLC_BLOCK
    r_new = lax.broadcasted_iota(jnp.int32, (1, TOK_PAD), 1)
    d_new = pos_q - (past + r_new)
    mask_new = (d_new >= 0) & (r_new < n_tok)
    wb = win_ref.shape[-1]
    d_win = pos_q - (past - wb + lax.broadcasted_iota(jnp.int32, (1, wb), 1))
    mask_win = (d_win >= 0) & (d_win <= WINDOW)

    def joint_attend(q, slope, kt_old, vt_old, d_old, mask_old, k_new, v_new):
        s_o = jnp.where(mask_old, _dot(q, kt_old) - slope * d_old.astype(F32), NEG_INF)
        s_n = jnp.where(mask_new, _dot_nt(q, k_new) - slope * d_new.astype(F32), NEG_INF)
        m = jnp.maximum(jnp.max(s_o, axis=-1, keepdims=True), jnp.max(s_n, axis=-1, keepdims=True))
        e_o = jnp.where(mask_old, jnp.exp(s_o - m), 0.0)
        e_n = jnp.where(mask_new, jnp.exp(s_n - m), 0.0)
        l = jnp.sum(e_o, axis=-1, keepdims=True) + jnp.sum(e_n, axis=-1, keepdims=True)
        return (_dot_nt(e_o.astype(BF16), vt_old) + _dot(e_n.astype(BF16), v_new)) / jnp.maximum(l, 1e-30)

    for g in range(N_KV):
        for t in range(n_tok):
            for k in range(SLC_TOPK):
                for cp in page_copies(b, half, g, t, k):
                    cp.wait()

    for g in range(N_KV):
        ksl = slice(g * HEAD_DIM, (g + 1) * HEAD_DIM)
        vsl = slice(KV_WIDTH + g * HEAD_DIM, KV_WIDTH + (g + 1) * HEAD_DIM)
        qg = q_ref[0, g]
        q = qg[:, :HEAD_DIM]
        slope = qg[:, HEAD_DIM:HEAD_DIM + 1].astype(F32)
        k_snew = snew_ref[0][:, ksl].astype(BF16)
        v_snew = snew_ref[0][:, vsl].astype(BF16)
        o_s = jnp.zeros((rows, HEAD_DIM), F32)
        for t in range(n_tok):
            blk_of_lane = jnp.zeros((1, n_sel), jnp.int32)
            for k in range(SLC_TOPK):
                blk_of_lane = jnp.where(lane_slot == k, sel_block(g, t, k), blk_of_lane)
            d_sel = pos_q - (blk_of_lane * SLC_BLOCK + (lane & (SLC_BLOCK - 1)))
            mask_sel = (d_sel >= 0) & (blk_of_lane < cur_blk) & (blk_of_lane % per_page == lane_blk)
            o_t = joint_attend(q, slope, kvbuf_ref[half, g * n_tok + t, 0].astype(BF16),
                               kvbuf_ref[half, g * n_tok + t, 1].astype(BF16), d_sel, mask_sel, k_snew, v_snew)
            o_s = jnp.where(tok == t, o_t, o_s)
        o_w = joint_attend(q, slope, win_ref[0, 0, 0, g].astype(BF16), win_ref[0, 0, 1, g].astype(BF16), d_win,
                           mask_win, wnew_ref[0][:, ksl].astype(BF16), wnew_ref[0][:, vsl].astype(BF16))
        gate = gate_ref[0, g]
        o_ref[0, g] = gate[:, 0:1] * oc_ref[0, g] + gate[:, 1:2] * o_s + gate[:, 2:3] * o_w


def _nsa_sample_attend(idx, page_table, cache_slc_t, cache_win_t, layer, qsel, o_c, gates, slc_new, win_new, past, n_tok):
    b = qsel.shape[0]
    assert past % PAGE_SIZE == 0 and n_tok <= TOK_PAD
    rows = HPG * TOK_PAD
    wb = cache_win_t.shape[-1]
    per_b = lambda shape: pl.BlockSpec((1,) + shape, lambda i, *_: (i,) + (0,) * len(shape))
    grid_spec = pltpu.PrefetchScalarGridSpec(
        num_scalar_prefetch=2,
        grid=(b,),
        in_specs=[pl.BlockSpec(memory_space=pl.ANY),
                  per_b((N_KV, rows, LANE)), per_b((N_KV, rows, HEAD_DIM)), per_b((N_KV, rows, GATE_LANES)),
                  per_b((TOK_PAD, 2 * KV_WIDTH)),
                  pl.BlockSpec((1, 1, 2, N_KV, HEAD_DIM, wb), lambda i, *_: (layer, i, 0, 0, 0, 0)),
                  per_b((TOK_PAD, 2 * KV_WIDTH))],
        out_specs=per_b((N_KV, rows, HEAD_DIM)),
        scratch_shapes=[pltpu.VMEM((2, N_KV * n_tok, 2, HEAD_DIM, SLC_TOPK * PAGE_SIZE), F32),
                        pltpu.SemaphoreType.DMA((2, 2, N_KV * n_tok))])
    return pl.pallas_call(
        functools.partial(_nsa_sample_attend_kernel, layer=layer, past=past, n_tok=n_tok),
        grid_spec=grid_spec,
        out_shape=jax.ShapeDtypeStruct((b, N_KV, rows, HEAD_DIM), F32),
        compiler_params=_cparams(1),
        name="nsa_sample_attend",
    )(idx, page_table.reshape(-1), cache_slc_t, qsel, o_c, gates, slc_new, cache_win_t, win_new)


def _pad_tokens(x, axis):
    pad = [(0, 0)] * x.ndim
    pad[axis] = (0, TOK_PAD - x.shape[axis])
    return jnp.pad(x, pad)


def _sample_layer(x2, p, b, s, layer, cache_cmp_t, cache_slc_t, cache_win_t, cache_win, h0r, h0i, cache_mem,
                  page_table, final_g, final_norm):
    past = page_table.shape[1] * PAGE_SIZE
    assert past % SLC_BLOCK == 0 and s <= CMP_STRIDE
    q, kvc, kvs, kvw, u, mq, gb, gn = _in_proj(x2, p['norm1_g'], p['w_in'])
    lh_cache = _cmp_lh_paged(cache_cmp_t, layer, page_table, p['pe'], p['w1'])
    tail = jnp.pad(kvc.reshape(b, s, 2 * KV_WIDTH), ((0, 0), (0, CHUNKS_PER_PAGE * CMP_STRIDE - s), (0, 0)))
    lh_tail = _cmp_lh(tail.reshape(b * CHUNKS_PER_PAGE, CMP_ROW), p['pe'], p['w1'])
    pos = past + jnp.arange(s, dtype=jnp.int32)
    qsel = _pad_tokens(_query_pack(q.reshape(b, s, NSA_WIDTH), pos), 2).reshape(b, N_KV, HPG * TOK_PAD, LANE)
    o_c, idx = _nsa_sample_select(lh_cache, lh_tail.reshape(b, CHUNKS_PER_PAGE, 4 * KV_WIDTH), p['w2'], qsel, past, s)
    gates = gn[:, :3 * N_HEADS].reshape(b, s, 3, N_KV, HPG).transpose(0, 3, 4, 1, 2)
    gates = jnp.pad(_pad_tokens(gates, 3), ((0, 0),) * 4 + ((0, GATE_LANES - 3),))
    gates = gates.reshape(b, N_KV, HPG * TOK_PAD, GATE_LANES)
    slc_new = kvs.reshape(b, s, 2 * KV_WIDTH)
    win_new = kvw.reshape(b, s, 2 * KV_WIDTH)
    o = _nsa_sample_attend(idx, page_table, cache_slc_t, cache_win_t, layer, qsel, o_c, gates,
                           _pad_tokens(slc_new, 1), _pad_tokens(win_new, 1), past, s)
    o_a = o.reshape(b, N_KV, HPG, TOK_PAD, HEAD_DIM)[:, :, :, :s].transpose(0, 3, 1, 2, 4).reshape(b * s, NSA_WIDTH)
    o_b, hr, hi = _ssm(_pad_tokens(u.reshape(b, s, SSM_WIDTH), 1), h0r, h0i, p['disc'], p['cdr'], p['cdi'], p['d'],
                       p['w_glu'], p['b_glu'], t_valid=s)
    o_c2 = _mem_attend(_pad_tokens(mq.reshape(b, s, MEM_WIDTH), 1), cache_mem, layer)
    x2 = _merge(x2, o_a, o_b[:, :s].reshape(b * s, SSM_WIDTH), o_c2[:, :s].reshape(b * s, MEM_WIDTH), gb, p['w_o'])
    x2 = _ffn(x2, p['norm2_g'], p['w_ff1'], p['w_ff2'], final_g, final_norm)
    kv_shape = (b, s, 2, N_KV, HEAD_DIM)
    new_win = jnp.concatenate([cache_win, kvw.reshape(kv_shape).astype(cache_win.dtype)], axis=1)[:, s:]
    return x2, dict(cmp=kvc.reshape(kv_shape), slc=kvs.reshape(kv_shape), win=new_win, hr=hr, hi=hi)


def kernel(x_prompt, x_sample, cache_cmp_kv, cache_slc_kv, cache_win_kv, state_ssm_re, state_ssm_im, cache_mem_kv,
           page_table, mem_prompt, norm1_g, w_in, cmp_pe, cmp_w1, cmp_w2, ssm_a_re, ssm_a_im, ssm_log_dt, ssm_b_re,
           ssm_b_im, ssm_c_re, ssm_c_im, ssm_d, w_glu, b_glu, mem_norm_g, w_mem_kv, w_o, norm2_g, w_ff1, w_ff2,
           final_norm_g):
    bp, t, _ = x_prompt.shape
    bs, s, _ = x_sample.shape
    depth = w_in.shape[0]
    xp = x_prompt.reshape(bp * t, D_MODEL)
    xs = x_sample.reshape(bs * s, D_MODEL)
    mem2 = mem_prompt.reshape(bp * N_MEM, D_MODEL)
    cache_cmp_t = _rows_minor(cache_cmp_kv)
    cache_slc_t = _rows_minor(cache_slc_kv)
    cache_win_t = _rows_minor(cache_win_kv)
    aux_p, aux_s = [], []
    for l in range(depth):
        p = _layer_params(l, norm1_g, w_in, cmp_pe, cmp_w1, cmp_w2, ssm_a_re, ssm_a_im, ssm_log_dt, ssm_b_re,
                          ssm_b_im, ssm_c_re, ssm_c_im, ssm_d, w_glu, b_glu, mem_norm_g, w_mem_kv, w_o, norm2_g,
                          w_ff1, w_ff2)
        last = l == depth - 1
        xp, ap = _prompt_layer(xp, mem2, p, bp, t, final_norm_g, last)
        xs, a_s = _sample_layer(xs, p, bs, s, l, cache_cmp_t, cache_slc_t, cache_win_t, cache_win_kv[l],
                                state_ssm_re[l], state_ssm_im[l], cache_mem_kv, page_table, final_norm_g, last)
        aux_p.append(ap)
        aux_s.append(a_s)
    stack = lambda aux, key: jnp.stack([a[key] for a in aux])
    return (xp.reshape(bp, t, D_MODEL), xs.reshape(bs, s, D_MODEL),
            stack(aux_p, 'cmp'), stack(aux_s, 'cmp'), stack(aux_p, 'slc'), stack(aux_s, 'slc'),
            stack(aux_p, 'win'), stack(aux_s, 'win'), stack(aux_p, 'hr'), stack(aux_p, 'hi'),
            stack(aux_s, 'hr'), stack(aux_s, 'hi'), stack(aux_p, 'mkv'))
```
